```python
import math
import jax
import jax.numpy as jnp
from jax import lax
import numpy as np

D_MODEL = 1024
BATCH = 4
SEQ = 4096
DEPTH = 2

CTX_LEN = 256
GRID_W = 64

RW_HEADS = 16
RW_HEAD_DIM = 64
RW_WIDTH = RW_HEADS * RW_HEAD_DIM
RW_DECAY_LORA = 64
RW_AAA_LORA = 64
RW_GATE_LORA = 160
RW_GN_EPS = 64e-5

SSM_EXPAND = 2
SSM_INNER = SSM_EXPAND * D_MODEL
SSM_HEAD_DIM = 64
SSM_HEADS = SSM_INNER // SSM_HEAD_DIM
SSM_GROUPS = 8
SSM_HEADS_PER_GROUP = SSM_HEADS // SSM_GROUPS
SSM_STATE = 128
SSM_CONV = 5
SSM_CHUNK = 128
SSM_XBC = SSM_INNER + 2 * SSM_GROUPS * SSM_STATE
SSM_NORM_EPS = 1e-5

RW_SPLITS = (RW_WIDTH, RW_DECAY_LORA, RW_DECAY_LORA, RW_WIDTH, RW_WIDTH, RW_AAA_LORA, RW_AAA_LORA, RW_GATE_LORA)
RW_COLS = sum(RW_SPLITS)
SSM_SPLITS = (SSM_INNER, SSM_XBC, 2 * SSM_HEADS)
SSM_COLS = sum(SSM_SPLITS)
GATE_COLS = 2 * D_MODEL
IN_COLS = RW_COLS + SSM_COLS + GATE_COLS

N_EXPERTS = 32
N_EXPERT_GROUPS = 4
EXPERTS_PER_GROUP = N_EXPERTS // N_EXPERT_GROUPS
TOP_K = 2
D_EXPERT = 512
MOE_BLOCK = 128

NORM_EPS = 1e-6
F32 = jnp.float32

kernel_name = "hybrid_rwkv7_mamba2_moe_prefix_dit"


def rms_norm(x, gain, eps=NORM_EPS):
    xf = x.astype(F32)
    y = xf * lax.rsqrt(jnp.mean(xf * xf, axis=-1, keepdims=True) + eps)
    return (y * gain.astype(F32)).astype(x.dtype)


def split_cols(t, sizes):
    return jnp.split(t, np.cumsum(sizes)[:-1].tolist(), axis=-1)


def raster_to_colmajor(t, rows):
    b, n, ch = t.shape
    return t.reshape(b, rows, GRID_W, ch).transpose(0, 2, 1, 3).reshape(b, n, ch)


def colmajor_to_raster(t, rows):
    b, n, ch = t.shape
    return t.reshape(b, GRID_W, rows, ch).transpose(0, 2, 1, 3).reshape(b, n, ch)


def centred_token_shift(s, mu):
    pad = jnp.pad(s, ((0, 0), (1, 1), (0, 0)))
    return s + mu * (0.5 * (pad[:, :-2] + pad[:, 2:]) - s)


def wkv7_scan(r, w, k, v, a, b, s0, reverse, emit):
    def step(S, inp):
        r_t, w_t, k_t, v_t, a_t, b_t = inp
        sa = jnp.einsum("bhij,bhj->bhi", S, a_t)
        S = S * w_t[:, :, None, :] + sa[..., None] * b_t[:, :, None, :] + v_t[..., None] * k_t[:, :, None, :]
        y = jnp.einsum("bhij,bhj->bhi", S, r_t) if emit else None
        return S, y
    xs = tuple(jnp.moveaxis(t, 1, 0) for t in (r, w, k, v, a, b))
    S, ys = lax.scan(step, s0, xs, reverse=reverse)
    return S, (jnp.moveaxis(ys, 0, 1) if emit else None)


def rwkv7_branch(s_ctx, s_lat, shift_mu, w0, w2, a0, a2, g2, k_k, k_a, r_k, ln_w, ln_b, with_ctx_out):
    n_ctx = s_ctx.shape[1]
    s = jnp.concatenate([centred_token_shift(s_ctx, shift_mu), centred_token_shift(s_lat, shift_mu)], axis=1)
    r, w_f, w_b, k, v, a_f, a_b, g = split_cols(s, RW_SPLITS)
    bsz = s.shape[0]
    heads = lambda t: t.astype(F32).reshape(bsz, -1, RW_HEADS, RW_HEAD_DIM)
    kk = heads(k * k_k)
    kk = kk / jnp.maximum(jnp.linalg.norm(kk, axis=-1, keepdims=True), 1e-12)
    r_h, k_h, v_h = heads(r), heads(k), heads(v)
    ka_h = k_a.astype(F32).reshape(RW_HEADS, RW_HEAD_DIM)
    s0 = jnp.zeros((bsz, RW_HEADS, RW_HEAD_DIM, RW_HEAD_DIM), F32)
    y = 0.0
    for d, (w_lora, a_lora, reverse) in enumerate(((w_f, a_f, False), (w_b, a_b, True))):
        w_log = -jax.nn.softplus(-(w0[d] + jnp.tanh(w_lora) @ w2[d])) - 0.5
        decay = heads(jnp.exp(-jnp.exp(w_log.astype(F32))))
        a_rate = heads(jax.nn.sigmoid(a0[d] + a_lora @ a2[d]))
        k_d = k_h * (1.0 + (a_rate - 1.0) * ka_h)
        ins = (r_h, decay, k_d, v_h, -kk, kk * a_rate)
        s_c, y_c = wkv7_scan(*[t[:, :n_ctx] for t in ins], s0, reverse, with_ctx_out)
        _, y_l = wkv7_scan(*[t[:, n_ctx:] for t in ins], s_c, reverse, True)
        y = y + (jnp.concatenate([y_c, y_l], axis=1) if with_ctx_out else y_l)
    keep = slice(None) if with_ctx_out else slice(n_ctx, None)
    mean = jnp.mean(y, axis=-1, keepdims=True)
    var = jnp.mean(jnp.square(y - mean), axis=-1, keepdims=True)
    y = ((y - mean) * lax.rsqrt(var + RW_GN_EPS)).reshape(bsz, -1, RW_WIDTH) * ln_w + ln_b
    bonus = jnp.sum(r_h[:, keep] * k_h[:, keep] * r_k, axis=-1, keepdims=True) * v_h[:, keep]
    y = y + bonus.reshape(bsz, -1, RW_WIDTH)
    out_gate = jax.nn.sigmoid(g[:, keep]) @ g2
    return (y * out_gate).astype(s_lat.dtype)


def depthwise_conv_silu(t, w, b):
    half = SSM_CONV // 2
    y = lax.conv_general_dilated(t, w[:, None, :].astype(t.dtype), window_strides=(1,), padding=[(half, half)],
                                 dimension_numbers=("NWC", "WIO", "NWC"), feature_group_count=t.shape[-1])
    return jax.nn.silu(y + b)


def ssd_chunked(x, log_decay, b_in, c_in, s0, emit):
    bsz, n_tok, G, E, P = x.shape
    N = b_in.shape[-1]
    nc, L = n_tok // SSM_CHUNK, SSM_CHUNK
    x = x.reshape(bsz, nc, L, G, E, P)
    b_in = b_in.reshape(bsz, nc, L, G, N)
    c_in = c_in.reshape(bsz, nc, L, G, N)
    cum = jnp.cumsum(log_decay.reshape(bsz, nc, L, G, E).transpose(0, 3, 4, 1, 2), axis=-1)
    decay_to_end = jnp.exp(cum[..., -1:] - cum)
    chunk_states = jnp.einsum("bclgn,bgecl,bclgep->bcgepn", b_in, decay_to_end, x)
    chunk_decay = jnp.exp(cum[..., -1])

    def step(S, inp):
        st, dc = inp
        return S * dc[..., None, None] + st, S

    s_final, s_prev = lax.scan(step, s0, (jnp.moveaxis(chunk_states, 1, 0), jnp.moveaxis(chunk_decay, 3, 0)))
    if not emit:
        return s_final, None
    s_prev = jnp.moveaxis(s_prev, 0, 1)
    lower = jnp.tril(jnp.ones((L, L), dtype=bool))
    seg = cum[..., :, None] - cum[..., None, :]
    lmat = jnp.exp(jnp.where(lower, seg, -jnp.inf))
    cb = jnp.einsum("bclgn,bcsgn->bgcls", c_in, b_in)
    y_diag = jnp.einsum("bgecls,bcsgep->bclgep", cb[:, :, None] * lmat, x)
    y_off = jnp.einsum("bclgn,bcgepn,bgecl->bclgep", c_in, s_prev, jnp.exp(cum))
    return s_final, (y_diag + y_off).reshape(bsz, n_tok, G, E, P)


def ssd_scan(x, log_decay, b_in, c_in, s0, reverse, emit):
    if reverse:
        x, log_decay, b_in, c_in = (jnp.flip(t, axis=1) for t in (x, log_decay, b_in, c_in))
    s_final, y = ssd_chunked(x, log_decay, b_in, c_in, s0, emit)
    if reverse and emit:
        y = jnp.flip(y, axis=1)
    return s_final, y


def ssm_branch(s_ctx, s_lat, rows, conv_w, conv_b, dt_bias, a_log, d_skip, norm_w, with_ctx_out):
    n_ctx = s_ctx.shape[1]
    out_dtype = s_lat.dtype
    s_lat = raster_to_colmajor(s_lat, rows)
    z_c, xbc_c, dt_c = split_cols(s_ctx, SSM_SPLITS)
    z_l, xbc_l, dt_l = split_cols(s_lat, SSM_SPLITS)
    xbc = jnp.concatenate([depthwise_conv_silu(xbc_c, conv_w, conv_b), depthwise_conv_silu(xbc_l, conv_w, conv_b)], axis=1)
    dt_raw = jnp.concatenate([dt_c, dt_l], axis=1).astype(F32)
    z = jnp.concatenate([z_c, z_l], axis=1) if with_ctx_out else z_l
    xs, b_in, c_in = split_cols(xbc.astype(F32), (SSM_INNER, SSM_GROUPS * SSM_STATE, SSM_GROUPS * SSM_STATE))
    bsz, n_tok = xs.shape[:2]
    G, E = SSM_GROUPS, SSM_HEADS_PER_GROUP
    xs = xs.reshape(bsz, n_tok, G, E, SSM_HEAD_DIM)
    b_in = b_in.reshape(bsz, n_tok, G, SSM_STATE)
    c_in = c_in.reshape(bsz, n_tok, G, SSM_STATE)
    dt_raw = dt_raw.reshape(bsz, n_tok, 2, SSM_HEADS)
    s0 = jnp.zeros((bsz, G, E, SSM_HEAD_DIM, SSM_STATE), F32)
    y = 0.0
    for d in range(2):
        reverse = d == 1
        dt = jax.nn.softplus(dt_raw[:, :, d] + dt_bias[d].astype(F32)).reshape(bsz, n_tok, G, E)
        log_decay = dt * (-jnp.exp(a_log[d].astype(F32))).reshape(G, E)
        x_dt = xs * dt[..., None]
        s_c, y_c = ssd_scan(x_dt[:, :n_ctx], log_decay[:, :n_ctx], b_in[:, :n_ctx], c_in[:, :n_ctx], s0, reverse, with_ctx_out)
        _, y_l = ssd_scan(x_dt[:, n_ctx:], log_decay[:, n_ctx:], b_in[:, n_ctx:], c_in[:, n_ctx:], s_c, reverse, True)
        y = y + (jnp.concatenate([y_c, y_l], axis=1) if with_ctx_out else y_l)
    keep = slice(None) if with_ctx_out else slice(n_ctx, None)
    y = y + (d_skip[0] + d_skip[1]).astype(F32).reshape(G, E)[..., None] * xs[:, keep]
    y = y.reshape(bsz, -1, SSM_INNER) * jax.nn.silu(z.astype(F32))
    yg = y.reshape(bsz, -1, SSM_GROUPS, SSM_INNER // SSM_GROUPS)
    yg = yg * lax.rsqrt(jnp.mean(yg * yg, axis=-1, keepdims=True) + SSM_NORM_EPS)
    y = (yg.reshape(bsz, -1, SSM_INNER) * norm_w).astype(out_dtype)
    if with_ctx_out:
        return jnp.concatenate([y[:, :n_ctx], colmajor_to_raster(y[:, n_ctx:], rows)], axis=1)
    return colmajor_to_raster(y, rows)


def mixing_block(h_ctx, h_lat, rows, with_ctx_out, w_in, rw_shift_mu, rw_w0, rw_w2, rw_a0, rw_a2, rw_g2,
                 rw_k_k, rw_k_a, rw_r_k, rw_ln_w, rw_ln_b, ssm_conv_w, ssm_conv_b, ssm_dt_bias, ssm_a_log,
                 ssm_d, ssm_norm_w, w_branch_rw, w_branch_ssm, w_out):
    p_ctx = h_ctx @ w_in
    p_lat = h_lat @ w_in
    rw_c, ssm_c, gate_c = split_cols(p_ctx, (RW_COLS, SSM_COLS, GATE_COLS))
    rw_l, ssm_l, gate_l = split_cols(p_lat, (RW_COLS, SSM_COLS, GATE_COLS))
    y_rw = rwkv7_branch(rw_c, rw_l, rw_shift_mu, rw_w0, rw_w2, rw_a0, rw_a2, rw_g2, rw_k_k, rw_k_a,
                        rw_r_k, rw_ln_w, rw_ln_b, with_ctx_out)
    y_ssm = ssm_branch(ssm_c, ssm_l, rows, ssm_conv_w, ssm_conv_b, ssm_dt_bias, ssm_a_log, ssm_d,
                       ssm_norm_w, with_ctx_out)
    gates = jnp.concatenate([gate_c, gate_l], axis=1) if with_ctx_out else gate_l
    g_rw, g_ssm = split_cols(gates, (D_MODEL, D_MODEL))
    merged = jax.nn.sigmoid(g_rw) * (y_rw @ w_branch_rw) + jax.nn.sigmoid(g_ssm) * (y_ssm @ w_branch_ssm)
    return merged @ w_out


def moe_ffn(h, w_router, b_router, w1, w3, w2):
    n_tok, d = h.shape
    scores = jax.nn.sigmoid(h.astype(F32) @ w_router.astype(F32))
    biased = (scores + b_router.astype(F32)).reshape(n_tok, N_EXPERT_GROUPS, EXPERTS_PER_GROUP)
    group_score = lax.top_k(biased, 2)[0].sum(-1)
    top_group = jnp.argmax(group_score, axis=-1)
    in_group = jnp.take_along_axis(biased, top_group[:, None, None], axis=1)[:, 0]
    _, local = lax.top_k(in_group, TOP_K)
    expert = top_group[:, None] * EXPERTS_PER_GROUP + local
    gate = jnp.take_along_axis(scores, expert, axis=-1)
    gate = gate / jnp.sum(gate, axis=-1, keepdims=True)
    n_assign = n_tok * TOP_K
    flat_e = expert.reshape(-1).astype(jnp.int32)
    order = jnp.argsort(flat_e)
    sorted_e = flat_e[order]
    counts = jnp.bincount(flat_e, length=N_EXPERTS)
    padded = (counts + MOE_BLOCK - 1) // MOE_BLOCK * MOE_BLOCK
    pad_start = jnp.cumsum(padded) - padded
    start = jnp.cumsum(counts) - counts
    dest_sorted = (pad_start[sorted_e] + jnp.arange(n_assign) - start[sorted_e]).astype(jnp.int32)
    dest = jnp.zeros((n_assign,), jnp.int32).at[order].set(dest_sorted)
    n_blocks = -(-n_assign // MOE_BLOCK) + N_EXPERTS
    slot_token = jnp.full((n_blocks * MOE_BLOCK,), n_tok, jnp.int32).at[dest].set(
        jnp.arange(n_assign, dtype=jnp.int32) // TOP_K)
    block_expert = jnp.minimum(jnp.searchsorted(jnp.cumsum(padded) // MOE_BLOCK, jnp.arange(n_blocks), side="right"),
                               N_EXPERTS - 1)
    h_pad = jnp.concatenate([h, jnp.zeros((1, d), h.dtype)], axis=0)
    xb = h_pad[slot_token].reshape(n_blocks, MOE_BLOCK, d)

    def expert_block(args):
        xblk, e = args
        return (jax.nn.silu(xblk @ w1[e]) * (xblk @ w3[e])) @ w2[e]

    yb = lax.map(expert_block, (xb, block_expert)).reshape(-1, d)
    y = yb[dest].reshape(n_tok, TOP_K, d)
    return jnp.einsum("tkd,tk->td", y, gate.astype(y.dtype))


def setup_inputs(seed: int = 0) -> dict:
    key = jax.random.key(seed)
    ks = iter(jax.random.split(key, 48))
    nrm = lambda shape, scale: scale * jax.random.normal(next(ks), shape, F32)
    unif = lambda shape, lo, hi: jax.random.uniform(next(ks), shape, F32, lo, hi)
    dt0 = jnp.exp(unif((DEPTH, 2, SSM_HEADS), math.log(1e-3), math.log(1e-1)))
    return {
        "x": nrm((BATCH, SEQ, D_MODEL), 1.0),
        "c": nrm((BATCH, D_MODEL), 1.0),
        "ctx": nrm((BATCH, CTX_LEN, D_MODEL), 1.0),
        "c_ctx": nrm((D_MODEL,), 1.0),
        "w_mod": nrm((DEPTH, D_MODEL, 6 * D_MODEL), 0.5 * D_MODEL ** -0.5),
        "b_mod": nrm((DEPTH, 6 * D_MODEL), 0.01),
        "norm_mix_g": 1.0 + nrm((DEPTH, D_MODEL), 0.05),
        "w_in": nrm((DEPTH, D_MODEL, IN_COLS), D_MODEL ** -0.5),
        "rw_shift_mu": unif((DEPTH, RW_COLS), 0.0, 1.0),
        "rw_w0": unif((DEPTH, 2, RW_WIDTH), -5.0, -0.5),
        "rw_w2": nrm((DEPTH, 2, RW_DECAY_LORA, RW_WIDTH), 0.1),
        "rw_a0": nrm((DEPTH, 2, RW_WIDTH), 0.1),
        "rw_a2": nrm((DEPTH, 2, RW_AAA_LORA, RW_WIDTH), 0.05),
        "rw_g2": nrm((DEPTH, RW_GATE_LORA, RW_WIDTH), RW_GATE_LORA ** -0.5),
        "rw_k_k": 0.85 + nrm((DEPTH, RW_WIDTH), 0.05),
        "rw_k_a": 1.0 + nrm((DEPTH, RW_WIDTH), 0.05),
        "rw_r_k": nrm((DEPTH, RW_HEADS, RW_HEAD_DIM), 0.1),
        "rw_ln_w": 1.0 + nrm((DEPTH, RW_WIDTH), 0.05),
        "rw_ln_b": nrm((DEPTH, RW_WIDTH), 0.01),
        "ssm_conv_w": nrm((DEPTH, SSM_CONV, SSM_XBC), SSM_CONV ** -0.5),
        "ssm_conv_b": nrm((DEPTH, SSM_XBC), 0.01),
        "ssm_dt_bias": dt0 + jnp.log(-jnp.expm1(-dt0)),
        "ssm_a_log": jnp.log(unif((DEPTH, 2, SSM_HEADS), 1.0, 16.0)),
        "ssm_d": 1.0 + nrm((DEPTH, 2, SSM_HEADS), 0.1),
        "ssm_norm_w": 1.0 + nrm((DEPTH, SSM_INNER), 0.05),
        "w_branch_rw": nrm((DEPTH, RW_WIDTH, D_MODEL), RW_WIDTH ** -0.5),
        "w_branch_ssm": nrm((DEPTH, SSM_INNER, D_MODEL), SSM_INNER ** -0.5),
        "w_out": nrm((DEPTH, D_MODEL, D_MODEL), D_MODEL ** -0.5),
        "norm_ffn_g": 1.0 + nrm((DEPTH, D_MODEL), 0.05),
        "w_router": nrm((D_MODEL, N_EXPERTS), D_MODEL ** -0.5),
        "b_router": nrm((N_EXPERTS,), 0.01),
        "exp_w1": nrm((DEPTH, N_EXPERTS, D_MODEL, D_EXPERT), D_MODEL ** -0.5),
        "exp_w3": nrm((DEPTH, N_EXPERTS, D_MODEL, D_EXPERT), D_MODEL ** -0.5),
        "exp_w2": nrm((DEPTH, N_EXPERTS, D_EXPERT, D_MODEL), D_EXPERT ** -0.5),
        "norm_final_g": 1.0 + nrm((D_MODEL,), 0.05),
    }


def reference(x, c, ctx, c_ctx, w_mod, b_mod, norm_mix_g, w_in, rw_shift_mu, rw_w0, rw_w2, rw_a0, rw_a2,
              rw_g2, rw_k_k, rw_k_a, rw_r_k, rw_ln_w, rw_ln_b, ssm_conv_w, ssm_conv_b, ssm_dt_bias, ssm_a_log,
              ssm_d, ssm_norm_w, w_branch_rw, w_branch_ssm, w_out, norm_ffn_g, w_router, b_router,
              exp_w1, exp_w3, exp_w2, norm_final_g):
    bsz, n_lat, d = x.shape
    rows = n_lat // GRID_W
    n_ctx = ctx.shape[1]
    silu_c = jax.nn.silu(c)
    silu_cc = jax.nn.silu(c_ctx)
    for l in range(DEPTH):
        last = l == DEPTH - 1
        shift_m, scale_m, gate_m, shift_f, scale_f, gate_f = split_cols((silu_c @ w_mod[l] + b_mod[l])[:, None, :], (D_MODEL,) * 6)
        cshift_m, cscale_m, cgate_m, cshift_f, cscale_f, cgate_f = split_cols(silu_cc @ w_mod[l] + b_mod[l], (D_MODEL,) * 6)
        h_lat = rms_norm(x, norm_mix_g[l]) * (1 + scale_m) + shift_m
        h_ctx = rms_norm(ctx, norm_mix_g[l]) * (1 + cscale_m) + cshift_m
        mix = mixing_block(h_ctx, h_lat, rows, not last, w_in[l], rw_shift_mu[l], rw_w0[l], rw_w2[l], rw_a0[l],
                           rw_a2[l], rw_g2[l], rw_k_k[l], rw_k_a[l], rw_r_k[l], rw_ln_w[l], rw_ln_b[l],
                           ssm_conv_w[l], ssm_conv_b[l], ssm_dt_bias[l], ssm_a_log[l], ssm_d[l], ssm_norm_w[l],
                           w_branch_rw[l], w_branch_ssm[l], w_out[l])
        if last:
            x = x + gate_m * mix
            h_lat = rms_norm(x, norm_ffn_g[l]) * (1 + scale_f) + shift_f
            f_lat = moe_ffn(h_lat.reshape(-1, d), w_router, b_router, exp_w1[l], exp_w3[l], exp_w2[l])
            x = x + gate_f * f_lat.reshape(x.shape)
        else:
            x = x + gate_m * mix[:, n_ctx:]
            ctx = ctx + cgate_m * mix[:, :n_ctx]
            h_lat = rms_norm(x, norm_ffn_g[l]) * (1 + scale_f) + shift_f
            h_ctx = rms_norm(ctx, norm_ffn_g[l]) * (1 + cscale_f) + cshift_f
            tok = jnp.concatenate([h_ctx.reshape(-1, d), h_lat.reshape(-1, d)], axis=0)
            f = moe_ffn(tok, w_router, b_router, exp_w1[l], exp_w3[l], exp_w2[l])
            ctx = ctx + cgate_f * f[: bsz * n_ctx].reshape(ctx.shape)
            x = x + gate_f * f[bsz * n_ctx:].reshape(x.shape)
    return rms_norm(x, norm_final_g)
```

```python
import functools
import math

import jax
import jax.numpy as jnp
from jax import lax
from jax.experimental import pallas as pl
from jax.experimental.pallas import tpu as pltpu

F32 = jnp.float32
BF16 = jnp.bfloat16
HIGHEST = lax.Precision.HIGHEST

GRID_W = 64
RW_HEADS = 16
RW_HEAD_DIM = 64
RW_GN_EPS = 64e-5
SSM_HEAD_DIM = 64
SSM_GROUPS = 8
SSM_STATE = 128
SSM_CONV = 5
SSM_NORM_EPS = 1e-5
N_EXPERT_GROUPS = 4
TOP_K = 2
NORM_EPS = 1e-6

LANES = 128
WKV_CHUNK = 64
SSD_CHUNK = 128
MOE_ROWS = 256
VMEM_LIMIT = 56 * 1024 * 1024


def _dot(a, b, precision=None):
    return jnp.dot(a, b, preferred_element_type=F32, precision=precision)


def _dot_nt(a, b, precision=None):
    return lax.dot_general(a, b, (((1,), (1,)), ((), ())), preferred_element_type=F32, precision=precision)


def _dot_tn(a, b, precision=None):
    return lax.dot_general(a, b, (((0,), (0,)), ((), ())), preferred_element_type=F32, precision=precision)


def _row_tile(n_rows_per_sample, limit=1088):
    for tm in (1088, 544, 512, 272, 256, 128, 64, 32, 16):
        if tm <= limit and n_rows_per_sample % tm == 0:
            return tm
    raise ValueError(n_rows_per_sample)


def _col_tile(n_cols, limit=1536):
    best = LANES
    for k in range(1, n_cols // LANES + 1):
        tn = k * LANES
        if n_cols % tn == 0 and tn <= limit:
            best = tn
    return best


def _modulated(x, g, sc_l, sh_l, sc_c, sh_c, is_ctx):
    y = x * lax.rsqrt(jnp.mean(x * x, axis=-1, keepdims=True) + NORM_EPS) * g
    return y * (1.0 + jnp.where(is_ctx, sc_c, sc_l)) + jnp.where(is_ctx, sh_c, sh_l)


def _ctx_rows(tm, tiles_per_sample, n_ctx):
    row0 = (pl.program_id(0) % tiles_per_sample) * tm
    return row0 + lax.broadcasted_iota(jnp.int32, (tm, 1), 0) < n_ctx


def _norm_proj_kernel(x_ref, g_ref, scl_ref, shl_ref, scc_ref, shc_ref, w_ref, o_ref, h_ref, *,
                      tm, tiles_per_sample, n_ctx):
    @pl.when(pl.program_id(1) == 0)
    def _():
        is_ctx = _ctx_rows(tm, tiles_per_sample, n_ctx)
        h = _modulated(x_ref[...], g_ref[...], scl_ref[0], shl_ref[0], scc_ref[...], shc_ref[...], is_ctx)
        h_ref[...] = h.astype(h_ref.dtype)

    o_ref[...] = _dot(h_ref[...], w_ref[...]).astype(o_ref.dtype)


def norm_proj(x, gain, scale_l, shift_l, scale_c, shift_c, w, n_ctx, out_dtype=F32):
    bsz, t, d = x.shape
    n = w.shape[1]
    tm = _row_tile(t)
    tps = t // tm
    tn = _col_tile(n)
    row = lambda v: v.reshape(1, d)
    out = pl.pallas_call(
        functools.partial(_norm_proj_kernel, tm=tm, tiles_per_sample=tps, n_ctx=n_ctx),
        grid=(bsz * tps, n // tn),
        in_specs=[
            pl.BlockSpec((tm, d), lambda i, j: (i, 0)),
            pl.BlockSpec((1, d), lambda i, j: (0, 0)),
            pl.BlockSpec((1, 1, d), lambda i, j: (i // tps, 0, 0)),
            pl.BlockSpec((1, 1, d), lambda i, j: (i // tps, 0, 0)),
            pl.BlockSpec((1, d), lambda i, j: (0, 0)),
            pl.BlockSpec((1, d), lambda i, j: (0, 0)),
            pl.BlockSpec((d, tn), lambda i, j: (0, j)),
        ],
        out_specs=pl.BlockSpec((tm, tn), lambda i, j: (i, j)),
        out_shape=jax.ShapeDtypeStruct((bsz * t, n), out_dtype),
        scratch_shapes=[pltpu.VMEM((tm, d), w.dtype)],
        compiler_params=pltpu.CompilerParams(dimension_semantics=("parallel", "arbitrary"),
                                             vmem_limit_bytes=VMEM_LIMIT),
        name="norm_proj",
    )(x.reshape(bsz * t, d), row(gain), scale_l.reshape(bsz, 1, d), shift_l.reshape(bsz, 1, d),
      row(scale_c), row(shift_c), w)
    return out.reshape(bsz, t, n)


def _matmul_kernel(x_ref, w_ref, o_ref, *, act, precision):
    x = x_ref[...]
    if act == "tanh":
        x = jnp.tanh(x)
    elif act == "sigmoid":
        x = jax.nn.sigmoid(x)
    o_ref[...] = _dot(x.astype(w_ref.dtype), w_ref[...], precision).astype(o_ref.dtype)


def matmul(x, w, act=None, precision=None, out_dtype=F32, name="matmul"):
    m, k = x.shape
    n = w.shape[1]
    tm = _row_tile(m) if m >= 16 else m
    tn = _col_tile(n)
    return pl.pallas_call(
        functools.partial(_matmul_kernel, act=act, precision=precision),
        grid=(m // tm, n // tn),
        in_specs=[pl.BlockSpec((tm, k), lambda i, j: (i, 0)), pl.BlockSpec((k, tn), lambda i, j: (0, j))],
        out_specs=pl.BlockSpec((tm, tn), lambda i, j: (i, j)),
        out_shape=jax.ShapeDtypeStruct((m, n), out_dtype),
        compiler_params=pltpu.CompilerParams(dimension_semantics=("parallel", "parallel"),
                                             vmem_limit_bytes=VMEM_LIMIT),
        name=name,
    )(x, w)


def _wkv_kernel(r_ref, lw_ref, k_ref, v_ref, a_ref, b_ref, y_ref, h_ref, *, chunk, reverse):
    c = chunk

    @pl.when(pl.program_id(2) == 0)
    def _():
        h_ref[...] = jnp.zeros_like(h_ref)

    r, lw, k, v, a, b = (ref[0] for ref in (r_ref, lw_ref, k_ref, v_ref, a_ref, b_ref))
    rc = lax.broadcasted_iota(jnp.int32, (c, c), 0)
    cc = lax.broadcasted_iota(jnp.int32, (c, c), 1)
    seen = (rc <= cc) if reverse else (rc >= cc)
    cum = _dot(seen.astype(F32), lw, HIGHEST)
    total = _dot_tn(lw, jnp.ones((c, LANES), F32), HIGHEST)
    e_neg = jnp.exp(-cum)
    rt = r * jnp.exp(cum)
    at = a * jnp.exp(cum - lw)
    bt = b * e_neg
    kt = k * e_neg

    first = lax.broadcasted_iota(jnp.int32, (c, LANES), 1) < RW_HEAD_DIM
    stack = lambda x: jnp.concatenate([jnp.where(first, x, 0.0), jnp.where(first, 0.0, x)], axis=0)
    rs, as_, bs, ks, vs = stack(rt), stack(at), stack(bt), stack(kt), stack(v)

    r2 = lax.broadcasted_iota(jnp.int32, (2 * c, 2 * c), 0)
    c2 = lax.broadcasted_iota(jnp.int32, (2 * c, 2 * c), 1)
    before = (r2 < c2) if reverse else (r2 > c2)
    upto = (r2 <= c2) if reverse else (r2 >= c2)
    l_ab = jnp.where(before, _dot_nt(as_, bs, HIGHEST), 0.0)
    m_ak = jnp.where(before, _dot_nt(as_, ks, HIGHEST), 0.0)
    m_rb = jnp.where(upto, _dot_nt(rs, bs, HIGHEST), 0.0)
    m_rk = jnp.where(upto, _dot_nt(rs, ks, HIGHEST), 0.0)

    inv = jnp.where(r2 == c2, 1.0, 0.0) + l_ab
    power = l_ab
    for _ in range(int(math.log2(c)) - 1):
        power = _dot(power, power, HIGHEST)
        inv = inv + _dot(inv, power, HIGHEST)

    h0 = h_ref[...]
    u = _dot(inv, _dot(as_, h0, HIGHEST) + _dot(m_ak, vs, HIGHEST), HIGHEST)
    y = _dot(rs, h0, HIGHEST) + _dot(m_rb, u, HIGHEST) + _dot(m_rk, vs, HIGHEST)
    y_ref[0] = y[:c] + y[c:]

    e_tot = jnp.exp(total)
    e_tot_keys = jnp.exp(cum[0:1] if reverse else cum[c - 1:c])
    h_ref[...] = (h0 * e_tot + _dot_tn(bs * e_tot_keys, u, HIGHEST) + _dot_tn(ks * e_tot_keys, vs, HIGHEST))


def wkv7(r, lw, k, v, a, b, n_ctx, reverse):
    bsz, t, width = r.shape
    c = WKV_CHUNK
    n_cc, n_chunks = n_ctx // c, t // c

    def chunk_of(s):
        if not reverse:
            return s
        return jnp.where(s < n_cc, n_cc - 1 - s, n_chunks - 1 + n_cc - s)

    spec = pl.BlockSpec((1, c, LANES), lambda i, p, s: (i, chunk_of(s), p))
    return pl.pallas_call(
        functools.partial(_wkv_kernel, chunk=c, reverse=reverse),
        grid=(bsz, width // LANES, n_chunks),
        in_specs=[spec] * 6,
        out_specs=spec,
        out_shape=jax.ShapeDtypeStruct((bsz, t, width), F32),
        scratch_shapes=[pltpu.VMEM((LANES, LANES), F32)],
        compiler_params=pltpu.CompilerParams(dimension_semantics=("parallel", "parallel", "arbitrary"),
                                             vmem_limit_bytes=VMEM_LIMIT),
        name="wkv7_rev" if reverse else "wkv7_fwd",
    )(r, lw, k, v, a, b)


def _ssd_kernel(x_ref, dt_ref, ld_ref, b_ref, c_ref, y_ref, s_ref, *, chunk, heads_per_group, reverse):
    L, hp = chunk, SSM_HEAD_DIM
    e_heads = heads_per_group
    width = e_heads * hp
    n_heads = dt_ref.shape[-1]
    g = pl.program_id(1)

    @pl.when(pl.program_id(2) == 0)
    def _():
        s_ref[...] = jnp.zeros_like(s_ref)

    x, dt, ld, bm, cm = x_ref[0], dt_ref[0], ld_ref[0], b_ref[0], c_ref[0]
    rl = lax.broadcasted_iota(jnp.int32, (L, L), 0)
    cl = lax.broadcasted_iota(jnp.int32, (L, L), 1)
    upto = (rl <= cl) if reverse else (rl >= cl)
    cum = _dot(upto.astype(F32), ld, HIGHEST)

    head_of_lane = g * e_heads + lax.broadcasted_iota(jnp.int32, (n_heads, width), 1) // hp
    expand = (lax.broadcasted_iota(jnp.int32, (n_heads, width), 0) == head_of_lane).astype(F32)
    cum_x = _dot(cum, expand, HIGHEST)
    tot_x = cum_x[0:1] if reverse else cum_x[L - 1:L]
    xdt = x * _dot(dt, expand, HIGHEST)

    cb = _dot_nt(cm, bm, HIGHEST)
    lane_head = lax.broadcasted_iota(jnp.int32, (L, width), 1) // hp
    ones_l = jnp.ones((L, n_heads), F32)
    y = _dot(cm, s_ref[...], HIGHEST) * jnp.exp(cum_x)
    for e in range(e_heads):
        pick = (lax.broadcasted_iota(jnp.int32, (L, n_heads), 1) == g * e_heads + e).astype(F32)
        seg = _dot_nt(cum * pick, ones_l, HIGHEST) - _dot_nt(pick, cum, HIGHEST)
        lmat = jnp.where(upto, jnp.exp(jnp.minimum(seg, 0.0)), 0.0)
        y = y + _dot(cb * lmat, jnp.where(lane_head == e, xdt, 0.0), HIGHEST)
    y_ref[0] = y
    s_ref[...] = s_ref[...] * jnp.exp(tot_x) + _dot_tn(bm, xdt * jnp.exp(tot_x - cum_x), HIGHEST)


def ssd(x, dt, ld, b_in, c_in, n_ctx, reverse):
    bsz, t, inner = x.shape
    L = SSD_CHUNK
    n_heads = dt.shape[-1]
    e_heads = n_heads // SSM_GROUPS
    width = e_heads * SSM_HEAD_DIM
    n_cc, n_chunks = n_ctx // L, t // L

    def chunk_of(s):
        if not reverse:
            return s
        return jnp.where(s < n_cc, n_cc - 1 - s, n_chunks - 1 + n_cc - s)

    return pl.pallas_call(
        functools.partial(_ssd_kernel, chunk=L, heads_per_group=e_heads, reverse=reverse),
        grid=(bsz, SSM_GROUPS, n_chunks),
        in_specs=[
            pl.BlockSpec((1, L, width), lambda i, g, s: (i, chunk_of(s), g)),
            pl.BlockSpec((1, L, n_heads), lambda i, g, s: (i, chunk_of(s), 0)),
            pl.BlockSpec((1, L, n_heads), lambda i, g, s: (i, chunk_of(s), 0)),
            pl.BlockSpec((1, L, SSM_STATE), lambda i, g, s: (i, chunk_of(s), g)),
            pl.BlockSpec((1, L, SSM_STATE), lambda i, g, s: (i, chunk_of(s), g)),
        ],
        out_specs=pl.BlockSpec((1, L, width), lambda i, g, s: (i, chunk_of(s), g)),
        out_shape=jax.ShapeDtypeStruct((bsz, t, inner), F32),
        scratch_shapes=[pltpu.VMEM((SSM_STATE, width), F32)],
        compiler_params=pltpu.CompilerParams(dimension_semantics=("parallel", "parallel", "arbitrary"),
                                             vmem_limit_bytes=VMEM_LIMIT),
        name="ssd_rev" if reverse else "ssd_fwd",
    )(x, dt, ld, b_in, c_in)


def _merge_kernel(x_ref, yrw_ref, z_ref, grw_ref, gssm_ref, gl_ref, gc_ref, wrw_ref, wout_ref, o_ref, *,
                  tm, tiles_per_sample, n_ctx):
    t1 = _dot(yrw_ref[...].astype(wrw_ref.dtype), wrw_ref[...])
    merged = jax.nn.sigmoid(grw_ref[...]) * t1 + jax.nn.sigmoid(gssm_ref[...]) * z_ref[...]
    mix = _dot(merged.astype(wout_ref.dtype), wout_ref[...])
    gate = jnp.where(_ctx_rows(tm, tiles_per_sample, n_ctx), gc_ref[...], gl_ref[0])
    o_ref[...] = x_ref[...] + gate * mix


def merge(x, y_rw, z_ssm, proj, col_grw, col_gssm, gate_l, gate_c, w_rw, w_out, n_ctx):
    bsz, t, d = x.shape
    tm = _row_tile(t, 544)
    tps = t // tm
    m = bsz * t
    rows = lambda blk: pl.BlockSpec((tm, d), lambda i: (i, blk))
    const = lambda shape: pl.BlockSpec(shape, lambda i: (0,) * len(shape))
    out = pl.pallas_call(
        functools.partial(_merge_kernel, tm=tm, tiles_per_sample=tps, n_ctx=n_ctx),
        grid=(m // tm,),
        in_specs=[rows(0), rows(0), rows(0), rows(col_grw // d), rows(col_gssm // d),
                  pl.BlockSpec((1, 1, d), lambda i: (i // tps, 0, 0)), const((1, d)),
                  const(w_rw.shape), const(w_out.shape)],
        out_specs=rows(0),
        out_shape=jax.ShapeDtypeStruct((m, d), F32),
        compiler_params=pltpu.CompilerParams(dimension_semantics=("parallel",), vmem_limit_bytes=VMEM_LIMIT),
        name="merge",
    )(x.reshape(m, d), y_rw.reshape(m, -1), z_ssm.reshape(m, d), proj.reshape(m, -1), proj.reshape(m, -1),
      gate_l.reshape(bsz, 1, d), gate_c.reshape(1, d), w_rw, w_out)
    return out.reshape(bsz, t, d)


def _router_kernel(x_ref, g_ref, scl_ref, shl_ref, scc_ref, shc_ref, wr_ref, h_ref, logit_ref, *,
                   tm, tiles_per_sample, n_ctx):
    is_ctx = _ctx_rows(tm, tiles_per_sample, n_ctx)
    h = _modulated(x_ref[...], g_ref[...], scl_ref[0], shl_ref[0], scc_ref[...], shc_ref[...], is_ctx)
    h_ref[...] = h.astype(h_ref.dtype)
    logit_ref[...] = _dot(h, wr_ref[...], HIGHEST)


def router(x, gain, scale_l, shift_l, scale_c, shift_c, w_router_padded, n_ctx):
    bsz, t, d = x.shape
    tm = _row_tile(t, 544)
    tps = t // tm
    m = bsz * t
    row = lambda v: v.reshape(1, d)
    const = lambda shape: pl.BlockSpec(shape, lambda i: (0,) * len(shape))
    per_sample = pl.BlockSpec((1, 1, d), lambda i: (i // tps, 0, 0))
    return pl.pallas_call(
        functools.partial(_router_kernel, tm=tm, tiles_per_sample=tps, n_ctx=n_ctx),
        grid=(m // tm,),
        in_specs=[pl.BlockSpec((tm, d), lambda i: (i, 0)), const((1, d)), per_sample, per_sample,
                  const((1, d)), const((1, d)), const(w_router_padded.shape)],
        out_specs=[pl.BlockSpec((tm, d), lambda i: (i, 0)), pl.BlockSpec((tm, LANES), lambda i: (i, 0))],
        out_shape=[jax.ShapeDtypeStruct((m, d), BF16), jax.ShapeDtypeStruct((m, LANES), F32)],
        compiler_params=pltpu.CompilerParams(dimension_semantics=("parallel",), vmem_limit_bytes=VMEM_LIMIT),
        name="router",
    )(x.reshape(m, d), row(gain), scale_l.reshape(bsz, 1, d), shift_l.reshape(bsz, 1, d),
      row(scale_c), row(shift_c), w_router_padded)


def _experts_kernel(be_ref, nb_ref, x_ref, w1_ref, w3_ref, w2_ref, o_ref):
    i = pl.program_id(0)

    @pl.when(i < nb_ref[0])
    def _():
        x = x_ref[...]
        hidden = jax.nn.silu(_dot(x, w1_ref[0])) * _dot(x, w3_ref[0])
        o_ref[...] = _dot(hidden.astype(w2_ref.dtype), w2_ref[0])

    @pl.when(i >= nb_ref[0])
    def _():
        o_ref[...] = jnp.zeros_like(o_ref)


def experts(xb, block_expert, n_used, w1, w3, w2):
    n_rows, d = xb.shape
    rows = MOE_ROWS
    n_blocks = n_rows // rows
    de = w1.shape[-1]
    return pl.pallas_call(
        _experts_kernel,
        grid_spec=pltpu.PrefetchScalarGridSpec(
            num_scalar_prefetch=2,
            grid=(n_blocks,),
            in_specs=[
                pl.BlockSpec((rows, d), lambda i, be, nb: (i, 0)),
                pl.BlockSpec((1, d, de), lambda i, be, nb: (be[i], 0, 0)),
                pl.BlockSpec((1, d, de), lambda i, be, nb: (be[i], 0, 0)),
                pl.BlockSpec((1, de, d), lambda i, be, nb: (be[i], 0, 0)),
            ],
            out_specs=pl.BlockSpec((rows, d), lambda i, be, nb: (i, 0)),
        ),
        out_shape=jax.ShapeDtypeStruct((n_rows, d), F32),
        compiler_params=pltpu.CompilerParams(dimension_semantics=("arbitrary",), vmem_limit_bytes=VMEM_LIMIT),
        name="experts",
    )(block_expert, n_used, xb, w1, w3, w2)


def moe(h, logits, b_router, w1, w3, w2):
    n_tok, d = h.shape
    n_exp = b_router.shape[0]
    epg = n_exp // N_EXPERT_GROUPS
    scores = jax.nn.sigmoid(logits)
    biased = (scores + b_router.astype(F32)).reshape(n_tok, N_EXPERT_GROUPS, epg)
    group_score = lax.top_k(biased, 2)[0].sum(-1)
    top_group = jnp.argmax(group_score, axis=-1)
    in_group = jnp.take_along_axis(biased, top_group[:, None, None], axis=1)[:, 0]
    _, local = lax.top_k(in_group, TOP_K)
    expert = top_group[:, None] * epg + local
    gate = jnp.take_along_axis(scores, expert, axis=-1)
    gate = gate / jnp.sum(gate, axis=-1, keepdims=True)

    n_assign = n_tok * TOP_K
    flat_e = expert.reshape(-1).astype(jnp.int32)
    onehot = (flat_e[:, None] == jnp.arange(n_exp, dtype=jnp.int32)[None, :]).astype(jnp.int32)
    rank = jnp.sum((jnp.cumsum(onehot, axis=0) - onehot) * onehot, axis=-1)
    counts = jnp.sum(onehot, axis=0)
    padded = (counts + MOE_ROWS - 1) // MOE_ROWS * MOE_ROWS
    pad_end = jnp.cumsum(padded)
    dest = ((pad_end - padded)[flat_e] + rank).astype(jnp.int32)
    n_blocks = -(-n_assign // MOE_ROWS) + n_exp
    slot_token = jnp.full((n_blocks * MOE_ROWS,), n_tok, jnp.int32).at[dest].set(
        jnp.arange(n_assign, dtype=jnp.int32) // TOP_K)
    block_expert = jnp.minimum(
        jnp.searchsorted(pad_end // MOE_ROWS, jnp.arange(n_blocks), side="right"), n_exp - 1).astype(jnp.int32)
    n_used = (pad_end[-1:] // MOE_ROWS).astype(jnp.int32)
    h_pad = jnp.concatenate([h, jnp.zeros((1, d), h.dtype)], axis=0)
    xb = h_pad[slot_token]
    yb = experts(xb, block_expert, n_used, w1, w3, w2)
    y = yb[dest].reshape(n_tok, TOP_K, d)
    return jnp.einsum("tkd,tk->td", y, gate)


def _neighbour(s, offset, n_ctx):
    t = s.shape[1]
    idx = jnp.arange(t)
    src = idx + offset
    same = (src >= 0) & (src < t) & ((src >= n_ctx) == (idx >= n_ctx))
    return jnp.where(same[None, :, None], jnp.roll(s, -offset, axis=1), 0.0)


def _rwkv_branch(p, lay, prm, n_ctx):
    bsz, t, _ = p.shape
    width = RW_HEADS * RW_HEAD_DIM
    def shifted(name):
        s = p[..., lay[name][0]:lay[name][0] + lay[name][1]]
        return s + prm["shift_mu"][name] * (0.5 * (_neighbour(s, -1, n_ctx) + _neighbour(s, 1, n_ctx)) - s)

    r, k, v = shifted("r"), shifted("k"), shifted("v")
    lora, g = shifted("lora"), shifted("g")
    heads = lambda a: a.reshape(bsz, t, RW_HEADS, RW_HEAD_DIM)
    kk = heads(k * prm["k_k"])
    kk = (kk / jnp.maximum(jnp.linalg.norm(kk, axis=-1, keepdims=True), 1e-12)).reshape(bsz, t, width)
    dl, al = prm["w2"].shape[1], prm["a2"].shape[1]
    y = 0.0
    for d in range(2):
        w_lora = lora[..., d * dl:(d + 1) * dl]
        a_lora = lora[..., 2 * dl + d * al:2 * dl + (d + 1) * al]
        w_pre = prm["w0"][d] + matmul(w_lora.reshape(bsz * t, dl), prm["w2"][d], act="tanh", precision=HIGHEST,
                                      name="decay_lora").reshape(bsz, t, width)
        lw = -jnp.exp(-jax.nn.softplus(-w_pre) - 0.5)
        a_rate = jax.nn.sigmoid(prm["a0"][d] + matmul(a_lora.reshape(bsz * t, al), prm["a2"][d], precision=HIGHEST,
                                                      name="rate_lora").reshape(bsz, t, width))
        k_d = k * (1.0 + (a_rate - 1.0) * prm["k_a"])
        y = y + wkv7(r, lw, k_d, v, -kk, kk * a_rate, n_ctx, reverse=(d == 1))
    yh = heads(y)
    mean = jnp.mean(yh, axis=-1, keepdims=True)
    var = jnp.mean(jnp.square(yh - mean), axis=-1, keepdims=True)
    y = ((yh - mean) * lax.rsqrt(var + RW_GN_EPS)).reshape(bsz, t, width) * prm["ln_w"] + prm["ln_b"]
    bonus = jnp.sum(heads(r) * heads(k) * prm["r_k"], axis=-1, keepdims=True) * heads(v)
    y = y + bonus.reshape(bsz, t, width)
    out_gate = matmul(g.reshape(bsz * t, -1), prm["g2"], act="sigmoid", precision=HIGHEST,
                      name="gate_lora").reshape(bsz, t, width)
    return y * out_gate


def _ssm_branch(p, prm, n_ctx):
    bsz, t, _ = p.shape
    inner = prm["norm_w"].shape[0]
    n_heads = prm["dt_bias"].shape[1]
    gn = SSM_GROUPS * SSM_STATE
    z = p[..., :inner]
    xbc = p[..., inner:2 * inner + 2 * gn]
    dt_raw = p[..., 2 * inner + 2 * gn:2 * inner + 2 * gn + 2 * n_heads]
    half = SSM_CONV // 2
    conv = sum(_neighbour(xbc, o - half, n_ctx) * prm["conv_w"][o] for o in range(SSM_CONV))
    xbc = jax.nn.silu(conv + prm["conv_b"])
    xs, b_in, c_in = xbc[..., :inner], xbc[..., inner:inner + gn], xbc[..., inner + gn:]
    y = 0.0
    for d in range(2):
        dt = jax.nn.softplus(dt_raw[..., d * n_heads:(d + 1) * n_heads] + prm["dt_bias"][d])
        ld = dt * (-jnp.exp(prm["a_log"][d]))
        y = y + ssd(xs, dt, ld, b_in, c_in, n_ctx, reverse=(d == 1))
    d_skip = jnp.repeat(prm["d"][0] + prm["d"][1], SSM_HEAD_DIM)
    y = (y + d_skip * xs) * jax.nn.silu(z)
    yg = y.reshape(bsz, t, SSM_GROUPS, inner // SSM_GROUPS)
    yg = yg * lax.rsqrt(jnp.mean(yg * yg, axis=-1, keepdims=True) + SSM_NORM_EPS)
    return yg.reshape(bsz, t, inner) * prm["norm_w"]


def _pad_cols(w, n):
    return jnp.pad(w, ((0, 0), (0, n - w.shape[1])))


def kernel(x, c, ctx, c_ctx, w_mod, b_mod, norm_mix_g, w_in, rw_shift_mu, rw_w0, rw_w2, rw_a0, rw_a2, rw_g2,
           rw_k_k, rw_k_a, rw_r_k, rw_ln_w, rw_ln_b, ssm_conv_w, ssm_conv_b, ssm_dt_bias, ssm_a_log, ssm_d,
           ssm_norm_w, w_branch_rw, w_branch_ssm, w_out, norm_ffn_g, w_router, b_router, exp_w1, exp_w3, exp_w2,
           norm_final_g):
    bsz, n_lat, d = x.shape
    depth = w_in.shape[0]
    n_ctx = ctx.shape[1]
    rows = n_lat // GRID_W
    t = n_ctx + n_lat
    width = RW_HEADS * RW_HEAD_DIM
    dl, al, gl = rw_w2.shape[2], rw_a2.shape[2], rw_g2.shape[1]
    inner = ssm_norm_w.shape[1]
    xbc_w = ssm_conv_w.shape[2]
    n_heads = ssm_dt_bias.shape[2]
    n_exp = w_router.shape[1]

    def to_c(a):
        ch = a.shape[-1]
        lat = a[:, n_ctx:].reshape(bsz, rows, GRID_W, ch).transpose(0, 2, 1, 3).reshape(bsz, n_lat, ch)
        return jnp.concatenate([a[:, :n_ctx], lat], axis=1)

    def to_r(a):
        ch = a.shape[-1]
        lat = a[:, n_ctx:].reshape(bsz, GRID_W, rows, ch).transpose(0, 2, 1, 3).reshape(bsz, n_lat, ch)
        return jnp.concatenate([a[:, :n_ctx], lat], axis=1)

    o = 0
    src = {}
    for name, size in (("r", width), ("w_f", dl), ("w_b", dl), ("k", width), ("v", width), ("a_f", al),
                       ("a_b", al), ("g", gl), ("z", inner), ("xbc", xbc_w), ("dt", 2 * n_heads),
                       ("g_rw", d), ("g_ssm", d)):
        src[name] = (o, size)
        o += size
    take = lambda a, name: a[..., src[name][0]:src[name][0] + src[name][1]]
    lora_w = 2 * dl + 2 * al
    lora_pad = -(-lora_w // LANES) * LANES
    g_pad = -(-gl // LANES) * LANES
    dt_pad = -(-2 * n_heads // LANES) * LANES
    lay = {"r": (0, width), "k": (width, width), "v": (2 * width, width), "g_rw": (3 * width, d),
           "g_ssm": (3 * width + d, d), "lora": (3 * width + 2 * d, lora_w),
           "g": (3 * width + 2 * d + lora_pad, gl)}

    silu_c = jax.nn.silu(c)
    silu_cc = jax.nn.silu(c_ctx)[None, :]
    act = jnp.concatenate([silu_c, silu_cc, jnp.zeros((-(bsz + 1) % 8, d), F32)], axis=0)
    w_router_p = _pad_cols(w_router.astype(F32), LANES)

    xr = jnp.concatenate([ctx, x], axis=1)
    for l in range(depth):
        last = l == depth - 1
        mod = matmul(act, w_mod[l], precision=HIGHEST, name="modulation")[:bsz + 1] + b_mod[l]
        shift_m, scale_m, gate_m, shift_f, scale_f, gate_f = jnp.split(mod[:bsz], 6, axis=-1)
        cshift_m, cscale_m, cgate_m, cshift_f, cscale_f, cgate_f = jnp.split(mod[bsz], 6, axis=-1)

        wl = w_in[l]
        w_r = jnp.concatenate(
            [take(wl, "r"), take(wl, "k"), take(wl, "v"), take(wl, "g_rw"), take(wl, "g_ssm"),
             _pad_cols(jnp.concatenate([take(wl, n) for n in ("w_f", "w_b", "a_f", "a_b")], axis=1), lora_pad),
             _pad_cols(take(wl, "g"), g_pad)], axis=1).astype(BF16)
        w_c = jnp.concatenate([take(wl, "z"), take(wl, "xbc"), _pad_cols(take(wl, "dt"), dt_pad)],
                              axis=1).astype(BF16)
        mods = (norm_mix_g[l], scale_m, shift_m, cscale_m, cshift_m)
        p_r = norm_proj(xr, *mods, w_r, n_ctx)
        p_c = norm_proj(to_c(xr), *mods, w_c, n_ctx)

        mu = rw_shift_mu[l]
        rw_prm = dict(
            shift_mu={"r": take(mu, "r"), "k": take(mu, "k"), "v": take(mu, "v"), "g": take(mu, "g"),
                      "lora": jnp.concatenate([take(mu, n) for n in ("w_f", "w_b", "a_f", "a_b")])},
            w0=rw_w0[l], w2=rw_w2[l], a0=rw_a0[l], a2=rw_a2[l], g2=rw_g2[l],
            k_k=rw_k_k[l], k_a=rw_k_a[l], r_k=rw_r_k[l], ln_w=rw_ln_w[l], ln_b=rw_ln_b[l])
        y_rw = _rwkv_branch(p_r, lay, rw_prm, n_ctx)
        ssm_prm = dict(conv_w=ssm_conv_w[l], conv_b=ssm_conv_b[l], dt_bias=ssm_dt_bias[l], a_log=ssm_a_log[l],
                       d=ssm_d[l], norm_w=ssm_norm_w[l])
        y_ssm = _ssm_branch(p_c, ssm_prm, n_ctx)
        z_ssm = to_r(matmul(y_ssm.reshape(bsz * t, inner), w_branch_ssm[l].astype(BF16),
                            name="ssm_out").reshape(bsz, t, d))
        xr = merge(xr, y_rw, z_ssm, p_r, lay["g_rw"][0], lay["g_ssm"][0], gate_m, cgate_m,
                   w_branch_rw[l].astype(BF16), w_out[l].astype(BF16), n_ctx)

        h, logits = router(xr, norm_ffn_g[l], scale_f, shift_f, cscale_f, cshift_f, w_router_p, n_ctx)
        f = moe(h, logits[:, :n_exp], b_router, exp_w1[l].astype(BF16), exp_w3[l].astype(BF16),
                exp_w2[l].astype(BF16)).reshape(bsz, t, d)
        is_ctx = (jnp.arange(t) < n_ctx)[None, :, None]
        xr = xr + jnp.where(is_ctx, cgate_f[None, None, :], gate_f[:, None, :]) * f
        del last
    xl = xr[:, n_ctx:]
    return xl * lax.rsqrt(jnp.mean(xl * xl, axis=-1, keepdims=True) + NORM_EPS) * norm_final_g
```

```python
import functools
import math

import jax
import jax.numpy as jnp
from jax import lax
from jax.experimental import pallas as pl
from jax.experimental.pallas import tpu as pltpu

F32 = jnp.float32
BF16 = jnp.bfloat16
HIGHEST = lax.Precision.HIGHEST

GRID_W = 64
RW_HEADS = 16
RW_HEAD_DIM = 64
RW_GN_EPS = 64e-5
SSM_HEAD_DIM = 64
SSM_GROUPS = 8
SSM_STATE = 128
SSM_CONV = 5
SSM_NORM_EPS = 1e-5
N_EXPERT_GROUPS = 4
TOP_K = 2
NORM_EPS = 1e-6

LANES = 128
WKV_CHUNK = 64
WKV_PAIRS = 2
SSD_CHUNK = 128
MOE_ROWS = 256
VMEM_LIMIT = 56 * 1024 * 1024


def _dot(a, b, precision=None):
    return jnp.dot(a, b, preferred_element_type=F32, precision=precision)


def _dot_nt(a, b, precision=None):
    return lax.dot_general(a, b, (((1,), (1,)), ((), ())), preferred_element_type=F32, precision=precision)


def _dot_tn(a, b, precision=None):
    return lax.dot_general(a, b, (((0,), (0,)), ((), ())), preferred_element_type=F32, precision=precision)


def _row_tile(n_rows_per_sample, limit=1088):
    for tm in (1088, 544, 512, 272, 256, 128, 64, 32, 16):
        if tm <= limit and n_rows_per_sample % tm == 0:
            return tm
    raise ValueError(n_rows_per_sample)


def _col_tile(n_cols, limit=1536):
    best = LANES
    for k in range(1, n_cols // LANES + 1):
        tn = k * LANES
        if n_cols % tn == 0 and tn <= limit:
            best = tn
    return best


def _mirrored_chunk(s, n_ctx_chunks, n_chunks):
    return jnp.where(s < n_ctx_chunks, n_ctx_chunks - 1 - s, n_chunks - 1 + n_ctx_chunks - s)


def _modulated(x, g, sc_l, sh_l, sc_c, sh_c, is_ctx):
    y = x * lax.rsqrt(jnp.mean(x * x, axis=-1, keepdims=True) + NORM_EPS) * g
    return y * (1.0 + jnp.where(is_ctx, sc_c, sc_l)) + jnp.where(is_ctx, sh_c, sh_l)


def _ctx_rows(tm, tiles_per_sample, n_ctx):
    row0 = (pl.program_id(0) % tiles_per_sample) * tm
    return row0 + lax.broadcasted_iota(jnp.int32, (tm, 1), 0) < n_ctx


def _norm_proj_kernel(x_ref, g_ref, scl_ref, shl_ref, scc_ref, shc_ref, w_ref, o_ref, h_ref, *,
                      tm, tiles_per_sample, n_ctx):
    @pl.when(pl.program_id(1) == 0)
    def _():
        is_ctx = _ctx_rows(tm, tiles_per_sample, n_ctx)
        h = _modulated(x_ref[...], g_ref[...], scl_ref[0], shl_ref[0], scc_ref[...], shc_ref[...], is_ctx)
        h_ref[...] = h.astype(h_ref.dtype)

    o_ref[...] = _dot(h_ref[...], w_ref[...]).astype(o_ref.dtype)


def norm_proj(x, gain, scale_l, shift_l, scale_c, shift_c, w, n_ctx, out_dtype=F32):
    bsz, t, d = x.shape
    n = w.shape[1]
    tm = _row_tile(t)
    tps = t // tm
    tn = _col_tile(n)
    row = lambda v: v.reshape(1, d)
    out = pl.pallas_call(
        functools.partial(_norm_proj_kernel, tm=tm, tiles_per_sample=tps, n_ctx=n_ctx),
        grid=(bsz * tps, n // tn),
        in_specs=[
            pl.BlockSpec((tm, d), lambda i, j: (i, 0)),
            pl.BlockSpec((1, d), lambda i, j: (0, 0)),
            pl.BlockSpec((1, 1, d), lambda i, j: (i // tps, 0, 0)),
            pl.BlockSpec((1, 1, d), lambda i, j: (i // tps, 0, 0)),
            pl.BlockSpec((1, d), lambda i, j: (0, 0)),
            pl.BlockSpec((1, d), lambda i, j: (0, 0)),
            pl.BlockSpec((d, tn), lambda i, j: (0, j)),
        ],
        out_specs=pl.BlockSpec((tm, tn), lambda i, j: (i, j)),
        out_shape=jax.ShapeDtypeStruct((bsz * t, n), out_dtype),
        scratch_shapes=[pltpu.VMEM((tm, d), w.dtype)],
        compiler_params=pltpu.CompilerParams(dimension_semantics=("parallel", "arbitrary"),
                                             vmem_limit_bytes=VMEM_LIMIT),
        name="norm_proj",
    )(x.reshape(bsz * t, d), row(gain), scale_l.reshape(bsz, 1, d), shift_l.reshape(bsz, 1, d),
      row(scale_c), row(shift_c), w)
    return out.reshape(bsz, t, n)


def _matmul_kernel(x_ref, w_ref, o_ref, *, act, precision):
    x = x_ref[...]
    if act == "tanh":
        x = jnp.tanh(x)
    elif act == "sigmoid":
        x = jax.nn.sigmoid(x)
    o_ref[...] = _dot(x.astype(w_ref.dtype), w_ref[...], precision).astype(o_ref.dtype)


def matmul(x, w, act=None, precision=None, out_dtype=F32, name="matmul"):
    m, k = x.shape
    n = w.shape[1]
    tm = _row_tile(m) if m >= 16 else m
    tn = _col_tile(n)
    return pl.pallas_call(
        functools.partial(_matmul_kernel, act=act, precision=precision),
        grid=(m // tm, n // tn),
        in_specs=[pl.BlockSpec((tm, k), lambda i, j: (i, 0)), pl.BlockSpec((k, tn), lambda i, j: (0, j))],
        out_specs=pl.BlockSpec((tm, tn), lambda i, j: (i, j)),
        out_shape=jax.ShapeDtypeStruct((m, n), out_dtype),
        compiler_params=pltpu.CompilerParams(dimension_semantics=("parallel", "parallel"),
                                             vmem_limit_bytes=VMEM_LIMIT),
        name=name,
    )(x, w)


def _wkv_chunks(chains):
    c = chains[0][0].shape[0]
    n = 2 * c
    rev = [ch[8] for ch in chains]
    each = lambda f, *cols: [f(*a) for a in zip(*cols)]
    bf = lambda x: x.astype(BF16)
    cat0 = lambda *xs: jnp.concatenate([bf(x) for x in xs], axis=0)
    cat1 = lambda *xs: jnp.concatenate([bf(x) for x in xs], axis=1)

    rc = lax.broadcasted_iota(jnp.int32, (c, c), 0)
    cc = lax.broadcasted_iota(jnp.int32, (c, c), 1)
    seen = {False: (rc >= cc).astype(F32), True: (rc <= cc).astype(F32)}
    r2 = lax.broadcasted_iota(jnp.int32, (n, n), 0)
    c2 = lax.broadcasted_iota(jnp.int32, (n, n), 1)
    before = {False: r2 > c2, True: r2 < c2}
    upto = {False: r2 >= c2, True: r2 <= c2}
    eye = jnp.where(r2 == c2, 1.0, 0.0)
    first = lax.broadcasted_iota(jnp.int32, (c, LANES), 1) < RW_HEAD_DIM
    stack = lambda x: jnp.concatenate([jnp.where(first, x, 0.0), jnp.where(first, 0.0, x)], axis=0)
    ones = jnp.ones((c, LANES), F32)

    cum = [_dot(seen[ch[8]], ch[4], HIGHEST) for ch in chains]
    total = [_dot_tn(ch[4], ones, HIGHEST) for ch in chains]
    e_neg = each(lambda q: jnp.exp(-q), cum)
    e_end = [jnp.exp(q[0:1] if ch[8] else q[c - 1:c]) for q, ch in zip(cum, chains)]
    rs = [stack(ch[0] * jnp.exp(q)) for ch, q in zip(chains, cum)]
    as_ = [stack(-ch[3] * jnp.exp(q - ch[4])) for ch, q in zip(chains, cum)]
    bs = [stack(ch[3] * ch[5] * en) for ch, en in zip(chains, e_neg)]
    ks = [stack(ch[1] * (1.0 + (ch[5] - 1.0) * ch[6]) * en) for ch, en in zip(chains, e_neg)]
    vs = [stack(ch[2]) for ch in chains]
    h0 = [ch[7] for ch in chains]

    pair = each(lambda a, b, c_, d: _dot_nt(cat0(a, b), cat0(c_, d)), rs, as_, bs, ks)
    m_rb = [jnp.where(upto[v_], p[:n, :n], 0.0) for p, v_ in zip(pair, rev)]
    m_rk = [jnp.where(upto[v_], p[:n, n:], 0.0) for p, v_ in zip(pair, rev)]
    l_ab = [jnp.where(before[v_], p[n:, :n], 0.0) for p, v_ in zip(pair, rev)]
    m_ak = [jnp.where(before[v_], p[n:, n:], 0.0) for p, v_ in zip(pair, rev)]

    inv = each(lambda l: eye + l, l_ab)
    power = each(lambda l: _dot(bf(l), bf(l)), l_ab)
    steps = int(math.log2(c)) - 1
    for it in range(steps):
        if it < steps - 1:
            both = each(lambda i, p: _dot(cat0(i, p), bf(p)), inv, power)
            inv = each(lambda i, b: i + b[:n], inv, both)
            power = each(lambda b: b[n:], both)
        else:
            inv = each(lambda i, p: i + _dot(bf(i), bf(p)), inv, power)

    state_and_v = each(cat0, h0, vs)
    w = each(lambda a, m, sv: _dot(cat1(a, m), sv), as_, m_ak, state_and_v)
    y0 = each(lambda r_, m, sv: _dot(cat1(r_, m), sv), rs, m_rk, state_and_v)
    u = each(lambda i, w_: _dot(bf(i), bf(w_)), inv, w)
    y = each(lambda y_, m, u_: y_ + _dot(bf(m), bf(u_)), y0, m_rb, u)
    decayed = each(lambda b, k_, e: jnp.concatenate([b * e, k_ * e], axis=0).T, bs, ks, e_end)
    h_new = each(lambda h, t_, dc, u_, v_: h * jnp.exp(t_) + _dot(bf(dc), cat0(u_, v_)), h0, total, decayed, u, vs)
    return [q[:c] + q[c:] for q in y], h_new


def _wkv_kernel(rf_ref, kf_ref, vf_ref, kkf_ref, lwf_ref, arf_ref, rb_ref, kb_ref, vb_ref, kkb_ref, lwb_ref,
                arb_ref, ka_ref, yf_ref, yb_ref, h_ref, *, pairs):
    @pl.when(pl.program_id(2) == 0)
    def _():
        h_ref[...] = jnp.zeros_like(h_ref)

    dirs = ((rf_ref, kf_ref, vf_ref, kkf_ref, lwf_ref, arf_ref, yf_ref),
            (rb_ref, kb_ref, vb_ref, kkb_ref, lwb_ref, arb_ref, yb_ref))
    chains, outs = [], []
    for d, (r_ref, k_ref, v_ref, kk_ref, lw_ref, ar_ref, y_ref) in enumerate(dirs):
        for p in range(pairs):
            lanes = slice(p * LANES, (p + 1) * LANES)
            chains.append((r_ref[0, :, lanes], k_ref[0, :, lanes], v_ref[0, :, lanes], kk_ref[0, :, lanes],
                           lw_ref[0, :, lanes], ar_ref[0, :, lanes], ka_ref[:, lanes], h_ref[d, p], d == 1))
            outs.append((y_ref, lanes, d, p))
    ys, hs = _wkv_chunks(chains)
    for (y_ref, lanes, d, p), y, h_new in zip(outs, ys, hs):
        y_ref[0, :, lanes] = y
        h_ref[d, p] = h_new


def wkv7(r, k, v, kk, lw, ar, k_a, n_ctx):
    bsz, t, width = r.shape
    c = WKV_CHUNK
    n_cc, n_chunks = n_ctx // c, t // c
    pairs = WKV_PAIRS
    wb = pairs * LANES
    fwd = pl.BlockSpec((1, c, wb), lambda i, p, s: (i, s, p))
    bwd = pl.BlockSpec((1, c, wb), lambda i, p, s: (i, _mirrored_chunk(s, n_cc, n_chunks), p))
    return pl.pallas_call(
        functools.partial(_wkv_kernel, pairs=pairs),
        grid=(bsz, width // wb, n_chunks),
        in_specs=[fwd] * 6 + [bwd] * 6 + [pl.BlockSpec((1, wb), lambda i, p, s: (0, p))],
        out_specs=[fwd, bwd],
        out_shape=[jax.ShapeDtypeStruct((bsz, t, width), F32)] * 2,
        scratch_shapes=[pltpu.VMEM((2, pairs, LANES, LANES), F32)],
        compiler_params=pltpu.CompilerParams(dimension_semantics=("parallel", "parallel", "arbitrary"),
                                             vmem_limit_bytes=VMEM_LIMIT),
        name="wkv7",
    )(r, k, v, kk, lw[0], ar[0], r, k, v, kk, lw[1], ar[1], k_a.reshape(1, width))


def _bf16_pieces(x, n):
    pieces = []
    for _ in range(n):
        p = x.astype(BF16)
        pieces.append(p)
        x = x - p.astype(F32)
    return pieces


def _ssd_kernel(xf_ref, dtf_ref, xb_ref, dtb_ref, bias_ref, arow_ref, tabw_ref, tabx_ref, yf_ref, yb_ref,
                s_ref, ct_ref, cp_ref, *, inner, heads_per_group):
    L, hp = SSD_CHUNK, SSM_HEAD_DIM
    e_heads = heads_per_group
    width = e_heads * hp
    gn = SSM_GROUPS * SSM_STATE
    n_heads = SSM_GROUPS * e_heads

    @pl.when(pl.program_id(1) == 0)
    def _():
        s_ref[...] = jnp.zeros_like(s_ref)

    rl = lax.broadcasted_iota(jnp.int32, (L, L), 0)
    cl = lax.broadcasted_iota(jnp.int32, (L, L), 1)
    upto = (rl >= cl, rl <= cl)
    dirs = ((xf_ref, dtf_ref, yf_ref), (xb_ref, dtb_ref, yb_ref))
    for d, (_, dt_ref, _) in enumerate(dirs):
        dt_all = jax.nn.softplus(dt_ref[0] + bias_ref[...])
        cum = _dot(upto[d].astype(F32), dt_all * arow_ref[...], HIGHEST)
        ct_ref[d] = cum.T
        for i, piece in enumerate(_bf16_pieces(cum, 3) + _bf16_pieces(dt_all, 2)):
            cp_ref[d, i] = piece

    lane = lax.broadcasted_iota(jnp.int32, (L, LANES), 1)
    lane_head = lax.broadcasted_iota(jnp.int32, (L, width), 1) // hp

    def group(g, carry):
        two = range(2)
        xcol = pl.ds(pl.multiple_of(g * width, width), width)
        bcol = pl.ds(pl.multiple_of(inner + g * SSM_STATE, SSM_STATE), SSM_STATE)
        ccol = pl.ds(pl.multiple_of(inner + gn + g * SSM_STATE, SSM_STATE), SSM_STATE)
        x = [dirs[d][0][0, :, xcol] for d in two]
        bm = [dirs[d][0][0, :, bcol] for d in two]
        cmb = [dirs[d][0][0, :, ccol].astype(BF16) for d in two]
        tabw = [tabw_ref[d, g] for d in two]
        tabx = [tabx_ref[d, g] for d in two]
        cum_w = [_dot(cp_ref[d, 0], tabw[d]) + _dot(cp_ref[d, 1], tabw[d]) + _dot(cp_ref[d, 2], tabw[d])
                 for d in two]
        dt_x = [_dot(cp_ref[d, 3], tabx[d]) + _dot(cp_ref[d, 4], tabx[d]) for d in two]
        s0 = [s_ref[d, g] for d in two]
        cb = [_dot_nt(cmb[d], bm[d].astype(BF16)) for d in two]
        y_off = [_dot(cmb[d], s0[d].astype(BF16)) for d in two]
        bt = [bm[d].T.astype(BF16) for d in two]
        cum_x = [jnp.concatenate(
            [jnp.where(lane < hp, cum_w[d][:, (2 * q) * LANES:(2 * q + 1) * LANES],
                       cum_w[d][:, (2 * q + 1) * LANES:(2 * q + 2) * LANES]) for q in range(e_heads // 2)],
            axis=1) for d in two]
        tot_x = [cum_x[d][0:1] if d == 1 else cum_x[d][L - 1:L] for d in two]
        xdt = [x[d] * dt_x[d] for d in two]
        lmats, xmasked = [], []
        for d in two:
            for e in range(e_heads):
                row = d * n_heads + g * e_heads + e
                seg = cum_w[d][:, e * LANES:(e + 1) * LANES] - ct_ref[d, pl.ds(row, 1), :]
                lmats.append((cb[d] * jnp.where(upto[d], jnp.exp(jnp.minimum(seg, 0.0)), 0.0)).astype(BF16))
                xmasked.append(jnp.where(lane_head == e, xdt[d], 0.0).astype(BF16))
        y_diag = [_dot(m, xm) for m, xm in zip(lmats, xmasked)]
        s_add = [_dot(bt[d], (xdt[d] * jnp.exp(tot_x[d] - cum_x[d])).astype(BF16)) for d in two]
        for d in two:
            y = y_off[d] * jnp.exp(cum_x[d])
            for e in range(e_heads):
                y = y + y_diag[d * e_heads + e]
            dirs[d][2][0, :, xcol] = y
            s_ref[d, g] = s0[d] * jnp.exp(tot_x[d]) + s_add[d]
        return carry

    lax.fori_loop(0, SSM_GROUPS, group, 0)


def ssd(xbc, p_c, dt_col, dt_bias, a_log, n_ctx, inner):
    bsz, t, xw = xbc.shape
    L = SSD_CHUNK
    n_heads = dt_bias.shape[1]
    e_heads = n_heads // SSM_GROUPS
    width = e_heads * SSM_HEAD_DIM
    n_cc, n_chunks = n_ctx // L, t // L
    pad = LANES - 2 * n_heads
    bias = jnp.concatenate([dt_bias[0], dt_bias[1], jnp.zeros((pad,), F32)]).reshape(1, LANES)
    arow = jnp.concatenate([-jnp.exp(a_log[0]), -jnp.exp(a_log[1]), jnp.zeros((pad,), F32)]).reshape(1, LANES)
    head_row = (jnp.arange(2)[:, None, None, None] * n_heads + jnp.arange(SSM_GROUPS)[None, :, None, None] * e_heads)
    src = jnp.arange(LANES)[None, None, :, None]
    tabw = (src == head_row + jnp.arange(e_heads * LANES)[None, None, None, :] // LANES).astype(BF16)
    tabx = (src == head_row + jnp.arange(width)[None, None, None, :] // SSM_HEAD_DIM).astype(BF16)

    mirrored = lambda s: _mirrored_chunk(s, n_cc, n_chunks)
    const = lambda a: pl.BlockSpec(a.shape, lambda i, s: (0,) * a.ndim)
    return pl.pallas_call(
        functools.partial(_ssd_kernel, inner=inner, heads_per_group=e_heads),
        grid=(bsz, n_chunks),
        in_specs=[
            pl.BlockSpec((1, L, xw), lambda i, s: (i, s, 0)),
            pl.BlockSpec((1, L, LANES), lambda i, s: (i, s, dt_col // LANES)),
            pl.BlockSpec((1, L, xw), lambda i, s: (i, mirrored(s), 0)),
            pl.BlockSpec((1, L, LANES), lambda i, s: (i, mirrored(s), dt_col // LANES)),
            const(bias), const(arow), const(tabw), const(tabx),
        ],
        out_specs=[pl.BlockSpec((1, L, inner), lambda i, s: (i, s, 0)),
                   pl.BlockSpec((1, L, inner), lambda i, s: (i, mirrored(s), 0))],
        out_shape=[jax.ShapeDtypeStruct((bsz, t, inner), F32)] * 2,
        scratch_shapes=[pltpu.VMEM((2, SSM_GROUPS, SSM_STATE, width), F32),
                        pltpu.VMEM((2, LANES, L), F32),
                        pltpu.VMEM((2, 5, L, LANES), BF16)],
        compiler_params=pltpu.CompilerParams(dimension_semantics=("parallel", "arbitrary"),
                                             vmem_limit_bytes=VMEM_LIMIT),
        name="ssd",
    )(xbc, p_c, xbc, p_c, bias, arow, tabw, tabx)


def _merge_kernel(x_ref, yrw_ref, z_ref, grw_ref, gssm_ref, gl_ref, gc_ref, wrw_ref, wout_ref, o_ref, *,
                  tm, tiles_per_sample, n_ctx):
    t1 = _dot(yrw_ref[...].astype(wrw_ref.dtype), wrw_ref[...])
    merged = jax.nn.sigmoid(grw_ref[...]) * t1 + jax.nn.sigmoid(gssm_ref[...]) * z_ref[...]
    mix = _dot(merged.astype(wout_ref.dtype), wout_ref[...])
    gate = jnp.where(_ctx_rows(tm, tiles_per_sample, n_ctx), gc_ref[...], gl_ref[0])
    o_ref[...] = x_ref[...] + gate * mix


def merge(x, y_rw, z_ssm, proj, col_grw, col_gssm, gate_l, gate_c, w_rw, w_out, n_ctx):
    bsz, t, d = x.shape
    tm = _row_tile(t, 544)
    tps = t // tm
    m = bsz * t
    rows = lambda blk: pl.BlockSpec((tm, d), lambda i: (i, blk))
    const = lambda shape: pl.BlockSpec(shape, lambda i: (0,) * len(shape))
    out = pl.pallas_call(
        functools.partial(_merge_kernel, tm=tm, tiles_per_sample=tps, n_ctx=n_ctx),
        grid=(m // tm,),
        in_specs=[rows(0), rows(0), rows(0), rows(col_grw // d), rows(col_gssm // d),
                  pl.BlockSpec((1, 1, d), lambda i: (i // tps, 0, 0)), const((1, d)),
                  const(w_rw.shape), const(w_out.shape)],
        out_specs=rows(0),
        out_shape=jax.ShapeDtypeStruct((m, d), F32),
        compiler_params=pltpu.CompilerParams(dimension_semantics=("parallel",), vmem_limit_bytes=VMEM_LIMIT),
        name="merge",
    )(x.reshape(m, d), y_rw.reshape(m, -1), z_ssm.reshape(m, d), proj.reshape(m, -1), proj.reshape(m, -1),
      gate_l.reshape(bsz, 1, d), gate_c.reshape(1, d), w_rw, w_out)
    return out.reshape(bsz, t, d)


def _router_kernel(x_ref, g_ref, scl_ref, shl_ref, scc_ref, shc_ref, wr_ref, h_ref, logit_ref, *,
                   tm, tiles_per_sample, n_ctx):
    is_ctx = _ctx_rows(tm, tiles_per_sample, n_ctx)
    h = _modulated(x_ref[...], g_ref[...], scl_ref[0], shl_ref[0], scc_ref[...], shc_ref[...], is_ctx)
    h_ref[...] = h.astype(h_ref.dtype)
    logit_ref[...] = _dot(h, wr_ref[...], HIGHEST)


def router(x, gain, scale_l, shift_l, scale_c, shift_c, w_router_padded, n_ctx):
    bsz, t, d = x.shape
    tm = _row_tile(t, 544)
    tps = t // tm
    m = bsz * t
    row = lambda v: v.reshape(1, d)
    const = lambda shape: pl.BlockSpec(shape, lambda i: (0,) * len(shape))
    per_sample = pl.BlockSpec((1, 1, d), lambda i: (i // tps, 0, 0))
    return pl.pallas_call(
        functools.partial(_router_kernel, tm=tm, tiles_per_sample=tps, n_ctx=n_ctx),
        grid=(m // tm,),
        in_specs=[pl.BlockSpec((tm, d), lambda i: (i, 0)), const((1, d)), per_sample, per_sample,
                  const((1, d)), const((1, d)), const(w_router_padded.shape)],
        out_specs=[pl.BlockSpec((tm, d), lambda i: (i, 0)), pl.BlockSpec((tm, LANES), lambda i: (i, 0))],
        out_shape=[jax.ShapeDtypeStruct((m, d), BF16), jax.ShapeDtypeStruct((m, LANES), F32)],
        compiler_params=pltpu.CompilerParams(dimension_semantics=("parallel",), vmem_limit_bytes=VMEM_LIMIT),
        name="router",
    )(x.reshape(m, d), row(gain), scale_l.reshape(bsz, 1, d), shift_l.reshape(bsz, 1, d),
      row(scale_c), row(shift_c), w_router_padded)


def _experts_kernel(be_ref, nb_ref, x_ref, w1_ref, w3_ref, w2_ref, o_ref):
    i = pl.program_id(0)

    @pl.when(i < nb_ref[0])
    def _():
        x = x_ref[...]
        hidden = jax.nn.silu(_dot(x, w1_ref[0])) * _dot(x, w3_ref[0])
        o_ref[...] = _dot(hidden.astype(w2_ref.dtype), w2_ref[0])

    @pl.when(i >= nb_ref[0])
    def _():
        o_ref[...] = jnp.zeros_like(o_ref)


def experts(xb, block_expert, n_used, w1, w3, w2):
    n_rows, d = xb.shape
    rows = MOE_ROWS
    n_blocks = n_rows // rows
    de = w1.shape[-1]
    return pl.pallas_call(
        _experts_kernel,
        grid_spec=pltpu.PrefetchScalarGridSpec(
            num_scalar_prefetch=2,
            grid=(n_blocks,),
            in_specs=[
                pl.BlockSpec((rows, d), lambda i, be, nb: (i, 0)),
                pl.BlockSpec((1, d, de), lambda i, be, nb: (be[i], 0, 0)),
                pl.BlockSpec((1, d, de), lambda i, be, nb: (be[i], 0, 0)),
                pl.BlockSpec((1, de, d), lambda i, be, nb: (be[i], 0, 0)),
            ],
            out_specs=pl.BlockSpec((rows, d), lambda i, be, nb: (i, 0)),
        ),
        out_shape=jax.ShapeDtypeStruct((n_rows, d), F32),
        compiler_params=pltpu.CompilerParams(dimension_semantics=("arbitrary",), vmem_limit_bytes=VMEM_LIMIT),
        name="experts",
    )(block_expert, n_used, xb, w1, w3, w2)


def _top2(vals):
    idx = jnp.arange(vals.shape[-1], dtype=jnp.int32)
    i1 = jnp.argmax(vals, axis=-1).astype(jnp.int32)
    v1 = jnp.max(vals, axis=-1)
    rest = jnp.where(idx == i1[..., None], -jnp.inf, vals)
    i2 = jnp.argmax(rest, axis=-1).astype(jnp.int32)
    v2 = jnp.max(rest, axis=-1)
    return v1, i1, v2, i2


def moe(h, logits, b_router, w1, w3, w2):
    n_tok, d = h.shape
    n_exp = b_router.shape[0]
    epg = n_exp // N_EXPERT_GROUPS
    scores = jax.nn.sigmoid(logits)
    biased = (scores + b_router.astype(F32)).reshape(n_tok, N_EXPERT_GROUPS, epg)
    g1, _, g2, _ = _top2(biased)
    top_group = jnp.argmax(g1 + g2, axis=-1).astype(jnp.int32)
    in_top = jnp.arange(N_EXPERT_GROUPS, dtype=jnp.int32)[None, :, None] == top_group[:, None, None]
    in_group = jnp.sum(jnp.where(in_top, biased, 0.0), axis=1)
    _, l1, _, l2 = _top2(in_group)
    expert = top_group[:, None] * epg + jnp.stack([l1, l2], axis=-1)
    picked = expert[:, :, None] == jnp.arange(n_exp, dtype=jnp.int32)[None, None, :]
    gate = jnp.sum(jnp.where(picked, scores[:, None, :], 0.0), axis=-1)
    gate = gate / jnp.sum(gate, axis=-1, keepdims=True)

    n_assign = n_tok * TOP_K
    flat_e = expert.reshape(-1).astype(jnp.int32)
    onehot = picked.reshape(n_assign, n_exp).astype(jnp.int32)
    rank = jnp.sum((jnp.cumsum(onehot, axis=0) - onehot) * onehot, axis=-1)
    counts = jnp.sum(onehot, axis=0)
    padded = (counts + MOE_ROWS - 1) // MOE_ROWS * MOE_ROWS
    pad_end = jnp.cumsum(padded)
    dest = ((pad_end - padded)[flat_e] + rank).astype(jnp.int32)
    n_blocks = -(-n_assign // MOE_ROWS) + n_exp
    slot_token = jnp.full((n_blocks * MOE_ROWS,), n_tok, jnp.int32).at[dest].set(
        jnp.arange(n_assign, dtype=jnp.int32) // TOP_K)
    first_block = (pad_end - padded) // MOE_ROWS
    block_expert = jnp.clip(
        jnp.sum(jnp.arange(n_blocks, dtype=jnp.int32)[:, None] >= (pad_end // MOE_ROWS)[None, :], axis=-1),
        0, n_exp - 1).astype(jnp.int32)
    del first_block
    n_used = (pad_end[-1:] // MOE_ROWS).astype(jnp.int32)
    h_pad = jnp.concatenate([h, jnp.zeros((1, d), h.dtype)], axis=0)
    xb = h_pad[slot_token]
    yb = experts(xb, block_expert, n_used, w1, w3, w2)
    y = yb[dest].reshape(n_tok, TOP_K, d)
    return jnp.einsum("tkd,tk->td", y, gate)


def _neighbour(s, offset, n_ctx):
    t = s.shape[1]
    idx = jnp.arange(t)
    src = idx + offset
    same = (src >= 0) & (src < t) & ((src >= n_ctx) == (idx >= n_ctx))
    return jnp.where(same[None, :, None], jnp.roll(s, -offset, axis=1), 0.0)


def _rwkv_branch(p, lay, prm, n_ctx):
    bsz, t, _ = p.shape
    width = RW_HEADS * RW_HEAD_DIM

    def shifted(name):
        s = p[..., lay[name][0]:lay[name][0] + lay[name][1]]
        return s + prm["shift_mu"][name] * (0.5 * (_neighbour(s, -1, n_ctx) + _neighbour(s, 1, n_ctx)) - s)

    r, k, v = shifted("r"), shifted("k"), shifted("v")
    lora, g = shifted("lora"), shifted("g")
    heads = lambda a: a.reshape(bsz, t, RW_HEADS, RW_HEAD_DIM)
    kk = heads(k * prm["k_k"])
    kk = (kk / jnp.maximum(jnp.linalg.norm(kk, axis=-1, keepdims=True), 1e-12)).reshape(bsz, t, width)
    dl, al = prm["w2"].shape[1], prm["a2"].shape[1]
    lws, ars = [], []
    for d in range(2):
        w_lora = lora[..., d * dl:(d + 1) * dl]
        a_lora = lora[..., 2 * dl + d * al:2 * dl + (d + 1) * al]
        w_pre = prm["w0"][d] + matmul(w_lora.reshape(bsz * t, dl), prm["w2"][d], act="tanh", precision=HIGHEST,
                                      name="decay_lora").reshape(bsz, t, width)
        lws.append(-jnp.exp(-jax.nn.softplus(-w_pre) - 0.5))
        ars.append(jax.nn.sigmoid(prm["a0"][d] + matmul(a_lora.reshape(bsz * t, al), prm["a2"][d],
                                                        precision=HIGHEST, name="rate_lora").reshape(bsz, t, width)))
    y_f, y_b = wkv7(r, k, v, kk, lws, ars, prm["k_a"], n_ctx)
    yh = heads(y_f + y_b)
    mean = jnp.mean(yh, axis=-1, keepdims=True)
    var = jnp.mean(jnp.square(yh - mean), axis=-1, keepdims=True)
    y = ((yh - mean) * lax.rsqrt(var + RW_GN_EPS)).reshape(bsz, t, width) * prm["ln_w"] + prm["ln_b"]
    bonus = jnp.sum(heads(r) * heads(k) * prm["r_k"], axis=-1, keepdims=True) * heads(v)
    y = y + bonus.reshape(bsz, t, width)
    out_gate = matmul(g.reshape(bsz * t, -1), prm["g2"], act="sigmoid", precision=HIGHEST,
                      name="gate_lora").reshape(bsz, t, width)
    return y * out_gate


def _ssm_branch(p, prm, n_ctx):
    bsz, t, _ = p.shape
    inner = prm["norm_w"].shape[0]
    gn = SSM_GROUPS * SSM_STATE
    z = p[..., :inner]
    xbc = p[..., inner:2 * inner + 2 * gn]
    half = SSM_CONV // 2
    conv = sum(_neighbour(xbc, o - half, n_ctx) * prm["conv_w"][o] for o in range(SSM_CONV))
    xbc = jax.nn.silu(conv + prm["conv_b"])
    xs = xbc[..., :inner]
    y_f, y_b = ssd(xbc, p, 2 * inner + 2 * gn, prm["dt_bias"], prm["a_log"], n_ctx, inner)
    d_skip = jnp.repeat(prm["d"][0] + prm["d"][1], SSM_HEAD_DIM)
    y = (y_f + y_b + d_skip * xs) * jax.nn.silu(z)
    yg = y.reshape(bsz, t, SSM_GROUPS, inner // SSM_GROUPS)
    yg = yg * lax.rsqrt(jnp.mean(yg * yg, axis=-1, keepdims=True) + SSM_NORM_EPS)
    return yg.reshape(bsz, t, inner) * prm["norm_w"]


def _pad_cols(w, n):
    return jnp.pad(w, ((0, 0), (0, n - w.shape[1])))


def kernel(x, c, ctx, c_ctx, w_mod, b_mod, norm_mix_g, w_in, rw_shift_mu, rw_w0, rw_w2, rw_a0, rw_a2, rw_g2,
           rw_k_k, rw_k_a, rw_r_k, rw_ln_w, rw_ln_b, ssm_conv_w, ssm_conv_b, ssm_dt_bias, ssm_a_log, ssm_d,
           ssm_norm_w, w_branch_rw, w_branch_ssm, w_out, norm_ffn_g, w_router, b_router, exp_w1, exp_w3, exp_w2,
           norm_final_g):
    bsz, n_lat, d = x.shape
    depth = w_in.shape[0]
    n_ctx = ctx.shape[1]
    rows = n_lat // GRID_W
    t = n_ctx + n_lat
    width = RW_HEADS * RW_HEAD_DIM
    dl, al, gl = rw_w2.shape[2], rw_a2.shape[2], rw_g2.shape[1]
    inner = ssm_norm_w.shape[1]
    xbc_w = ssm_conv_w.shape[2]
    n_heads = ssm_dt_bias.shape[2]
    n_exp = w_router.shape[1]

    def to_c(a):
        ch = a.shape[-1]
        lat = a[:, n_ctx:].reshape(bsz, rows, GRID_W, ch).transpose(0, 2, 1, 3).reshape(bsz, n_lat, ch)
        return jnp.concatenate([a[:, :n_ctx], lat], axis=1)

    def to_r(a):
        ch = a.shape[-1]
        lat = a[:, n_ctx:].reshape(bsz, GRID_W, rows, ch).transpose(0, 2, 1, 3).reshape(bsz, n_lat, ch)
        return jnp.concatenate([a[:, :n_ctx], lat], axis=1)

    o = 0
    src = {}
    for name, size in (("r", width), ("w_f", dl), ("w_b", dl), ("k", width), ("v", width), ("a_f", al),
                       ("a_b", al), ("g", gl), ("z", inner), ("xbc", xbc_w), ("dt", 2 * n_heads),
                       ("g_rw", d), ("g_ssm", d)):
        src[name] = (o, size)
        o += size
    take = lambda a, name: a[..., src[name][0]:src[name][0] + src[name][1]]
    lora_w = 2 * dl + 2 * al
    lora_pad = -(-lora_w // LANES) * LANES
    g_pad = -(-gl // LANES) * LANES
    dt_pad = -(-2 * n_heads // LANES) * LANES
    lay = {"r": (0, width), "k": (width, width), "v": (2 * width, width), "g_rw": (3 * width, d),
           "g_ssm": (3 * width + d, d), "lora": (3 * width + 2 * d, lora_w),
           "g": (3 * width + 2 * d + lora_pad, gl)}

    silu_c = jax.nn.silu(c)
    silu_cc = jax.nn.silu(c_ctx)[None, :]
    act = jnp.concatenate([silu_c, silu_cc, jnp.zeros((-(bsz + 1) % 8, d), F32)], axis=0)
    w_router_p = _pad_cols(w_router.astype(F32), LANES)

    xr = jnp.concatenate([ctx, x], axis=1)
    for l in range(depth):
        mod = matmul(act, w_mod[l], precision=HIGHEST, name="modulation")[:bsz + 1] + b_mod[l]
        shift_m, scale_m, gate_m, shift_f, scale_f, gate_f = jnp.split(mod[:bsz], 6, axis=-1)
        cshift_m, cscale_m, cgate_m, cshift_f, cscale_f, cgate_f = jnp.split(mod[bsz], 6, axis=-1)

        wl = w_in[l]
        w_r = jnp.concatenate(
            [take(wl, "r"), take(wl, "k"), take(wl, "v"), take(wl, "g_rw"), take(wl, "g_ssm"),
             _pad_cols(jnp.concatenate([take(wl, n) for n in ("w_f", "w_b", "a_f", "a_b")], axis=1), lora_pad),
             _pad_cols(take(wl, "g"), g_pad)], axis=1).astype(BF16)
        w_c = jnp.concatenate([take(wl, "z"), take(wl, "xbc"), _pad_cols(take(wl, "dt"), dt_pad)],
                              axis=1).astype(BF16)
        mods = (norm_mix_g[l], scale_m, shift_m, cscale_m, cshift_m)
        p_r = norm_proj(xr, *mods, w_r, n_ctx)
        p_c = norm_proj(to_c(xr), *mods, w_c, n_ctx)

        mu = rw_shift_mu[l]
        rw_prm = dict(
            shift_mu={"r": take(mu, "r"), "k": take(mu, "k"), "v": take(mu, "v"), "g": take(mu, "g"),
                      "lora": jnp.concatenate([take(mu, n) for n in ("w_f", "w_b", "a_f", "a_b")])},
            w0=rw_w0[l], w2=rw_w2[l], a0=rw_a0[l], a2=rw_a2[l], g2=rw_g2[l],
            k_k=rw_k_k[l], k_a=rw_k_a[l], r_k=rw_r_k[l], ln_w=rw_ln_w[l], ln_b=rw_ln_b[l])
        y_rw = _rwkv_branch(p_r, lay, rw_prm, n_ctx)
        ssm_prm = dict(conv_w=ssm_conv_w[l], conv_b=ssm_conv_b[l], dt_bias=ssm_dt_bias[l], a_log=ssm_a_log[l],
                       d=ssm_d[l], norm_w=ssm_norm_w[l])
        y_ssm = _ssm_branch(p_c, ssm_prm, n_ctx)
        z_ssm = to_r(matmul(y_ssm.reshape(bsz * t, inner), w_branch_ssm[l].astype(BF16),
                            name="ssm_out").reshape(bsz, t, d))
        xr = merge(xr, y_rw, z_ssm, p_r, lay["g_rw"][0], lay["g_ssm"][0], gate_m, cgate_m,
                   w_branch_rw[l].astype(BF16), w_out[l].astype(BF16), n_ctx)

        h, logits = router(xr, norm_ffn_g[l], scale_f, shift_f, cscale_f, cshift_f, w_router_p, n_ctx)
        f = moe(h, logits[:, :n_exp], b_router, exp_w1[l].astype(BF16), exp_w3[l].astype(BF16),
                exp_w2[l].astype(BF16)).reshape(bsz, t, d)
        is_ctx = (jnp.arange(t) < n_ctx)[None, :, None]
        xr = xr + jnp.where(is_ctx, cgate_f[None, None, :], gate_f[:, None, :]) * f
    xl = xr[:, n_ctx:]
    return xl * lax.rsqrt(jnp.mean(xl * xl, axis=-1, keepdims=True) + NORM_EPS) * norm_final_g
```

```python
import functools
import math

import jax
import jax.numpy as jnp
from jax import lax
from jax.experimental import pallas as pl
from jax.experimental.pallas import tpu as pltpu

F32 = jnp.float32
BF16 = jnp.bfloat16
HIGHEST = lax.Precision.HIGHEST

GRID_W = 64
RW_HEADS = 16
RW_HEAD_DIM = 64
RW_GN_EPS = 64e-5
SSM_HEAD_DIM = 64
SSM_GROUPS = 8
SSM_STATE = 128
SSM_CONV = 5
SSM_NORM_EPS = 1e-5
N_EXPERT_GROUPS = 4
TOP_K = 2
NORM_EPS = 1e-6

LANES = 128
WKV_CHUNK = 64
WKV_PAIRS = 2
SSD_CHUNK = 128
MOE_ROWS = 256
VMEM_LIMIT = 56 * 1024 * 1024


def _dot(a, b, precision=None):
    return jnp.dot(a, b, preferred_element_type=F32, precision=precision)


def _dot_nt(a, b, precision=None):
    return lax.dot_general(a, b, (((1,), (1,)), ((), ())), preferred_element_type=F32, precision=precision)


def _dot_tn(a, b, precision=None):
    return lax.dot_general(a, b, (((0,), (0,)), ((), ())), preferred_element_type=F32, precision=precision)


def _row_tile(n_rows_per_sample, limit=1088):
    for tm in (1088, 544, 512, 272, 256, 128, 64, 32, 16):
        if tm <= limit and n_rows_per_sample % tm == 0:
            return tm
    raise ValueError(n_rows_per_sample)


def _col_tile(n_cols, limit=1536):
    best = LANES
    for k in range(1, n_cols // LANES + 1):
        tn = k * LANES
        if n_cols % tn == 0 and tn <= limit:
            best = tn
    return best


def _bf16_pieces(x, n):
    pieces = []
    for _ in range(n):
        p = x.astype(BF16)
        pieces.append(p)
        x = x - p.astype(F32)
    return pieces


def _group_sums(x, group):
    span = max(group, LANES)
    rr = lax.broadcasted_iota(jnp.int32, (span, span), 0) // group
    cc = lax.broadcasted_iota(jnp.int32, (span, span), 1) // group
    ones = (rr == cc).astype(BF16)
    pieces = _bf16_pieces(x, 3)
    cols = []
    for j in range(x.shape[1] // span):
        sl = slice(j * span, (j + 1) * span)
        cols.append(_dot(pieces[0][:, sl], ones) + _dot(pieces[1][:, sl], ones) + _dot(pieces[2][:, sl], ones))
    return cols[0] if len(cols) == 1 else jnp.concatenate(cols, axis=1)


HALO = 8


def _row_from(x, before, after, offset):
    tm = x.shape[0]
    rows = lax.broadcasted_iota(jnp.int32, (tm, 1), 0)
    k = abs(offset)
    if offset < 0:
        out = pltpu.roll(x, k, 0)
        for j in range(k):
            out = jnp.where(rows == j, before[HALO - k + j:HALO - k + j + 1], out)
    else:
        out = pltpu.roll(x, tm - k, 0)
        for j in range(k):
            out = jnp.where(rows == tm - k + j, after[j:j + 1], out)
    return out


def _same_segment(tm, tiles_per_sample, n_ctx, t_total, offset):
    t = (pl.program_id(0) % tiles_per_sample) * tm + lax.broadcasted_iota(jnp.int32, (tm, 1), 0)
    src = t + offset
    return (src >= 0) & (src < t_total) & ((src >= n_ctx) == (t >= n_ctx))


def _halo_specs(tm, n_rows, width, col_block):
    last = n_rows // HALO - 1
    return (pl.BlockSpec((HALO, width), lambda i: (jnp.maximum(i * (tm // HALO) - 1, 0), col_block)),
            pl.BlockSpec((HALO, width), lambda i: (jnp.minimum((i + 1) * (tm // HALO), last), col_block)))


def _mirrored_chunk(s, n_ctx_chunks, n_chunks):
    return jnp.where(s < n_ctx_chunks, n_ctx_chunks - 1 - s, n_chunks - 1 + n_ctx_chunks - s)


def _modulated(x, g, sc_l, sh_l, sc_c, sh_c, is_ctx):
    y = x * lax.rsqrt(jnp.mean(x * x, axis=-1, keepdims=True) + NORM_EPS) * g
    return y * (1.0 + jnp.where(is_ctx, sc_c, sc_l)) + jnp.where(is_ctx, sh_c, sh_l)


def _ctx_rows(tm, tiles_per_sample, n_ctx):
    row0 = (pl.program_id(0) % tiles_per_sample) * tm
    return row0 + lax.broadcasted_iota(jnp.int32, (tm, 1), 0) < n_ctx


def _norm_proj_kernel(x_ref, g_ref, scl_ref, shl_ref, scc_ref, shc_ref, w_ref, o_ref, h_ref, *,
                      tm, tiles_per_sample, n_ctx):
    @pl.when(pl.program_id(1) == 0)
    def _():
        is_ctx = _ctx_rows(tm, tiles_per_sample, n_ctx)
        h = _modulated(x_ref[...], g_ref[...], scl_ref[0], shl_ref[0], scc_ref[...], shc_ref[...], is_ctx)
        h_ref[...] = h.astype(h_ref.dtype)

    o_ref[...] = _dot(h_ref[...], w_ref[...]).astype(o_ref.dtype)


def norm_proj(x, gain, scale_l, shift_l, scale_c, shift_c, w, n_ctx, out_dtype=F32):
    bsz, t, d = x.shape
    n = w.shape[1]
    tm = _row_tile(t)
    tps = t // tm
    tn = _col_tile(n)
    row = lambda v: v.reshape(1, d)
    out = pl.pallas_call(
        functools.partial(_norm_proj_kernel, tm=tm, tiles_per_sample=tps, n_ctx=n_ctx),
        grid=(bsz * tps, n // tn),
        in_specs=[
            pl.BlockSpec((tm, d), lambda i, j: (i, 0)),
            pl.BlockSpec((1, d), lambda i, j: (0, 0)),
            pl.BlockSpec((1, 1, d), lambda i, j: (i // tps, 0, 0)),
            pl.BlockSpec((1, 1, d), lambda i, j: (i // tps, 0, 0)),
            pl.BlockSpec((1, d), lambda i, j: (0, 0)),
            pl.BlockSpec((1, d), lambda i, j: (0, 0)),
            pl.BlockSpec((d, tn), lambda i, j: (0, j)),
        ],
        out_specs=pl.BlockSpec((tm, tn), lambda i, j: (i, j)),
        out_shape=jax.ShapeDtypeStruct((bsz * t, n), out_dtype),
        scratch_shapes=[pltpu.VMEM((tm, d), w.dtype)],
        compiler_params=pltpu.CompilerParams(dimension_semantics=("parallel", "arbitrary"),
                                             vmem_limit_bytes=VMEM_LIMIT),
        name="norm_proj",
    )(x.reshape(bsz * t, d), row(gain), scale_l.reshape(bsz, 1, d), shift_l.reshape(bsz, 1, d),
      row(scale_c), row(shift_c), w)
    return out.reshape(bsz, t, n)


def _matmul_kernel(x_ref, w_ref, o_ref, *, act, precision):
    x = x_ref[...]
    if act == "tanh":
        x = jnp.tanh(x)
    elif act == "sigmoid":
        x = jax.nn.sigmoid(x)
    o_ref[...] = _dot(x.astype(w_ref.dtype), w_ref[...], precision).astype(o_ref.dtype)


def matmul(x, w, act=None, precision=None, out_dtype=F32, name="matmul"):
    m, k = x.shape
    n = w.shape[1]
    tm = _row_tile(m) if m >= 16 else m
    tn = _col_tile(n)
    return pl.pallas_call(
        functools.partial(_matmul_kernel, act=act, precision=precision),
        grid=(m // tm, n // tn),
        in_specs=[pl.BlockSpec((tm, k), lambda i, j: (i, 0)), pl.BlockSpec((k, tn), lambda i, j: (0, j))],
        out_specs=pl.BlockSpec((tm, tn), lambda i, j: (i, j)),
        out_shape=jax.ShapeDtypeStruct((m, n), out_dtype),
        compiler_params=pltpu.CompilerParams(dimension_semantics=("parallel", "parallel"),
                                             vmem_limit_bytes=VMEM_LIMIT),
        name=name,
    )(x, w)


def _rwkv_prep_kernel(m_ref, mb_ref, ma_ref, s_ref, sb_ref, sa_ref, mu_m_ref, mu_s_ref, kk_ref, bias_ref,
                      wl_ref, g2_ref, r_ref, k_ref, v_ref, kko_ref, lwf_ref, lwb_ref, arf_ref, arb_ref, og_ref, *,
                      tm, tiles_per_sample, n_ctx, t_total, width, n_decay):
    has_prev = _same_segment(tm, tiles_per_sample, n_ctx, t_total, -1)
    has_next = _same_segment(tm, tiles_per_sample, n_ctx, t_total, 1)

    def shifted(x_ref, before_ref, after_ref, mu_ref):
        x, before, after = x_ref[...], before_ref[...], after_ref[...]
        near = (jnp.where(has_prev, _row_from(x, before, after, -1), 0.0)
                + jnp.where(has_next, _row_from(x, before, after, 1), 0.0))
        return x + mu_ref[...] * (0.5 * near - x)

    main = shifted(m_ref, mb_ref, ma_ref, mu_m_ref)
    small = shifted(s_ref, sb_ref, sa_ref, mu_s_ref)
    k = main[:, width:2 * width]
    r_ref[...] = main[:, :width]
    k_ref[...] = k
    v_ref[...] = main[:, 2 * width:]
    kk = k * kk_ref[...]
    kko_ref[...] = kk * lax.rsqrt(jnp.maximum(_group_sums(kk * kk, RW_HEAD_DIM), 1e-24))

    half = small.shape[1] // 2
    lora = small[:, :half]
    lane = lax.broadcasted_iota(jnp.int32, lora.shape, 1)
    heads = _dot(jnp.where(lane < n_decay, jnp.tanh(lora), lora).astype(BF16), wl_ref[...]) + bias_ref[...]
    for d, (lw_ref, ar_ref) in enumerate(((lwf_ref, arf_ref), (lwb_ref, arb_ref))):
        w_pre = heads[:, d * width:(d + 1) * width]
        lw_ref[...] = -jnp.exp(-jax.nn.softplus(-w_pre) - 0.5)
        ar_ref[...] = jax.nn.sigmoid(heads[:, (2 + d) * width:(3 + d) * width])
    og_ref[...] = _dot(jax.nn.sigmoid(small[:, half:]).astype(BF16), g2_ref[...])


def rwkv_prep(p, lay, prm, n_ctx):
    bsz, t, _ = p.shape
    width = RW_HEADS * RW_HEAD_DIM
    m = bsz * t
    tm = _row_tile(t, 272)
    tps = t // tm
    p2 = p.reshape(m, -1)
    main_w = 3 * width
    small0 = lay["lora"][0]
    small_w = p2.shape[1] - small0
    half = small_w // 2
    dl, al, gl = prm["w2"].shape[1], prm["a2"].shape[1], prm["g2"].shape[0]
    assert lay["r"][0] == 0 and lay["g"][0] == small0 + half and small0 % small_w == 0
    mu = prm["shift_mu"]
    pad1 = lambda a, n: jnp.pad(a, (0, n - a.shape[0]))
    mu_main = jnp.concatenate([mu["r"], mu["k"], mu["v"]]).reshape(1, main_w)
    mu_small = jnp.concatenate([pad1(mu["lora"], half), pad1(mu["g"], half)]).reshape(1, small_w)
    wl = jnp.zeros((half, 4 * width), F32)
    for j, (blk, rows0, nrows) in enumerate(((prm["w2"][0], 0, dl), (prm["w2"][1], dl, dl),
                                             (prm["a2"][0], 2 * dl, al), (prm["a2"][1], 2 * dl + al, al))):
        wl = wl.at[rows0:rows0 + nrows, j * width:(j + 1) * width].set(blk)
    bias = jnp.concatenate([prm["w0"][0], prm["w0"][1], prm["a0"][0], prm["a0"][1]]).reshape(1, 4 * width)
    g2 = jnp.pad(prm["g2"], ((0, half - gl), (0, 0))).astype(BF16)

    mb, ma = _halo_specs(tm, m, main_w, 0)
    sb, sa = _halo_specs(tm, m, small_w, small0 // small_w)
    const = lambda a: pl.BlockSpec(a.shape, lambda i: (0,) * a.ndim)
    out_spec = pl.BlockSpec((tm, width), lambda i: (i, 0))
    outs = pl.pallas_call(
        functools.partial(_rwkv_prep_kernel, tm=tm, tiles_per_sample=tps, n_ctx=n_ctx, t_total=t, width=width,
                          n_decay=2 * dl),
        grid=(m // tm,),
        in_specs=[pl.BlockSpec((tm, main_w), lambda i: (i, 0)), mb, ma,
                  pl.BlockSpec((tm, small_w), lambda i: (i, small0 // small_w)), sb, sa,
                  const(mu_main), const(mu_small), pl.BlockSpec((1, width), lambda i: (0, 0)), const(bias),
                  pl.BlockSpec(wl.shape, lambda i: (0, 0)), const(g2)],
        out_specs=[out_spec] * 9,
        out_shape=[jax.ShapeDtypeStruct((m, width), F32)] * 9,
        compiler_params=pltpu.CompilerParams(dimension_semantics=("parallel",), vmem_limit_bytes=VMEM_LIMIT),
        name="rwkv_prep",
    )(p2, p2, p2, p2, p2, p2, mu_main, mu_small, prm["k_k"].reshape(1, width), bias, wl.astype(BF16), g2)
    return [o.reshape(bsz, t, width) for o in outs]


def _wkv_chunks(chains):
    c = chains[0][0].shape[0]
    n = 2 * c
    rev = [ch[8] for ch in chains]
    each = lambda f, *cols: [f(*a) for a in zip(*cols)]
    bf = lambda x: x.astype(BF16)
    cat0 = lambda *xs: jnp.concatenate([bf(x) for x in xs], axis=0)
    cat1 = lambda *xs: jnp.concatenate([bf(x) for x in xs], axis=1)

    rc = lax.broadcasted_iota(jnp.int32, (c, c), 0)
    cc = lax.broadcasted_iota(jnp.int32, (c, c), 1)
    seen = {False: (rc >= cc).astype(F32), True: (rc <= cc).astype(F32)}
    r2 = lax.broadcasted_iota(jnp.int32, (n, n), 0)
    c2 = lax.broadcasted_iota(jnp.int32, (n, n), 1)
    before = {False: r2 > c2, True: r2 < c2}
    upto = {False: r2 >= c2, True: r2 <= c2}
    eye = jnp.where(r2 == c2, 1.0, 0.0)
    first = lax.broadcasted_iota(jnp.int32, (c, LANES), 1) < RW_HEAD_DIM
    stack = lambda x: jnp.concatenate([jnp.where(first, x, 0.0), jnp.where(first, 0.0, x)], axis=0)
    ones = jnp.ones((c, LANES), F32)

    cum = [_dot(seen[ch[8]], ch[4], HIGHEST) for ch in chains]
    total = [_dot_tn(ch[4], ones, HIGHEST) for ch in chains]
    e_neg = each(lambda q: jnp.exp(-q), cum)
    e_end = [jnp.exp(q[0:1] if ch[8] else q[c - 1:c]) for q, ch in zip(cum, chains)]
    rs = [stack(ch[0] * jnp.exp(q)) for ch, q in zip(chains, cum)]
    as_ = [stack(-ch[3] * jnp.exp(q - ch[4])) for ch, q in zip(chains, cum)]
    bs = [stack(ch[3] * ch[5] * en) for ch, en in zip(chains, e_neg)]
    ks = [stack(ch[1] * (1.0 + (ch[5] - 1.0) * ch[6]) * en) for ch, en in zip(chains, e_neg)]
    vs = [stack(ch[2]) for ch in chains]
    h0 = [ch[7] for ch in chains]

    pair = each(lambda a, b, c_, d: _dot_nt(cat0(a, b), cat0(c_, d)), rs, as_, bs, ks)
    m_rb = [jnp.where(upto[v_], p[:n, :n], 0.0) for p, v_ in zip(pair, rev)]
    m_rk = [jnp.where(upto[v_], p[:n, n:], 0.0) for p, v_ in zip(pair, rev)]
    l_ab = [jnp.where(before[v_], p[n:, :n], 0.0) for p, v_ in zip(pair, rev)]
    m_ak = [jnp.where(before[v_], p[n:, n:], 0.0) for p, v_ in zip(pair, rev)]

    inv = each(lambda l: eye + l, l_ab)
    power = each(lambda l: _dot(bf(l), bf(l)), l_ab)
    steps = int(math.log2(c)) - 1
    for it in range(steps):
        if it < steps - 1:
            both = each(lambda i, p: _dot(cat0(i, p), bf(p)), inv, power)
            inv = each(lambda i, b: i + b[:n], inv, both)
            power = each(lambda b: b[n:], both)
        else:
            inv = each(lambda i, p: i + _dot(bf(i), bf(p)), inv, power)

    state_and_v = each(cat0, h0, vs)
    w = each(lambda a, m, sv: _dot(cat1(a, m), sv), as_, m_ak, state_and_v)
    y0 = each(lambda r_, m, sv: _dot(cat1(r_, m), sv), rs, m_rk, state_and_v)
    u = each(lambda i, w_: _dot(bf(i), bf(w_)), inv, w)
    y = each(lambda y_, m, u_: y_ + _dot(bf(m), bf(u_)), y0, m_rb, u)
    decayed = each(lambda b, k_, e: jnp.concatenate([b * e, k_ * e], axis=0).T, bs, ks, e_end)
    h_new = each(lambda h, t_, dc, u_, v_: h * jnp.exp(t_) + _dot(bf(dc), cat0(u_, v_)), h0, total, decayed, u, vs)
    return [q[:c] + q[c:] for q in y], h_new


def _wkv_kernel(rf_ref, kf_ref, vf_ref, kkf_ref, lwf_ref, arf_ref, rb_ref, kb_ref, vb_ref, kkb_ref, lwb_ref,
                arb_ref, ka_ref, yf_ref, yb_ref, h_ref, *, pairs):
    @pl.when(pl.program_id(2) == 0)
    def _():
        h_ref[...] = jnp.zeros_like(h_ref)

    dirs = ((rf_ref, kf_ref, vf_ref, kkf_ref, lwf_ref, arf_ref, yf_ref),
            (rb_ref, kb_ref, vb_ref, kkb_ref, lwb_ref, arb_ref, yb_ref))
    chains, outs = [], []
    for d, (r_ref, k_ref, v_ref, kk_ref, lw_ref, ar_ref, y_ref) in enumerate(dirs):
        for p in range(pairs):
            lanes = slice(p * LANES, (p + 1) * LANES)
            chains.append((r_ref[0, :, lanes], k_ref[0, :, lanes], v_ref[0, :, lanes], kk_ref[0, :, lanes],
                           lw_ref[0, :, lanes], ar_ref[0, :, lanes], ka_ref[:, lanes], h_ref[d, p], d == 1))
            outs.append((y_ref, lanes, d, p))
    ys, hs = _wkv_chunks(chains)
    for (y_ref, lanes, d, p), y, h_new in zip(outs, ys, hs):
        y_ref[0, :, lanes] = y
        h_ref[d, p] = h_new


def wkv7(r, k, v, kk, lw, ar, k_a, n_ctx):
    bsz, t, width = r.shape
    c = WKV_CHUNK
    n_cc, n_chunks = n_ctx // c, t // c
    pairs = WKV_PAIRS
    wb = pairs * LANES
    fwd = pl.BlockSpec((1, c, wb), lambda i, p, s: (i, s, p))
    bwd = pl.BlockSpec((1, c, wb), lambda i, p, s: (i, _mirrored_chunk(s, n_cc, n_chunks), p))
    return pl.pallas_call(
        functools.partial(_wkv_kernel, pairs=pairs),
        grid=(bsz, width // wb, n_chunks),
        in_specs=[fwd] * 6 + [bwd] * 6 + [pl.BlockSpec((1, wb), lambda i, p, s: (0, p))],
        out_specs=[fwd, bwd],
        out_shape=[jax.ShapeDtypeStruct((bsz, t, width), F32)] * 2,
        scratch_shapes=[pltpu.VMEM((2, pairs, LANES, LANES), F32)],
        compiler_params=pltpu.CompilerParams(dimension_semantics=("parallel", "parallel", "arbitrary"),
                                             vmem_limit_bytes=VMEM_LIMIT),
        name="wkv7",
    )(r, k, v, kk, lw[0], ar[0], r, k, v, kk, lw[1], ar[1], k_a.reshape(1, width))


def _conv_kernel(x_ref, before_ref, after_ref, w_ref, b_ref, o_ref, *, tm, tiles_per_sample, n_ctx, t_total):
    x, before, after = x_ref[...], before_ref[...], after_ref[...]
    half = SSM_CONV // 2
    acc = x * w_ref[half:half + 1] + b_ref[...]
    for o in range(SSM_CONV):
        if o != half:
            ok = _same_segment(tm, tiles_per_sample, n_ctx, t_total, o - half)
            acc = acc + jnp.where(ok, _row_from(x, before, after, o - half), 0.0) * w_ref[o:o + 1]
    o_ref[...] = jax.nn.silu(acc)


def conv_silu(p, col0, conv_w, conv_b, n_ctx):
    bsz, t, _ = p.shape
    taps, ch = conv_w.shape
    tm = _row_tile(t, 272)
    tps = t // tm
    m = bsz * t
    p2 = p.reshape(m, -1)
    before, after = _halo_specs(tm, m, ch, col0 // ch)
    w_pad = jnp.concatenate([conv_w, jnp.zeros((-taps % 8, ch), F32)], axis=0)
    const = lambda a: pl.BlockSpec(a.shape, lambda i: (0,) * a.ndim)
    out = pl.pallas_call(
        functools.partial(_conv_kernel, tm=tm, tiles_per_sample=tps, n_ctx=n_ctx, t_total=t),
        grid=(m // tm,),
        in_specs=[pl.BlockSpec((tm, ch), lambda i: (i, col0 // ch)), before, after, const(w_pad),
                  pl.BlockSpec((1, ch), lambda i: (0, 0))],
        out_specs=pl.BlockSpec((tm, ch), lambda i: (i, 0)),
        out_shape=jax.ShapeDtypeStruct((m, ch), F32),
        compiler_params=pltpu.CompilerParams(dimension_semantics=("parallel",), vmem_limit_bytes=VMEM_LIMIT),
        name="conv_silu",
    )(p2, p2, p2, w_pad, conv_b.reshape(1, ch))
    return out.reshape(bsz, t, ch)


def _ssd_kernel(xf_ref, dtf_ref, xb_ref, dtb_ref, bias_ref, arow_ref, tabw_ref, tabx_ref, yf_ref, yb_ref,
                s_ref, ct_ref, cp_ref, *, inner, heads_per_group):
    L, hp = SSD_CHUNK, SSM_HEAD_DIM
    e_heads = heads_per_group
    width = e_heads * hp
    gn = SSM_GROUPS * SSM_STATE
    n_heads = SSM_GROUPS * e_heads

    @pl.when(pl.program_id(1) == 0)
    def _():
        s_ref[...] = jnp.zeros_like(s_ref)

    rl = lax.broadcasted_iota(jnp.int32, (L, L), 0)
    cl = lax.broadcasted_iota(jnp.int32, (L, L), 1)
    upto = (rl >= cl, rl <= cl)
    dirs = ((xf_ref, dtf_ref, yf_ref), (xb_ref, dtb_ref, yb_ref))
    for d, (_, dt_ref, _) in enumerate(dirs):
        dt_all = jax.nn.softplus(dt_ref[0] + bias_ref[...])
        cum = _dot(upto[d].astype(F32), dt_all * arow_ref[...], HIGHEST)
        ct_ref[d] = cum.T
        for i, piece in enumerate(_bf16_pieces(cum, 3) + _bf16_pieces(dt_all, 2)):
            cp_ref[d, i] = piece

    lane = lax.broadcasted_iota(jnp.int32, (L, LANES), 1)
    lane_head = lax.broadcasted_iota(jnp.int32, (L, width), 1) // hp

    def group(g, carry):
        two = range(2)
        xcol = pl.ds(pl.multiple_of(g * width, width), width)
        bcol = pl.ds(pl.multiple_of(inner + g * SSM_STATE, SSM_STATE), SSM_STATE)
        ccol = pl.ds(pl.multiple_of(inner + gn + g * SSM_STATE, SSM_STATE), SSM_STATE)
        x = [dirs[d][0][0, :, xcol] for d in two]
        bm = [dirs[d][0][0, :, bcol] for d in two]
        cmb = [dirs[d][0][0, :, ccol].astype(BF16) for d in two]
        tabw = [tabw_ref[d, g] for d in two]
        tabx = [tabx_ref[d, g] for d in two]
        cum_w = [_dot(cp_ref[d, 0], tabw[d]) + _dot(cp_ref[d, 1], tabw[d]) + _dot(cp_ref[d, 2], tabw[d])
                 for d in two]
        dt_x = [_dot(cp_ref[d, 3], tabx[d]) + _dot(cp_ref[d, 4], tabx[d]) for d in two]
        s0 = [s_ref[d, g] for d in two]
        cb = [_dot_nt(cmb[d], bm[d].astype(BF16)) for d in two]
        y_off = [_dot(cmb[d], s0[d].astype(BF16)) for d in two]
        bt = [bm[d].T.astype(BF16) for d in two]
        cum_x = [jnp.concatenate(
            [jnp.where(lane < hp, cum_w[d][:, (2 * q) * LANES:(2 * q + 1) * LANES],
                       cum_w[d][:, (2 * q + 1) * LANES:(2 * q + 2) * LANES]) for q in range(e_heads // 2)],
            axis=1) for d in two]
        tot_x = [cum_x[d][0:1] if d == 1 else cum_x[d][L - 1:L] for d in two]
        xdt = [x[d] * dt_x[d] for d in two]
        lmats, xmasked = [], []
        for d in two:
            for e in range(e_heads):
                row = d * n_heads + g * e_heads + e
                seg = cum_w[d][:, e * LANES:(e + 1) * LANES] - ct_ref[d, pl.ds(row, 1), :]
                lmats.append((cb[d] * jnp.where(upto[d], jnp.exp(jnp.minimum(seg, 0.0)), 0.0)).astype(BF16))
                xmasked.append(jnp.where(lane_head == e, xdt[d], 0.0).astype(BF16))
        y_diag = [_dot(m, xm) for m, xm in zip(lmats, xmasked)]
        s_add = [_dot(bt[d], (xdt[d] * jnp.exp(tot_x[d] - cum_x[d])).astype(BF16)) for d in two]
        for d in two:
            y = y_off[d] * jnp.exp(cum_x[d])
            for e in range(e_heads):
                y = y + y_diag[d * e_heads + e]
            dirs[d][2][0, :, xcol] = y
            s_ref[d, g] = s0[d] * jnp.exp(tot_x[d]) + s_add[d]
        return carry

    lax.fori_loop(0, SSM_GROUPS, group, 0)


def ssd(xbc, p_c, dt_col, dt_bias, a_log, n_ctx, inner):
    bsz, t, xw = xbc.shape
    L = SSD_CHUNK
    n_heads = dt_bias.shape[1]
    e_heads = n_heads // SSM_GROUPS
    width = e_heads * SSM_HEAD_DIM
    n_cc, n_chunks = n_ctx // L, t // L
    pad = LANES - 2 * n_heads
    bias = jnp.concatenate([dt_bias[0], dt_bias[1], jnp.zeros((pad,), F32)]).reshape(1, LANES)
    arow = jnp.concatenate([-jnp.exp(a_log[0]), -jnp.exp(a_log[1]), jnp.zeros((pad,), F32)]).reshape(1, LANES)
    head_row = (jnp.arange(2)[:, None, None, None] * n_heads + jnp.arange(SSM_GROUPS)[None, :, None, None] * e_heads)
    src = jnp.arange(LANES)[None, None, :, None]
    tabw = (src == head_row + jnp.arange(e_heads * LANES)[None, None, None, :] // LANES).astype(BF16)
    tabx = (src == head_row + jnp.arange(width)[None, None, None, :] // SSM_HEAD_DIM).astype(BF16)

    mirrored = lambda s: _mirrored_chunk(s, n_cc, n_chunks)
    const = lambda a: pl.BlockSpec(a.shape, lambda i, s: (0,) * a.ndim)
    return pl.pallas_call(
        functools.partial(_ssd_kernel, inner=inner, heads_per_group=e_heads),
        grid=(bsz, n_chunks),
        in_specs=[
            pl.BlockSpec((1, L, xw), lambda i, s: (i, s, 0)),
            pl.BlockSpec((1, L, LANES), lambda i, s: (i, s, dt_col // LANES)),
            pl.BlockSpec((1, L, xw), lambda i, s: (i, mirrored(s), 0)),
            pl.BlockSpec((1, L, LANES), lambda i, s: (i, mirrored(s), dt_col // LANES)),
            const(bias), const(arow), const(tabw), const(tabx),
        ],
        out_specs=[pl.BlockSpec((1, L, inner), lambda i, s: (i, s, 0)),
                   pl.BlockSpec((1, L, inner), lambda i, s: (i, mirrored(s), 0))],
        out_shape=[jax.ShapeDtypeStruct((bsz, t, inner), F32)] * 2,
        scratch_shapes=[pltpu.VMEM((2, SSM_GROUPS, SSM_STATE, width), F32),
                        pltpu.VMEM((2, LANES, L), F32),
                        pltpu.VMEM((2, 5, L, LANES), BF16)],
        compiler_params=pltpu.CompilerParams(dimension_semantics=("parallel", "arbitrary"),
                                             vmem_limit_bytes=VMEM_LIMIT),
        name="ssd",
    )(xbc, p_c, xbc, p_c, bias, arow, tabw, tabx)


def _merge_kernel(x_ref, yf_ref, yb_ref, r_ref, k_ref, v_ref, og_ref, z_ref, grw_ref, gssm_ref, gl_ref, gc_ref,
                  lnw_ref, lnb_ref, rk_ref, wrw_ref, wout_ref, o_ref, *, tm, tiles_per_sample, n_ctx):
    y = yf_ref[...] + yb_ref[...]
    inv_n = 1.0 / RW_HEAD_DIM
    centred = y - _group_sums(y, RW_HEAD_DIM) * inv_n
    var = _group_sums(centred * centred, RW_HEAD_DIM) * inv_n
    y = centred * lax.rsqrt(var + RW_GN_EPS) * lnw_ref[...] + lnb_ref[...]
    y = y + _group_sums(r_ref[...] * k_ref[...] * rk_ref[...], RW_HEAD_DIM) * v_ref[...]
    y_rw = y * og_ref[...]
    t1 = _dot(y_rw.astype(wrw_ref.dtype), wrw_ref[...])
    merged = jax.nn.sigmoid(grw_ref[...]) * t1 + jax.nn.sigmoid(gssm_ref[...]) * z_ref[...]
    mix = _dot(merged.astype(wout_ref.dtype), wout_ref[...])
    gate = jnp.where(_ctx_rows(tm, tiles_per_sample, n_ctx), gc_ref[...], gl_ref[0])
    o_ref[...] = x_ref[...] + gate * mix


def merge(x, y_f, y_b, r, k, v, out_gate, z_ssm, proj, col_grw, col_gssm, gate_l, gate_c, ln_w, ln_b, r_k,
          w_rw, w_out, n_ctx):
    bsz, t, d = x.shape
    tm = _row_tile(t, 272)
    tps = t // tm
    m = bsz * t
    rows = lambda blk: pl.BlockSpec((tm, d), lambda i: (i, blk))
    const = lambda shape: pl.BlockSpec(shape, lambda i: (0,) * len(shape))
    flat = lambda a: a.reshape(m, -1)
    out = pl.pallas_call(
        functools.partial(_merge_kernel, tm=tm, tiles_per_sample=tps, n_ctx=n_ctx),
        grid=(m // tm,),
        in_specs=[rows(0)] * 8 + [rows(col_grw // d), rows(col_gssm // d),
                                  pl.BlockSpec((1, 1, d), lambda i: (i // tps, 0, 0)), const((1, d)),
                                  const((1, d)), const((1, d)), const((1, d)),
                                  const(w_rw.shape), const(w_out.shape)],
        out_specs=rows(0),
        out_shape=jax.ShapeDtypeStruct((m, d), F32),
        compiler_params=pltpu.CompilerParams(dimension_semantics=("parallel",), vmem_limit_bytes=VMEM_LIMIT),
        name="merge",
    )(flat(x), flat(y_f), flat(y_b), flat(r), flat(k), flat(v), flat(out_gate), flat(z_ssm), flat(proj), flat(proj),
      gate_l.reshape(bsz, 1, d), gate_c.reshape(1, d), ln_w.reshape(1, d), ln_b.reshape(1, d), r_k.reshape(1, d),
      w_rw, w_out)
    return out.reshape(bsz, t, d)


def _ssm_out_kernel(yf_ref, yb_ref, xs_ref, z_ref, dskip_ref, nw_ref, w_ref, o_ref, *, group):
    y = (yf_ref[...] + yb_ref[...] + dskip_ref[...] * xs_ref[...]) * jax.nn.silu(z_ref[...])
    ms = _group_sums(y * y, group) * (1.0 / group)
    y = y * lax.rsqrt(ms + SSM_NORM_EPS) * nw_ref[...]
    o_ref[...] = _dot(y.astype(w_ref.dtype), w_ref[...])


def ssm_out(y_f, y_b, xbc, p_c, z_col, d_skip, norm_w, w):
    bsz, t, inner = y_f.shape
    m = bsz * t
    d = w.shape[1]
    tm = _row_tile(t, 272)
    rows = lambda blk: pl.BlockSpec((tm, inner), lambda i: (i, blk))
    const = lambda shape: pl.BlockSpec(shape, lambda i: (0,) * len(shape))
    flat = lambda a: a.reshape(m, -1)
    out = pl.pallas_call(
        functools.partial(_ssm_out_kernel, group=inner // SSM_GROUPS),
        grid=(m // tm,),
        in_specs=[rows(0), rows(0), rows(0), rows(z_col // inner), const((1, inner)), const((1, inner)),
                  const(w.shape)],
        out_specs=pl.BlockSpec((tm, d), lambda i: (i, 0)),
        out_shape=jax.ShapeDtypeStruct((m, d), F32),
        compiler_params=pltpu.CompilerParams(dimension_semantics=("parallel",), vmem_limit_bytes=VMEM_LIMIT),
        name="ssm_out",
    )(flat(y_f), flat(y_b), flat(xbc), flat(p_c), d_skip.reshape(1, inner), norm_w.reshape(1, inner), w)
    return out.reshape(bsz, t, d)


def _router_kernel(x_ref, g_ref, scl_ref, shl_ref, scc_ref, shc_ref, wr_ref, h_ref, logit_ref, *,
                   tm, tiles_per_sample, n_ctx):
    is_ctx = _ctx_rows(tm, tiles_per_sample, n_ctx)
    h = _modulated(x_ref[...], g_ref[...], scl_ref[0], shl_ref[0], scc_ref[...], shc_ref[...], is_ctx)
    h_ref[...] = h.astype(h_ref.dtype)
    logit_ref[...] = _dot(h, wr_ref[...], HIGHEST)


def router(x, gain, scale_l, shift_l, scale_c, shift_c, w_router_padded, n_ctx):
    bsz, t, d = x.shape
    tm = _row_tile(t, 544)
    tps = t // tm
    m = bsz * t
    row = lambda v: v.reshape(1, d)
    const = lambda shape: pl.BlockSpec(shape, lambda i: (0,) * len(shape))
    per_sample = pl.BlockSpec((1, 1, d), lambda i: (i // tps, 0, 0))
    return pl.pallas_call(
        functools.partial(_router_kernel, tm=tm, tiles_per_sample=tps, n_ctx=n_ctx),
        grid=(m // tm,),
        in_specs=[pl.BlockSpec((tm, d), lambda i: (i, 0)), const((1, d)), per_sample, per_sample,
                  const((1, d)), const((1, d)), const(w_router_padded.shape)],
        out_specs=[pl.BlockSpec((tm, d), lambda i: (i, 0)), pl.BlockSpec((tm, LANES), lambda i: (i, 0))],
        out_shape=[jax.ShapeDtypeStruct((m, d), BF16), jax.ShapeDtypeStruct((m, LANES), F32)],
        compiler_params=pltpu.CompilerParams(dimension_semantics=("parallel",), vmem_limit_bytes=VMEM_LIMIT),
        name="router",
    )(x.reshape(m, d), row(gain), scale_l.reshape(bsz, 1, d), shift_l.reshape(bsz, 1, d),
      row(scale_c), row(shift_c), w_router_padded)


def _experts_kernel(be_ref, nb_ref, x_ref, w1_ref, w3_ref, w2_ref, o_ref):
    i = pl.program_id(0)

    @pl.when(i < nb_ref[0])
    def _():
        x = x_ref[...]
        hidden = jax.nn.silu(_dot(x, w1_ref[0])) * _dot(x, w3_ref[0])
        o_ref[...] = _dot(hidden.astype(w2_ref.dtype), w2_ref[0])

    @pl.when(i >= nb_ref[0])
    def _():
        o_ref[...] = jnp.zeros_like(o_ref)


def experts(xb, block_expert, n_used, w1, w3, w2):
    n_rows, d = xb.shape
    rows = MOE_ROWS
    n_blocks = n_rows // rows
    de = w1.shape[-1]
    return pl.pallas_call(
        _experts_kernel,
        grid_spec=pltpu.PrefetchScalarGridSpec(
            num_scalar_prefetch=2,
            grid=(n_blocks,),
            in_specs=[
                pl.BlockSpec((rows, d), lambda i, be, nb: (i, 0)),
                pl.BlockSpec((1, d, de), lambda i, be, nb: (be[i], 0, 0)),
                pl.BlockSpec((1, d, de), lambda i, be, nb: (be[i], 0, 0)),
                pl.BlockSpec((1, de, d), lambda i, be, nb: (be[i], 0, 0)),
            ],
            out_specs=pl.BlockSpec((rows, d), lambda i, be, nb: (i, 0)),
        ),
        out_shape=jax.ShapeDtypeStruct((n_rows, d), F32),
        compiler_params=pltpu.CompilerParams(dimension_semantics=("arbitrary",), vmem_limit_bytes=VMEM_LIMIT),
        name="experts",
    )(block_expert, n_used, xb, w1, w3, w2)


def _combine_kernel(x_ref, y_ref, gate_ref, gl_ref, gc_ref, fin_ref, o_ref, *, tm, tiles_per_sample, n_ctx, final):
    d = x_ref.shape[1]
    g = gate_ref[...]
    f = y_ref[:, :d] * g[:, 0:1] + y_ref[:, d:] * g[:, 1:2]
    mod = jnp.where(_ctx_rows(tm, tiles_per_sample, n_ctx), gc_ref[...], gl_ref[0])
    out = x_ref[...] + mod * f
    if final:
        out = out * lax.rsqrt(jnp.mean(out * out, axis=-1, keepdims=True) + NORM_EPS) * fin_ref[...]
    o_ref[...] = out


def combine(x, y_pairs, gates, gate_l, gate_c, final_gain, n_ctx, final):
    bsz, t, d = x.shape
    m = bsz * t
    tm = _row_tile(t, 544)
    tps = t // tm
    const = lambda shape: pl.BlockSpec(shape, lambda i: (0,) * len(shape))
    out = pl.pallas_call(
        functools.partial(_combine_kernel, tm=tm, tiles_per_sample=tps, n_ctx=n_ctx, final=final),
        grid=(m // tm,),
        in_specs=[pl.BlockSpec((tm, d), lambda i: (i, 0)), pl.BlockSpec((tm, TOP_K * d), lambda i: (i, 0)),
                  pl.BlockSpec((tm, LANES), lambda i: (i, 0)), pl.BlockSpec((1, 1, d), lambda i: (i // tps, 0, 0)),
                  const((1, d)), const((1, d))],
        out_specs=pl.BlockSpec((tm, d), lambda i: (i, 0)),
        out_shape=jax.ShapeDtypeStruct((m, d), F32),
        compiler_params=pltpu.CompilerParams(dimension_semantics=("parallel",), vmem_limit_bytes=VMEM_LIMIT),
        name="combine",
    )(x.reshape(m, d), y_pairs.reshape(m, TOP_K * d), gates, gate_l.reshape(bsz, 1, d), gate_c.reshape(1, d),
      final_gain.reshape(1, d))
    return out.reshape(bsz, t, d)


def _top2(vals):
    idx = jnp.arange(vals.shape[-1], dtype=jnp.int32)
    i1 = jnp.argmax(vals, axis=-1).astype(jnp.int32)
    v1 = jnp.max(vals, axis=-1)
    rest = jnp.where(idx == i1[..., None], -jnp.inf, vals)
    i2 = jnp.argmax(rest, axis=-1).astype(jnp.int32)
    v2 = jnp.max(rest, axis=-1)
    return v1, i1, v2, i2


def moe(h, logits, b_router, w1, w3, w2):
    n_tok, d = h.shape
    n_exp = b_router.shape[0]
    epg = n_exp // N_EXPERT_GROUPS
    scores = jax.nn.sigmoid(logits)
    biased = (scores + b_router.astype(F32)).reshape(n_tok, N_EXPERT_GROUPS, epg)
    g1, _, g2, _ = _top2(biased)
    top_group = jnp.argmax(g1 + g2, axis=-1).astype(jnp.int32)
    in_top = jnp.arange(N_EXPERT_GROUPS, dtype=jnp.int32)[None, :, None] == top_group[:, None, None]
    in_group = jnp.sum(jnp.where(in_top, biased, 0.0), axis=1)
    _, l1, _, l2 = _top2(in_group)
    expert = top_group[:, None] * epg + jnp.stack([l1, l2], axis=-1)
    picked = expert[:, :, None] == jnp.arange(n_exp, dtype=jnp.int32)[None, None, :]
    gate = jnp.sum(jnp.where(picked, scores[:, None, :], 0.0), axis=-1)
    gate = gate / jnp.sum(gate, axis=-1, keepdims=True)

    n_assign = n_tok * TOP_K
    flat_e = expert.reshape(-1).astype(jnp.int32)
    onehot = picked.reshape(n_assign, n_exp).astype(jnp.int32)
    rank = jnp.sum((jnp.cumsum(onehot, axis=0) - onehot) * onehot, axis=-1)
    counts = jnp.sum(onehot, axis=0)
    padded = (counts + MOE_ROWS - 1) // MOE_ROWS * MOE_ROWS
    pad_end = jnp.cumsum(padded)
    dest = ((pad_end - padded)[flat_e] + rank).astype(jnp.int32)
    n_blocks = -(-n_assign // MOE_ROWS) + n_exp
    slot_token = jnp.full((n_blocks * MOE_ROWS,), n_tok, jnp.int32).at[dest].set(
        jnp.arange(n_assign, dtype=jnp.int32) // TOP_K)
    block_expert = jnp.clip(
        jnp.sum(jnp.arange(n_blocks, dtype=jnp.int32)[:, None] >= (pad_end // MOE_ROWS)[None, :], axis=-1),
        0, n_exp - 1).astype(jnp.int32)
    n_used = (pad_end[-1:] // MOE_ROWS).astype(jnp.int32)
    h_pad = jnp.concatenate([h, jnp.zeros((1, d), h.dtype)], axis=0)
    xb = h_pad[slot_token]
    yb = experts(xb, block_expert, n_used, w1, w3, w2)
    return yb[dest], jnp.pad(gate, ((0, 0), (0, LANES - TOP_K)))


def _ssm_branch(p, prm, w_branch, n_ctx):
    inner = prm["norm_w"].shape[0]
    xbc_w = prm["conv_w"].shape[1]
    xbc = conv_silu(p, 0, prm["conv_w"], prm["conv_b"], n_ctx)
    y_f, y_b = ssd(xbc, p, xbc_w + inner, prm["dt_bias"], prm["a_log"], n_ctx, inner)
    d_skip = jnp.repeat(prm["d"][0] + prm["d"][1], SSM_HEAD_DIM)
    return ssm_out(y_f, y_b, xbc, p, xbc_w, d_skip, prm["norm_w"], w_branch)


def _pad_cols(w, n):
    return jnp.pad(w, ((0, 0), (0, n - w.shape[1])))


def kernel(x, c, ctx, c_ctx, w_mod, b_mod, norm_mix_g, w_in, rw_shift_mu, rw_w0, rw_w2, rw_a0, rw_a2, rw_g2,
           rw_k_k, rw_k_a, rw_r_k, rw_ln_w, rw_ln_b, ssm_conv_w, ssm_conv_b, ssm_dt_bias, ssm_a_log, ssm_d,
           ssm_norm_w, w_branch_rw, w_branch_ssm, w_out, norm_ffn_g, w_router, b_router, exp_w1, exp_w3, exp_w2,
           norm_final_g):
    bsz, n_lat, d = x.shape
    depth = w_in.shape[0]
    n_ctx = ctx.shape[1]
    rows = n_lat // GRID_W
    t = n_ctx + n_lat
    width = RW_HEADS * RW_HEAD_DIM
    dl, al, gl = rw_w2.shape[2], rw_a2.shape[2], rw_g2.shape[1]
    inner = ssm_norm_w.shape[1]
    xbc_w = ssm_conv_w.shape[2]
    n_heads = ssm_dt_bias.shape[2]
    n_exp = w_router.shape[1]

    def to_c(a):
        ch = a.shape[-1]
        lat = a[:, n_ctx:].reshape(bsz, rows, GRID_W, ch).transpose(0, 2, 1, 3).reshape(bsz, n_lat, ch)
        return jnp.concatenate([a[:, :n_ctx], lat], axis=1)

    def to_r(a):
        ch = a.shape[-1]
        lat = a[:, n_ctx:].reshape(bsz, GRID_W, rows, ch).transpose(0, 2, 1, 3).reshape(bsz, n_lat, ch)
        return jnp.concatenate([a[:, :n_ctx], lat], axis=1)

    o = 0
    src = {}
    for name, size in (("r", width), ("w_f", dl), ("w_b", dl), ("k", width), ("v", width), ("a_f", al),
                       ("a_b", al), ("g", gl), ("z", inner), ("xbc", xbc_w), ("dt", 2 * n_heads),
                       ("g_rw", d), ("g_ssm", d)):
        src[name] = (o, size)
        o += size
    take = lambda a, name: a[..., src[name][0]:src[name][0] + src[name][1]]
    lora_w = 2 * dl + 2 * al
    lora_pad = -(-lora_w // LANES) * LANES
    g_pad = -(-gl // LANES) * LANES
    dt_pad = -(-2 * n_heads // LANES) * LANES
    lay = {"r": (0, width), "k": (width, width), "v": (2 * width, width), "g_rw": (3 * width, d),
           "g_ssm": (3 * width + d, d), "lora": (3 * width + 2 * d, lora_w),
           "g": (3 * width + 2 * d + lora_pad, gl)}

    silu_c = jax.nn.silu(c)
    silu_cc = jax.nn.silu(c_ctx)[None, :]
    act = jnp.concatenate([silu_c, silu_cc, jnp.zeros((-(bsz + 1) % 8, d), F32)], axis=0)
    w_router_p = _pad_cols(w_router.astype(F32), LANES)

    xr = jnp.concatenate([ctx, x], axis=1)
    for l in range(depth):
        mod = matmul(act, w_mod[l], precision=HIGHEST, name="modulation")[:bsz + 1] + b_mod[l]
        shift_m, scale_m, gate_m, shift_f, scale_f, gate_f = jnp.split(mod[:bsz], 6, axis=-1)
        cshift_m, cscale_m, cgate_m, cshift_f, cscale_f, cgate_f = jnp.split(mod[bsz], 6, axis=-1)

        wl = w_in[l]
        w_r = jnp.concatenate(
            [take(wl, "r"), take(wl, "k"), take(wl, "v"), take(wl, "g_rw"), take(wl, "g_ssm"),
             _pad_cols(jnp.concatenate([take(wl, n) for n in ("w_f", "w_b", "a_f", "a_b")], axis=1), lora_pad),
             _pad_cols(take(wl, "g"), g_pad)], axis=1).astype(BF16)
        w_c = jnp.concatenate([take(wl, "xbc"), take(wl, "z"), _pad_cols(take(wl, "dt"), dt_pad)],
                              axis=1).astype(BF16)
        mods = (norm_mix_g[l], scale_m, shift_m, cscale_m, cshift_m)
        p_r = norm_proj(xr, *mods, w_r, n_ctx)
        p_c = norm_proj(to_c(xr), *mods, w_c, n_ctx)

        mu = rw_shift_mu[l]
        rw_prm = dict(
            shift_mu={"r": take(mu, "r"), "k": take(mu, "k"), "v": take(mu, "v"), "g": take(mu, "g"),
                      "lora": jnp.concatenate([take(mu, n) for n in ("w_f", "w_b", "a_f", "a_b")])},
            w0=rw_w0[l], w2=rw_w2[l], a0=rw_a0[l], a2=rw_a2[l], g2=rw_g2[l],
            k_k=rw_k_k[l], k_a=rw_k_a[l], r_k=rw_r_k[l], ln_w=rw_ln_w[l], ln_b=rw_ln_b[l])
        r, k, v, kk, lw_f, lw_b, ar_f, ar_b, out_gate = rwkv_prep(p_r, lay, rw_prm, n_ctx)
        y_f, y_b = wkv7(r, k, v, kk, (lw_f, lw_b), (ar_f, ar_b), rw_k_a[l], n_ctx)
        ssm_prm = dict(conv_w=ssm_conv_w[l], conv_b=ssm_conv_b[l], dt_bias=ssm_dt_bias[l], a_log=ssm_a_log[l],
                       d=ssm_d[l], norm_w=ssm_norm_w[l])
        z_ssm = to_r(_ssm_branch(p_c, ssm_prm, w_branch_ssm[l].astype(BF16), n_ctx))
        xr = merge(xr, y_f, y_b, r, k, v, out_gate, z_ssm, p_r, lay["g_rw"][0], lay["g_ssm"][0], gate_m, cgate_m,
                   rw_ln_w[l], rw_ln_b[l], rw_r_k[l], w_branch_rw[l].astype(BF16), w_out[l].astype(BF16), n_ctx)

        h, logits = router(xr, norm_ffn_g[l], scale_f, shift_f, cscale_f, cshift_f, w_router_p, n_ctx)
        y_pairs, gates = moe(h, logits[:, :n_exp], b_router, exp_w1[l].astype(BF16), exp_w3[l].astype(BF16),
                             exp_w2[l].astype(BF16))
        xr = combine(xr, y_pairs, gates, gate_f, cgate_f, norm_final_g, n_ctx, final=(l == depth - 1))
    return xr[:, n_ctx:]
```

```python
import functools
import math

import jax
import jax.numpy as jnp
from jax import lax
from jax.experimental import pallas as pl
from jax.experimental.pallas import tpu as pltpu

F32 = jnp.float32
BF16 = jnp.bfloat16
HIGHEST = lax.Precision.HIGHEST

GRID_W = 64
RW_HEADS = 16
RW_HEAD_DIM = 64
RW_GN_EPS = 64e-5
SSM_HEAD_DIM = 64
SSM_GROUPS = 8
SSM_STATE = 128
SSM_CONV = 5
SSM_NORM_EPS = 1e-5
N_EXPERT_GROUPS = 4
TOP_K = 2
NORM_EPS = 1e-6

LANES = 128
WKV_CHUNK = 64
WKV_PAIRS = 4
SSD_CHUNK = 128
MOE_ROWS = 256
VMEM_LIMIT = 56 * 1024 * 1024


def _dot(a, b, precision=None):
    return jnp.dot(a, b, preferred_element_type=F32, precision=precision)


def _dot_nt(a, b, precision=None):
    return lax.dot_general(a, b, (((1,), (1,)), ((), ())), preferred_element_type=F32, precision=precision)


def _dot_tn(a, b, precision=None):
    return lax.dot_general(a, b, (((0,), (0,)), ((), ())), preferred_element_type=F32, precision=precision)


def _row_tile(n_rows_per_sample, limit=1088):
    for tm in (1088, 544, 512, 272, 256, 128, 64, 32, 16):
        if tm <= limit and n_rows_per_sample % tm == 0:
            return tm
    raise ValueError(n_rows_per_sample)


def _col_tile(n_cols, limit=1536):
    best = LANES
    for k in range(1, n_cols // LANES + 1):
        tn = k * LANES
        if n_cols % tn == 0 and tn <= limit:
            best = tn
    return best


def _bf16_pieces(x, n):
    pieces = []
    for _ in range(n):
        p = x.astype(BF16)
        pieces.append(p)
        x = x - p.astype(F32)
    return pieces


def _group_sums(x, group):
    span = max(group, LANES)
    rr = lax.broadcasted_iota(jnp.int32, (span, span), 0) // group
    cc = lax.broadcasted_iota(jnp.int32, (span, span), 1) // group
    ones = (rr == cc).astype(BF16)
    pieces = _bf16_pieces(x, 3)
    cols = []
    for j in range(x.shape[1] // span):
        sl = slice(j * span, (j + 1) * span)
        cols.append(_dot(pieces[0][:, sl], ones) + _dot(pieces[1][:, sl], ones) + _dot(pieces[2][:, sl], ones))
    return cols[0] if len(cols) == 1 else jnp.concatenate(cols, axis=1)


HALO = 8


def _row_from(x, before, after, offset):
    tm = x.shape[0]
    rows = lax.broadcasted_iota(jnp.int32, (tm, 1), 0)
    k = abs(offset)
    if offset < 0:
        out = pltpu.roll(x, k, 0)
        for j in range(k):
            out = jnp.where(rows == j, before[HALO - k + j:HALO - k + j + 1], out)
    else:
        out = pltpu.roll(x, tm - k, 0)
        for j in range(k):
            out = jnp.where(rows == tm - k + j, after[j:j + 1], out)
    return out


def _same_segment(tm, tiles_per_sample, n_ctx, t_total, offset):
    t = (pl.program_id(0) % tiles_per_sample) * tm + lax.broadcasted_iota(jnp.int32, (tm, 1), 0)
    src = t + offset
    return (src >= 0) & (src < t_total) & ((src >= n_ctx) == (t >= n_ctx))


def _halo_specs(tm, n_rows, width, col_block):
    last = n_rows // HALO - 1
    return (pl.BlockSpec((HALO, width), lambda i: (jnp.maximum(i * (tm // HALO) - 1, 0), col_block)),
            pl.BlockSpec((HALO, width), lambda i: (jnp.minimum((i + 1) * (tm // HALO), last), col_block)))


def _mirrored_chunk(s, n_ctx_chunks, n_chunks):
    return jnp.where(s < n_ctx_chunks, n_ctx_chunks - 1 - s, n_chunks - 1 + n_ctx_chunks - s)


def _modulated(x, g, sc_l, sh_l, sc_c, sh_c, is_ctx):
    y = x * lax.rsqrt(jnp.mean(x * x, axis=-1, keepdims=True) + NORM_EPS) * g
    return y * (1.0 + jnp.where(is_ctx, sc_c, sc_l)) + jnp.where(is_ctx, sh_c, sh_l)


def _ctx_rows(tm, tiles_per_sample, n_ctx):
    row0 = (pl.program_id(0) % tiles_per_sample) * tm
    return row0 + lax.broadcasted_iota(jnp.int32, (tm, 1), 0) < n_ctx


def _norm_proj_kernel(x_ref, g_ref, scl_ref, shl_ref, scc_ref, shc_ref, w_ref, o_ref, h_ref, *,
                      tm, tiles_per_sample, n_ctx):
    @pl.when(pl.program_id(1) == 0)
    def _():
        is_ctx = _ctx_rows(tm, tiles_per_sample, n_ctx)
        h = _modulated(x_ref[...], g_ref[...], scl_ref[0], shl_ref[0], scc_ref[...], shc_ref[...], is_ctx)
        h_ref[...] = h.astype(h_ref.dtype)

    o_ref[...] = _dot(h_ref[...], w_ref[...]).astype(o_ref.dtype)


def norm_proj(x, gain, scale_l, shift_l, scale_c, shift_c, w, n_ctx, out_dtype=F32):
    bsz, t, d = x.shape
    n = w.shape[1]
    tm = _row_tile(t)
    tps = t // tm
    tn = _col_tile(n)
    row = lambda v: v.reshape(1, d)
    out = pl.pallas_call(
        functools.partial(_norm_proj_kernel, tm=tm, tiles_per_sample=tps, n_ctx=n_ctx),
        grid=(bsz * tps, n // tn),
        in_specs=[
            pl.BlockSpec((tm, d), lambda i, j: (i, 0)),
            pl.BlockSpec((1, d), lambda i, j: (0, 0)),
            pl.BlockSpec((1, 1, d), lambda i, j: (i // tps, 0, 0)),
            pl.BlockSpec((1, 1, d), lambda i, j: (i // tps, 0, 0)),
            pl.BlockSpec((1, d), lambda i, j: (0, 0)),
            pl.BlockSpec((1, d), lambda i, j: (0, 0)),
            pl.BlockSpec((d, tn), lambda i, j: (0, j)),
        ],
        out_specs=pl.BlockSpec((tm, tn), lambda i, j: (i, j)),
        out_shape=jax.ShapeDtypeStruct((bsz * t, n), out_dtype),
        scratch_shapes=[pltpu.VMEM((tm, d), w.dtype)],
        compiler_params=pltpu.CompilerParams(dimension_semantics=("parallel", "arbitrary"),
                                             vmem_limit_bytes=VMEM_LIMIT),
        name="norm_proj",
    )(x.reshape(bsz * t, d), row(gain), scale_l.reshape(bsz, 1, d), shift_l.reshape(bsz, 1, d),
      row(scale_c), row(shift_c), w)
    return out.reshape(bsz, t, n)


def _matmul_kernel(x_ref, w_ref, o_ref, *, act, precision):
    x = x_ref[...]
    if act == "tanh":
        x = jnp.tanh(x)
    elif act == "sigmoid":
        x = jax.nn.sigmoid(x)
    o_ref[...] = _dot(x.astype(w_ref.dtype), w_ref[...], precision).astype(o_ref.dtype)


def matmul(x, w, act=None, precision=None, out_dtype=F32, name="matmul"):
    m, k = x.shape
    n = w.shape[1]
    tm = _row_tile(m) if m >= 16 else m
    tn = _col_tile(n)
    return pl.pallas_call(
        functools.partial(_matmul_kernel, act=act, precision=precision),
        grid=(m // tm, n // tn),
        in_specs=[pl.BlockSpec((tm, k), lambda i, j: (i, 0)), pl.BlockSpec((k, tn), lambda i, j: (0, j))],
        out_specs=pl.BlockSpec((tm, tn), lambda i, j: (i, j)),
        out_shape=jax.ShapeDtypeStruct((m, n), out_dtype),
        compiler_params=pltpu.CompilerParams(dimension_semantics=("parallel", "parallel"),
                                             vmem_limit_bytes=VMEM_LIMIT),
        name=name,
    )(x, w)


def _rwkv_prep_kernel(m_ref, mb_ref, ma_ref, s_ref, sb_ref, sa_ref, mu_m_ref, mu_s_ref, kk_ref, bias_ref,
                      wl_ref, g2_ref, r_ref, k_ref, v_ref, kko_ref, lwf_ref, lwb_ref, arf_ref, arb_ref, og_ref, *,
                      tm, tiles_per_sample, n_ctx, t_total, width, n_decay):
    has_prev = _same_segment(tm, tiles_per_sample, n_ctx, t_total, -1)
    has_next = _same_segment(tm, tiles_per_sample, n_ctx, t_total, 1)

    def shifted(x_ref, before_ref, after_ref, mu_ref):
        x, before, after = x_ref[...], before_ref[...], after_ref[...]
        near = (jnp.where(has_prev, _row_from(x, before, after, -1), 0.0)
                + jnp.where(has_next, _row_from(x, before, after, 1), 0.0))
        return x + mu_ref[...] * (0.5 * near - x)

    main = shifted(m_ref, mb_ref, ma_ref, mu_m_ref)
    small = shifted(s_ref, sb_ref, sa_ref, mu_s_ref)
    k = main[:, width:2 * width]
    r_ref[...] = main[:, :width]
    k_ref[...] = k
    v_ref[...] = main[:, 2 * width:]
    kk = k * kk_ref[...]
    kko_ref[...] = kk * lax.rsqrt(jnp.maximum(_group_sums(kk * kk, RW_HEAD_DIM), 1e-24))

    half = small.shape[1] // 2
    lora = small[:, :half]
    lane = lax.broadcasted_iota(jnp.int32, lora.shape, 1)
    heads = _dot(jnp.where(lane < n_decay, jnp.tanh(lora), lora).astype(BF16), wl_ref[...]) + bias_ref[...]
    for d, (lw_ref, ar_ref) in enumerate(((lwf_ref, arf_ref), (lwb_ref, arb_ref))):
        w_pre = heads[:, d * width:(d + 1) * width]
        lw_ref[...] = -jnp.exp(-jax.nn.softplus(-w_pre) - 0.5)
        ar_ref[...] = jax.nn.sigmoid(heads[:, (2 + d) * width:(3 + d) * width])
    og_ref[...] = _dot(jax.nn.sigmoid(small[:, half:]).astype(BF16), g2_ref[...])


def rwkv_prep(p, lay, prm, n_ctx):
    bsz, t, _ = p.shape
    width = RW_HEADS * RW_HEAD_DIM
    m = bsz * t
    tm = _row_tile(t, 272)
    tps = t // tm
    p2 = p.reshape(m, -1)
    main_w = 3 * width
    small0 = lay["lora"][0]
    small_w = p2.shape[1] - small0
    half = small_w // 2
    dl, al, gl = prm["w2"].shape[1], prm["a2"].shape[1], prm["g2"].shape[0]
    assert lay["r"][0] == 0 and lay["g"][0] == small0 + half and small0 % small_w == 0
    mu = prm["shift_mu"]
    pad1 = lambda a, n: jnp.pad(a, (0, n - a.shape[0]))
    mu_main = jnp.concatenate([mu["r"], mu["k"], mu["v"]]).reshape(1, main_w)
    mu_small = jnp.concatenate([pad1(mu["lora"], half), pad1(mu["g"], half)]).reshape(1, small_w)
    wl = jnp.zeros((half, 4 * width), F32)
    for j, (blk, rows0, nrows) in enumerate(((prm["w2"][0], 0, dl), (prm["w2"][1], dl, dl),
                                             (prm["a2"][0], 2 * dl, al), (prm["a2"][1], 2 * dl + al, al))):
        wl = wl.at[rows0:rows0 + nrows, j * width:(j + 1) * width].set(blk)
    bias = jnp.concatenate([prm["w0"][0], prm["w0"][1], prm["a0"][0], prm["a0"][1]]).reshape(1, 4 * width)
    g2 = jnp.pad(prm["g2"], ((0, half - gl), (0, 0))).astype(BF16)

    mb, ma = _halo_specs(tm, m, main_w, 0)
    sb, sa = _halo_specs(tm, m, small_w, small0 // small_w)
    const = lambda a: pl.BlockSpec(a.shape, lambda i: (0,) * a.ndim)
    out_spec = pl.BlockSpec((tm, width), lambda i: (i, 0))
    outs = pl.pallas_call(
        functools.partial(_rwkv_prep_kernel, tm=tm, tiles_per_sample=tps, n_ctx=n_ctx, t_total=t, width=width,
                          n_decay=2 * dl),
        grid=(m // tm,),
        in_specs=[pl.BlockSpec((tm, main_w), lambda i: (i, 0)), mb, ma,
                  pl.BlockSpec((tm, small_w), lambda i: (i, small0 // small_w)), sb, sa,
                  const(mu_main), const(mu_small), pl.BlockSpec((1, width), lambda i: (0, 0)), const(bias),
                  pl.BlockSpec(wl.shape, lambda i: (0, 0)), const(g2)],
        out_specs=[out_spec] * 9,
        out_shape=[jax.ShapeDtypeStruct((m, width), F32)] * 9,
        compiler_params=pltpu.CompilerParams(dimension_semantics=("parallel",), vmem_limit_bytes=VMEM_LIMIT),
        name="rwkv_prep",
    )(p2, p2, p2, p2, p2, p2, mu_main, mu_small, prm["k_k"].reshape(1, width), bias, wl.astype(BF16), g2)
    return [o.reshape(bsz, t, width) for o in outs]


def _wkv_chunks(chains):
    c = chains[0][0].shape[0]
    n = 2 * c
    rev = [ch[8] for ch in chains]
    each = lambda f, *cols: [f(*a) for a in zip(*cols)]
    bf = lambda x: x.astype(BF16)
    cat0 = lambda *xs: jnp.concatenate([bf(x) for x in xs], axis=0)
    cat1 = lambda *xs: jnp.concatenate([bf(x) for x in xs], axis=1)

    rc = lax.broadcasted_iota(jnp.int32, (c, c), 0)
    cc = lax.broadcasted_iota(jnp.int32, (c, c), 1)
    seen = {False: (rc >= cc).astype(F32), True: (rc <= cc).astype(F32)}
    r2 = lax.broadcasted_iota(jnp.int32, (n, n), 0)
    c2 = lax.broadcasted_iota(jnp.int32, (n, n), 1)
    before = {False: r2 > c2, True: r2 < c2}
    upto = {False: r2 >= c2, True: r2 <= c2}
    eye = jnp.where(r2 == c2, 1.0, 0.0)
    first = lax.broadcasted_iota(jnp.int32, (c, LANES), 1) < RW_HEAD_DIM
    stack = lambda x: jnp.concatenate([jnp.where(first, x, 0.0), jnp.where(first, 0.0, x)], axis=0)
    ones = jnp.ones((c, LANES), F32)

    cum = [_dot(seen[ch[8]], ch[4], HIGHEST) for ch in chains]
    total = [_dot_tn(ch[4], ones, HIGHEST) for ch in chains]
    e_neg = each(lambda q: jnp.exp(-q), cum)
    e_end = [jnp.exp(q[0:1] if ch[8] else q[c - 1:c]) for q, ch in zip(cum, chains)]
    rs = [stack(ch[0] * jnp.exp(q)) for ch, q in zip(chains, cum)]
    as_ = [stack(-ch[3] * jnp.exp(q - ch[4])) for ch, q in zip(chains, cum)]
    bs = [stack(ch[3] * ch[5] * en) for ch, en in zip(chains, e_neg)]
    ks = [stack(ch[1] * (1.0 + (ch[5] - 1.0) * ch[6]) * en) for ch, en in zip(chains, e_neg)]
    vs = [stack(ch[2]) for ch in chains]
    h0 = [ch[7] for ch in chains]

    pair = each(lambda a, b, c_, d: _dot_nt(cat0(a, b), cat0(c_, d)), rs, as_, bs, ks)
    m_rb = [jnp.where(upto[v_], p[:n, :n], 0.0) for p, v_ in zip(pair, rev)]
    m_rk = [jnp.where(upto[v_], p[:n, n:], 0.0) for p, v_ in zip(pair, rev)]
    l_ab = [jnp.where(before[v_], p[n:, :n], 0.0) for p, v_ in zip(pair, rev)]
    m_ak = [jnp.where(before[v_], p[n:, n:], 0.0) for p, v_ in zip(pair, rev)]

    inv = each(lambda l: eye + l, l_ab)
    power = each(lambda l: _dot(bf(l), bf(l)), l_ab)
    steps = int(math.log2(c)) - 1
    for it in range(steps):
        if it < steps - 1:
            both = each(lambda i, p: _dot(cat0(i, p), bf(p)), inv, power)
            inv = each(lambda i, b: i + b[:n], inv, both)
            power = each(lambda b: b[n:], both)
        else:
            inv = each(lambda i, p: i + _dot(bf(i), bf(p)), inv, power)

    state_and_v = each(cat0, h0, vs)
    w = each(lambda a, m, sv: _dot(cat1(a, m), sv), as_, m_ak, state_and_v)
    y0 = each(lambda r_, m, sv: _dot(cat1(r_, m), sv), rs, m_rk, state_and_v)
    u = each(lambda i, w_: _dot(bf(i), bf(w_)), inv, w)
    y = each(lambda y_, m, u_: y_ + _dot(bf(m), bf(u_)), y0, m_rb, u)
    decayed = each(lambda b, k_, e: jnp.concatenate([b * e, k_ * e], axis=0).T, bs, ks, e_end)
    h_new = each(lambda h, t_, dc, u_, v_: h * jnp.exp(t_) + _dot(bf(dc), cat0(u_, v_)), h0, total, decayed, u, vs)
    return [q[:c] + q[c:] for q in y], h_new


def _wkv_kernel(rf_ref, kf_ref, vf_ref, kkf_ref, lwf_ref, arf_ref, rb_ref, kb_ref, vb_ref, kkb_ref, lwb_ref,
                arb_ref, ka_ref, yf_ref, yb_ref, h_ref, *, pairs):
    @pl.when(pl.program_id(2) == 0)
    def _():
        h_ref[...] = jnp.zeros_like(h_ref)

    dirs = ((rf_ref, kf_ref, vf_ref, kkf_ref, lwf_ref, arf_ref, yf_ref),
            (rb_ref, kb_ref, vb_ref, kkb_ref, lwb_ref, arb_ref, yb_ref))
    chains, outs = [], []
    for d, (r_ref, k_ref, v_ref, kk_ref, lw_ref, ar_ref, y_ref) in enumerate(dirs):
        for p in range(pairs):
            lanes = slice(p * LANES, (p + 1) * LANES)
            chains.append((r_ref[0, :, lanes], k_ref[0, :, lanes], v_ref[0, :, lanes], kk_ref[0, :, lanes],
                           lw_ref[0, :, lanes], ar_ref[0, :, lanes], ka_ref[:, lanes], h_ref[d, p], d == 1))
            outs.append((y_ref, lanes, d, p))
    ys, hs = _wkv_chunks(chains)
    for (y_ref, lanes, d, p), y, h_new in zip(outs, ys, hs):
        y_ref[0, :, lanes] = y
        h_ref[d, p] = h_new


def wkv7(r, k, v, kk, lw, ar, k_a, n_ctx):
    bsz, t, width = r.shape
    c = WKV_CHUNK
    n_cc, n_chunks = n_ctx // c, t // c
    pairs = WKV_PAIRS
    wb = pairs * LANES
    fwd = pl.BlockSpec((1, c, wb), lambda i, p, s: (i, s, p))
    bwd = pl.BlockSpec((1, c, wb), lambda i, p, s: (i, _mirrored_chunk(s, n_cc, n_chunks), p))
    return pl.pallas_call(
        functools.partial(_wkv_kernel, pairs=pairs),
        grid=(bsz, width // wb, n_chunks),
        in_specs=[fwd] * 6 + [bwd] * 6 + [pl.BlockSpec((1, wb), lambda i, p, s: (0, p))],
        out_specs=[fwd, bwd],
        out_shape=[jax.ShapeDtypeStruct((bsz, t, width), F32)] * 2,
        scratch_shapes=[pltpu.VMEM((2, pairs, LANES, LANES), F32)],
        compiler_params=pltpu.CompilerParams(dimension_semantics=("parallel", "parallel", "arbitrary"),
                                             vmem_limit_bytes=VMEM_LIMIT),
        name="wkv7",
    )(r, k, v, kk, lw[0], ar[0], r, k, v, kk, lw[1], ar[1], k_a.reshape(1, width))


def _conv_kernel(x_ref, before_ref, after_ref, w_ref, b_ref, o_ref, *, tm, tiles_per_sample, n_ctx, t_total):
    x, before, after = x_ref[...], before_ref[...], after_ref[...]
    half = SSM_CONV // 2
    acc = x * w_ref[half:half + 1] + b_ref[...]
    for o in range(SSM_CONV):
        if o != half:
            ok = _same_segment(tm, tiles_per_sample, n_ctx, t_total, o - half)
            acc = acc + jnp.where(ok, _row_from(x, before, after, o - half), 0.0) * w_ref[o:o + 1]
    o_ref[...] = jax.nn.silu(acc)


def conv_silu(p, col0, conv_w, conv_b, n_ctx):
    bsz, t, _ = p.shape
    taps, ch = conv_w.shape
    tm = _row_tile(t, 272)
    tps = t // tm
    m = bsz * t
    p2 = p.reshape(m, -1)
    before, after = _halo_specs(tm, m, ch, col0 // ch)
    w_pad = jnp.concatenate([conv_w, jnp.zeros((-taps % 8, ch), F32)], axis=0)
    const = lambda a: pl.BlockSpec(a.shape, lambda i: (0,) * a.ndim)
    out = pl.pallas_call(
        functools.partial(_conv_kernel, tm=tm, tiles_per_sample=tps, n_ctx=n_ctx, t_total=t),
        grid=(m // tm,),
        in_specs=[pl.BlockSpec((tm, ch), lambda i: (i, col0 // ch)), before, after, const(w_pad),
                  pl.BlockSpec((1, ch), lambda i: (0, 0))],
        out_specs=pl.BlockSpec((tm, ch), lambda i: (i, 0)),
        out_shape=jax.ShapeDtypeStruct((m, ch), F32),
        compiler_params=pltpu.CompilerParams(dimension_semantics=("parallel",), vmem_limit_bytes=VMEM_LIMIT),
        name="conv_silu",
    )(p2, p2, p2, w_pad, conv_b.reshape(1, ch))
    return out.reshape(bsz, t, ch)


def _ssd_kernel(xf_ref, dtf_ref, xb_ref, dtb_ref, bias_ref, arow_ref, tabw_ref, tabx_ref, yf_ref, yb_ref,
                s_ref, ct_ref, cp_ref, *, inner, heads_per_group):
    L, hp = SSD_CHUNK, SSM_HEAD_DIM
    e_heads = heads_per_group
    width = e_heads * hp
    gn = SSM_GROUPS * SSM_STATE
    n_heads = SSM_GROUPS * e_heads

    @pl.when(pl.program_id(1) == 0)
    def _():
        s_ref[...] = jnp.zeros_like(s_ref)

    rl = lax.broadcasted_iota(jnp.int32, (L, L), 0)
    cl = lax.broadcasted_iota(jnp.int32, (L, L), 1)
    upto = (rl >= cl, rl <= cl)
    dirs = ((xf_ref, dtf_ref, yf_ref), (xb_ref, dtb_ref, yb_ref))
    for d, (_, dt_ref, _) in enumerate(dirs):
        dt_all = jax.nn.softplus(dt_ref[0] + bias_ref[...])
        cum = _dot(upto[d].astype(F32), dt_all * arow_ref[...], HIGHEST)
        ct_ref[d] = cum.T
        for i, piece in enumerate(_bf16_pieces(cum, 3) + _bf16_pieces(dt_all, 2)):
            cp_ref[d, i] = piece

    lane = lax.broadcasted_iota(jnp.int32, (L, LANES), 1)
    lane_head = lax.broadcasted_iota(jnp.int32, (L, width), 1) // hp

    def group(g, carry):
        two = range(2)
        xcol = pl.ds(pl.multiple_of(g * width, width), width)
        bcol = pl.ds(pl.multiple_of(inner + g * SSM_STATE, SSM_STATE), SSM_STATE)
        ccol = pl.ds(pl.multiple_of(inner + gn + g * SSM_STATE, SSM_STATE), SSM_STATE)
        x = [dirs[d][0][0, :, xcol] for d in two]
        bm = [dirs[d][0][0, :, bcol] for d in two]
        cmb = [dirs[d][0][0, :, ccol].astype(BF16) for d in two]
        tabw = [tabw_ref[d, g] for d in two]
        tabx = [tabx_ref[d, g] for d in two]
        cum_w = [_dot(cp_ref[d, 0], tabw[d]) + _dot(cp_ref[d, 1], tabw[d]) + _dot(cp_ref[d, 2], tabw[d])
                 for d in two]
        dt_x = [_dot(cp_ref[d, 3], tabx[d]) + _dot(cp_ref[d, 4], tabx[d]) for d in two]
        s0 = [s_ref[d, g] for d in two]
        cb = [_dot_nt(cmb[d], bm[d].astype(BF16)) for d in two]
        y_off = [_dot(cmb[d], s0[d].astype(BF16)) for d in two]
        bt = [bm[d].T.astype(BF16) for d in two]
        cum_x = [jnp.concatenate(
            [jnp.where(lane < hp, cum_w[d][:, (2 * q) * LANES:(2 * q + 1) * LANES],
                       cum_w[d][:, (2 * q + 1) * LANES:(2 * q + 2) * LANES]) for q in range(e_heads // 2)],
            axis=1) for d in two]
        tot_x = [cum_x[d][0:1] if d == 1 else cum_x[d][L - 1:L] for d in two]
        xdt = [x[d] * dt_x[d] for d in two]
        lmats, xmasked = [], []
        for d in two:
            for e in range(e_heads):
                row = d * n_heads + g * e_heads + e
                seg = cum_w[d][:, e * LANES:(e + 1) * LANES] - ct_ref[d, pl.ds(row, 1), :]
                lmats.append((cb[d] * jnp.where(upto[d], jnp.exp(jnp.minimum(seg, 0.0)), 0.0)).astype(BF16))
                xmasked.append(jnp.where(lane_head == e, xdt[d], 0.0).astype(BF16))
        y_diag = [_dot(m, xm) for m, xm in zip(lmats, xmasked)]
        s_add = [_dot(bt[d], (xdt[d] * jnp.exp(tot_x[d] - cum_x[d])).astype(BF16)) for d in two]
        for d in two:
            y = y_off[d] * jnp.exp(cum_x[d])
            for e in range(e_heads):
                y = y + y_diag[d * e_heads + e]
            dirs[d][2][0, :, xcol] = y
            s_ref[d, g] = s0[d] * jnp.exp(tot_x[d]) + s_add[d]
        return carry

    lax.fori_loop(0, SSM_GROUPS, group, 0, unroll=2)


def ssd(xbc, p_c, dt_col, dt_bias, a_log, n_ctx, inner):
    bsz, t, xw = xbc.shape
    L = SSD_CHUNK
    n_heads = dt_bias.shape[1]
    e_heads = n_heads // SSM_GROUPS
    width = e_heads * SSM_HEAD_DIM
    n_cc, n_chunks = n_ctx // L, t // L
    pad = LANES - 2 * n_heads
    bias = jnp.concatenate([dt_bias[0], dt_bias[1], jnp.zeros((pad,), F32)]).reshape(1, LANES)
    arow = jnp.concatenate([-jnp.exp(a_log[0]), -jnp.exp(a_log[1]), jnp.zeros((pad,), F32)]).reshape(1, LANES)
    head_row = (jnp.arange(2)[:, None, None, None] * n_heads + jnp.arange(SSM_GROUPS)[None, :, None, None] * e_heads)
    src = jnp.arange(LANES)[None, None, :, None]
    tabw = (src == head_row + jnp.arange(e_heads * LANES)[None, None, None, :] // LANES).astype(BF16)
    tabx = (src == head_row + jnp.arange(width)[None, None, None, :] // SSM_HEAD_DIM).astype(BF16)

    mirrored = lambda s: _mirrored_chunk(s, n_cc, n_chunks)
    const = lambda a: pl.BlockSpec(a.shape, lambda i, s: (0,) * a.ndim)
    return pl.pallas_call(
        functools.partial(_ssd_kernel, inner=inner, heads_per_group=e_heads),
        grid=(bsz, n_chunks),
        in_specs=[
            pl.BlockSpec((1, L, xw), lambda i, s: (i, s, 0)),
            pl.BlockSpec((1, L, LANES), lambda i, s: (i, s, dt_col // LANES)),
            pl.BlockSpec((1, L, xw), lambda i, s: (i, mirrored(s), 0)),
            pl.BlockSpec((1, L, LANES), lambda i, s: (i, mirrored(s), dt_col // LANES)),
            const(bias), const(arow), const(tabw), const(tabx),
        ],
        out_specs=[pl.BlockSpec((1, L, inner), lambda i, s: (i, s, 0)),
                   pl.BlockSpec((1, L, inner), lambda i, s: (i, mirrored(s), 0))],
        out_shape=[jax.ShapeDtypeStruct((bsz, t, inner), F32)] * 2,
        scratch_shapes=[pltpu.VMEM((2, SSM_GROUPS, SSM_STATE, width), F32),
                        pltpu.VMEM((2, LANES, L), F32),
                        pltpu.VMEM((2, 5, L, LANES), BF16)],
        compiler_params=pltpu.CompilerParams(dimension_semantics=("parallel", "arbitrary"),
                                             vmem_limit_bytes=VMEM_LIMIT),
        name="ssd",
    )(xbc, p_c, xbc, p_c, bias, arow, tabw, tabx)


def _merge_kernel(x_ref, yf_ref, yb_ref, r_ref, k_ref, v_ref, og_ref, z_ref, grw_ref, gssm_ref, gl_ref, gc_ref,
                  lnw_ref, lnb_ref, rk_ref, wrw_ref, wout_ref, o_ref, *, tm, tiles_per_sample, n_ctx):
    y = yf_ref[...] + yb_ref[...]
    inv_n = 1.0 / RW_HEAD_DIM
    centred = y - _group_sums(y, RW_HEAD_DIM) * inv_n
    var = _group_sums(centred * centred, RW_HEAD_DIM) * inv_n
    y = centred * lax.rsqrt(var + RW_GN_EPS) * lnw_ref[...] + lnb_ref[...]
    y = y + _group_sums(r_ref[...] * k_ref[...] * rk_ref[...], RW_HEAD_DIM) * v_ref[...]
    y_rw = y * og_ref[...]
    t1 = _dot(y_rw.astype(wrw_ref.dtype), wrw_ref[...])
    merged = jax.nn.sigmoid(grw_ref[...]) * t1 + jax.nn.sigmoid(gssm_ref[...]) * z_ref[...]
    mix = _dot(merged.astype(wout_ref.dtype), wout_ref[...])
    gate = jnp.where(_ctx_rows(tm, tiles_per_sample, n_ctx), gc_ref[...], gl_ref[0])
    o_ref[...] = x_ref[...] + gate * mix


def merge(x, y_f, y_b, r, k, v, out_gate, z_ssm, proj, col_grw, col_gssm, gate_l, gate_c, ln_w, ln_b, r_k,
          w_rw, w_out, n_ctx):
    bsz, t, d = x.shape
    tm = _row_tile(t, 272)
    tps = t // tm
    m = bsz * t
    rows = lambda blk: pl.BlockSpec((tm, d), lambda i: (i, blk))
    const = lambda shape: pl.BlockSpec(shape, lambda i: (0,) * len(shape))
    flat = lambda a: a.reshape(m, -1)
    out = pl.pallas_call(
        functools.partial(_merge_kernel, tm=tm, tiles_per_sample=tps, n_ctx=n_ctx),
        grid=(m // tm,),
        in_specs=[rows(0)] * 8 + [rows(col_grw // d), rows(col_gssm // d),
                                  pl.BlockSpec((1, 1, d), lambda i: (i // tps, 0, 0)), const((1, d)),
                                  const((1, d)), const((1, d)), const((1, d)),
                                  const(w_rw.shape), const(w_out.shape)],
        out_specs=rows(0),
        out_shape=jax.ShapeDtypeStruct((m, d), F32),
        compiler_params=pltpu.CompilerParams(dimension_semantics=("parallel",), vmem_limit_bytes=VMEM_LIMIT),
        name="merge",
    )(flat(x), flat(y_f), flat(y_b), flat(r), flat(k), flat(v), flat(out_gate), flat(z_ssm), flat(proj), flat(proj),
      gate_l.reshape(bsz, 1, d), gate_c.reshape(1, d), ln_w.reshape(1, d), ln_b.reshape(1, d), r_k.reshape(1, d),
      w_rw, w_out)
    return out.reshape(bsz, t, d)


def _ssm_out_kernel(yf_ref, yb_ref, xs_ref, z_ref, dskip_ref, nw_ref, w_ref, o_ref, *, group):
    y = (yf_ref[...] + yb_ref[...] + dskip_ref[...] * xs_ref[...]) * jax.nn.silu(z_ref[...])
    ms = _group_sums(y * y, group) * (1.0 / group)
    y = y * lax.rsqrt(ms + SSM_NORM_EPS) * nw_ref[...]
    o_ref[...] = _dot(y.astype(w_ref.dtype), w_ref[...])


def ssm_out(y_f, y_b, xbc, p_c, z_col, d_skip, norm_w, w):
    bsz, t, inner = y_f.shape
    m = bsz * t
    d = w.shape[1]
    tm = _row_tile(t, 272)
    rows = lambda blk: pl.BlockSpec((tm, inner), lambda i: (i, blk))
    const = lambda shape: pl.BlockSpec(shape, lambda i: (0,) * len(shape))
    flat = lambda a: a.reshape(m, -1)
    out = pl.pallas_call(
        functools.partial(_ssm_out_kernel, group=inner // SSM_GROUPS),
        grid=(m // tm,),
        in_specs=[rows(0), rows(0), rows(0), rows(z_col // inner), const((1, inner)), const((1, inner)),
                  const(w.shape)],
        out_specs=pl.BlockSpec((tm, d), lambda i: (i, 0)),
        out_shape=jax.ShapeDtypeStruct((m, d), F32),
        compiler_params=pltpu.CompilerParams(dimension_semantics=("parallel",), vmem_limit_bytes=VMEM_LIMIT),
        name="ssm_out",
    )(flat(y_f), flat(y_b), flat(xbc), flat(p_c), d_skip.reshape(1, inner), norm_w.reshape(1, inner), w)
    return out.reshape(bsz, t, d)


def _router_kernel(x_ref, g_ref, scl_ref, shl_ref, scc_ref, shc_ref, wr_ref, h_ref, logit_ref, *,
                   tm, tiles_per_sample, n_ctx):
    is_ctx = _ctx_rows(tm, tiles_per_sample, n_ctx)
    h = _modulated(x_ref[...], g_ref[...], scl_ref[0], shl_ref[0], scc_ref[...], shc_ref[...], is_ctx)
    bits = pltpu.bitcast(h.astype(BF16).astype(F32), jnp.uint32)
    half = h.shape[1] // 2
    h_ref[...] = (bits[:, :half] >> 16) | (bits[:, half:] & jnp.uint32(0xFFFF0000))
    logit_ref[...] = _dot(h, wr_ref[...], HIGHEST)


def router(x, gain, scale_l, shift_l, scale_c, shift_c, w_router_padded, n_ctx):
    bsz, t, d = x.shape
    tm = _row_tile(t, 544)
    tps = t // tm
    m = bsz * t
    row = lambda v: v.reshape(1, d)
    const = lambda shape: pl.BlockSpec(shape, lambda i: (0,) * len(shape))
    per_sample = pl.BlockSpec((1, 1, d), lambda i: (i // tps, 0, 0))
    return pl.pallas_call(
        functools.partial(_router_kernel, tm=tm, tiles_per_sample=tps, n_ctx=n_ctx),
        grid=(m // tm,),
        in_specs=[pl.BlockSpec((tm, d), lambda i: (i, 0)), const((1, d)), per_sample, per_sample,
                  const((1, d)), const((1, d)), const(w_router_padded.shape)],
        out_specs=[pl.BlockSpec((tm, d // 2), lambda i: (i, 0)), pl.BlockSpec((tm, LANES), lambda i: (i, 0))],
        out_shape=[jax.ShapeDtypeStruct((m, d // 2), jnp.uint32), jax.ShapeDtypeStruct((m, LANES), F32)],
        compiler_params=pltpu.CompilerParams(dimension_semantics=("parallel",), vmem_limit_bytes=VMEM_LIMIT),
        name="router",
    )(x.reshape(m, d), row(gain), scale_l.reshape(bsz, 1, d), shift_l.reshape(bsz, 1, d),
      row(scale_c), row(shift_c), w_router_padded)


def _experts_kernel(be_ref, nb_ref, x_ref, w1_ref, w3_ref, w2_ref, o_ref):
    i = pl.program_id(0)

    @pl.when(i < nb_ref[0])
    def _():
        packed = x_ref[...]
        x = jnp.concatenate([pltpu.bitcast(packed << 16, F32),
                             pltpu.bitcast(packed & jnp.uint32(0xFFFF0000), F32)], axis=1).astype(BF16)
        hidden = jax.nn.silu(_dot(x, w1_ref[0])) * _dot(x, w3_ref[0])
        o_ref[...] = _dot(hidden.astype(w2_ref.dtype), w2_ref[0])

    @pl.when(i >= nb_ref[0])
    def _():
        o_ref[...] = jnp.zeros_like(o_ref)


def experts(xb, block_expert, n_used, w1, w3, w2):
    n_rows = xb.shape[0]
    d, de = w1.shape[1:]
    rows = MOE_ROWS
    n_blocks = n_rows // rows
    return pl.pallas_call(
        _experts_kernel,
        grid_spec=pltpu.PrefetchScalarGridSpec(
            num_scalar_prefetch=2,
            grid=(n_blocks,),
            in_specs=[
                pl.BlockSpec((rows, d // 2), lambda i, be, nb: (i, 0)),
                pl.BlockSpec((1, d, de), lambda i, be, nb: (be[i], 0, 0)),
                pl.BlockSpec((1, d, de), lambda i, be, nb: (be[i], 0, 0)),
                pl.BlockSpec((1, de, d), lambda i, be, nb: (be[i], 0, 0)),
            ],
            out_specs=pl.BlockSpec((rows, d), lambda i, be, nb: (i, 0)),
        ),
        out_shape=jax.ShapeDtypeStruct((n_rows, d), F32),
        compiler_params=pltpu.CompilerParams(dimension_semantics=("arbitrary",), vmem_limit_bytes=VMEM_LIMIT),
        name="experts",
    )(block_expert, n_used, xb, w1, w3, w2)


def _combine_kernel(x_ref, y_ref, gate_ref, gl_ref, gc_ref, fin_ref, o_ref, *, tm, tiles_per_sample, n_ctx, final):
    d = x_ref.shape[1]
    g = gate_ref[...]
    f = y_ref[:, :d] * g[:, 0:1] + y_ref[:, d:] * g[:, 1:2]
    mod = jnp.where(_ctx_rows(tm, tiles_per_sample, n_ctx), gc_ref[...], gl_ref[0])
    out = x_ref[...] + mod * f
    if final:
        out = out * lax.rsqrt(jnp.mean(out * out, axis=-1, keepdims=True) + NORM_EPS) * fin_ref[...]
    o_ref[...] = out


def combine(x, y_pairs, gates, gate_l, gate_c, final_gain, n_ctx, final):
    bsz, t, d = x.shape
    m = bsz * t
    tm = _row_tile(t, 544)
    tps = t // tm
    const = lambda shape: pl.BlockSpec(shape, lambda i: (0,) * len(shape))
    out = pl.pallas_call(
        functools.partial(_combine_kernel, tm=tm, tiles_per_sample=tps, n_ctx=n_ctx, final=final),
        grid=(m // tm,),
        in_specs=[pl.BlockSpec((tm, d), lambda i: (i, 0)), pl.BlockSpec((tm, TOP_K * d), lambda i: (i, 0)),
                  pl.BlockSpec((tm, LANES), lambda i: (i, 0)), pl.BlockSpec((1, 1, d), lambda i: (i // tps, 0, 0)),
                  const((1, d)), const((1, d))],
        out_specs=pl.BlockSpec((tm, d), lambda i: (i, 0)),
        out_shape=jax.ShapeDtypeStruct((m, d), F32),
        compiler_params=pltpu.CompilerParams(dimension_semantics=("parallel",), vmem_limit_bytes=VMEM_LIMIT),
        name="combine",
    )(x.reshape(m, d), y_pairs.reshape(m, TOP_K * d), gates, gate_l.reshape(bsz, 1, d), gate_c.reshape(1, d),
      final_gain.reshape(1, d))
    return out.reshape(bsz, t, d)


def _top2(vals):
    idx = jnp.arange(vals.shape[-1], dtype=jnp.int32)
    i1 = jnp.argmax(vals, axis=-1).astype(jnp.int32)
    v1 = jnp.max(vals, axis=-1)
    rest = jnp.where(idx == i1[..., None], -jnp.inf, vals)
    i2 = jnp.argmax(rest, axis=-1).astype(jnp.int32)
    v2 = jnp.max(rest, axis=-1)
    return v1, i1, v2, i2


def moe(h, logits, b_router, w1, w3, w2):
    n_tok, d = h.shape
    n_exp = b_router.shape[0]
    epg = n_exp // N_EXPERT_GROUPS
    scores = jax.nn.sigmoid(logits)
    biased = (scores + b_router.astype(F32)).reshape(n_tok, N_EXPERT_GROUPS, epg)
    g1, _, g2, _ = _top2(biased)
    top_group = jnp.argmax(g1 + g2, axis=-1).astype(jnp.int32)
    in_top = jnp.arange(N_EXPERT_GROUPS, dtype=jnp.int32)[None, :, None] == top_group[:, None, None]
    in_group = jnp.sum(jnp.where(in_top, biased, 0.0), axis=1)
    _, l1, _, l2 = _top2(in_group)
    expert = top_group[:, None] * epg + jnp.stack([l1, l2], axis=-1)
    picked = expert[:, :, None] == jnp.arange(n_exp, dtype=jnp.int32)[None, None, :]
    gate = jnp.sum(jnp.where(picked, scores[:, None, :], 0.0), axis=-1)
    gate = gate / jnp.sum(gate, axis=-1, keepdims=True)

    n_assign = n_tok * TOP_K
    flat_e = expert.reshape(-1).astype(jnp.int32)
    onehot = picked.reshape(n_assign, n_exp).astype(jnp.int32)
    rank = jnp.sum((jnp.cumsum(onehot, axis=0) - onehot) * onehot, axis=-1)
    counts = jnp.sum(onehot, axis=0)
    padded = (counts + MOE_ROWS - 1) // MOE_ROWS * MOE_ROWS
    pad_end = jnp.cumsum(padded)
    dest = ((pad_end - padded)[flat_e] + rank).astype(jnp.int32)
    n_blocks = -(-n_assign // MOE_ROWS) + n_exp
    slot_token = jnp.zeros((n_blocks * MOE_ROWS,), jnp.int32).at[dest].set(
        jnp.arange(n_assign, dtype=jnp.int32) // TOP_K)
    block_expert = jnp.clip(
        jnp.sum(jnp.arange(n_blocks, dtype=jnp.int32)[:, None] >= (pad_end // MOE_ROWS)[None, :], axis=-1),
        0, n_exp - 1).astype(jnp.int32)
    n_used = (pad_end[-1:] // MOE_ROWS).astype(jnp.int32)
    yb = experts(h[slot_token], block_expert, n_used, w1, w3, w2)
    return yb[dest], jnp.pad(gate, ((0, 0), (0, LANES - TOP_K)))


def _ssm_branch(p, prm, w_branch, n_ctx):
    inner = prm["norm_w"].shape[0]
    xbc_w = prm["conv_w"].shape[1]
    xbc = conv_silu(p, 0, prm["conv_w"], prm["conv_b"], n_ctx)
    y_f, y_b = ssd(xbc, p, xbc_w + inner, prm["dt_bias"], prm["a_log"], n_ctx, inner)
    d_skip = jnp.repeat(prm["d"][0] + prm["d"][1], SSM_HEAD_DIM)
    return ssm_out(y_f, y_b, xbc, p, xbc_w, d_skip, prm["norm_w"], w_branch)


def _pad_cols(w, n):
    return jnp.pad(w, ((0, 0), (0, n - w.shape[1])))


def kernel(x, c, ctx, c_ctx, w_mod, b_mod, norm_mix_g, w_in, rw_shift_mu, rw_w0, rw_w2, rw_a0, rw_a2, rw_g2,
           rw_k_k, rw_k_a, rw_r_k, rw_ln_w, rw_ln_b, ssm_conv_w, ssm_conv_b, ssm_dt_bias, ssm_a_log, ssm_d,
           ssm_norm_w, w_branch_rw, w_branch_ssm, w_out, norm_ffn_g, w_router, b_router, exp_w1, exp_w3, exp_w2,
           norm_final_g):
    bsz, n_lat, d = x.shape
    depth = w_in.shape[0]
    n_ctx = ctx.shape[1]
    rows = n_lat // GRID_W
    t = n_ctx + n_lat
    width = RW_HEADS * RW_HEAD_DIM
    dl, al, gl = rw_w2.shape[2], rw_a2.shape[2], rw_g2.shape[1]
    inner = ssm_norm_w.shape[1]
    xbc_w = ssm_conv_w.shape[2]
    n_heads = ssm_dt_bias.shape[2]
    n_exp = w_router.shape[1]

    def to_c(a):
        ch = a.shape[-1]
        lat = a[:, n_ctx:].reshape(bsz, rows, GRID_W, ch).transpose(0, 2, 1, 3).reshape(bsz, n_lat, ch)
        return jnp.concatenate([a[:, :n_ctx], lat], axis=1)

    def to_r(a):
        ch = a.shape[-1]
        lat = a[:, n_ctx:].reshape(bsz, GRID_W, rows, ch).transpose(0, 2, 1, 3).reshape(bsz, n_lat, ch)
        return jnp.concatenate([a[:, :n_ctx], lat], axis=1)

    o = 0
    src = {}
    for name, size in (("r", width), ("w_f", dl), ("w_b", dl), ("k", width), ("v", width), ("a_f", al),
                       ("a_b", al), ("g", gl), ("z", inner), ("xbc", xbc_w), ("dt", 2 * n_heads),
                       ("g_rw", d), ("g_ssm", d)):
        src[name] = (o, size)
        o += size
    take = lambda a, name: a[..., src[name][0]:src[name][0] + src[name][1]]
    lora_w = 2 * dl + 2 * al
    lora_pad = -(-lora_w // LANES) * LANES
    g_pad = -(-gl // LANES) * LANES
    dt_pad = -(-2 * n_heads // LANES) * LANES
    lay = {"r": (0, width), "k": (width, width), "v": (2 * width, width), "g_rw": (3 * width, d),
           "g_ssm": (3 * width + d, d), "lora": (3 * width + 2 * d, lora_w),
           "g": (3 * width + 2 * d + lora_pad, gl)}

    silu_c = jax.nn.silu(c)
    silu_cc = jax.nn.silu(c_ctx)[None, :]
    act = jnp.concatenate([silu_c, silu_cc, jnp.zeros((-(bsz + 1) % 8, d), F32)], axis=0)
    w_router_p = _pad_cols(w_router.astype(F32), LANES)

    xr = jnp.concatenate([ctx, x], axis=1)
    for l in range(depth):
        mod = matmul(act, w_mod[l], precision=HIGHEST, name="modulation")[:bsz + 1] + b_mod[l]
        shift_m, scale_m, gate_m, shift_f, scale_f, gate_f = jnp.split(mod[:bsz], 6, axis=-1)
        cshift_m, cscale_m, cgate_m, cshift_f, cscale_f, cgate_f = jnp.split(mod[bsz], 6, axis=-1)

        wl = w_in[l]
        w_r = jnp.concatenate(
            [take(wl, "r"), take(wl, "k"), take(wl, "v"), take(wl, "g_rw"), take(wl, "g_ssm"),
             _pad_cols(jnp.concatenate([take(wl, n) for n in ("w_f", "w_b", "a_f", "a_b")], axis=1), lora_pad),
             _pad_cols(take(wl, "g"), g_pad)], axis=1).astype(BF16)
        w_c = jnp.concatenate([take(wl, "xbc"), take(wl, "z"), _pad_cols(take(wl, "dt"), dt_pad)],
                              axis=1).astype(BF16)
        mods = (norm_mix_g[l], scale_m, shift_m, cscale_m, cshift_m)
        p_r = norm_proj(xr, *mods, w_r, n_ctx)
        p_c = norm_proj(to_c(xr), *mods, w_c, n_ctx)

        mu = rw_shift_mu[l]
        rw_prm = dict(
            shift_mu={"r": take(mu, "r"), "k": take(mu, "k"), "v": take(mu, "v"), "g": take(mu, "g"),
                      "lora": jnp.concatenate([take(mu, n) for n in ("w_f", "w_b", "a_f", "a_b")])},
            w0=rw_w0[l], w2=rw_w2[l], a0=rw_a0[l], a2=rw_a2[l], g2=rw_g2[l],
            k_k=rw_k_k[l], k_a=rw_k_a[l], r_k=rw_r_k[l], ln_w=rw_ln_w[l], ln_b=rw_ln_b[l])
        r, k, v, kk, lw_f, lw_b, ar_f, ar_b, out_gate = rwkv_prep(p_r, lay, rw_prm, n_ctx)
        y_f, y_b = wkv7(r, k, v, kk, (lw_f, lw_b), (ar_f, ar_b), rw_k_a[l], n_ctx)
        ssm_prm = dict(conv_w=ssm_conv_w[l], conv_b=ssm_conv_b[l], dt_bias=ssm_dt_bias[l], a_log=ssm_a_log[l],
                       d=ssm_d[l], norm_w=ssm_norm_w[l])
        z_ssm = to_r(_ssm_branch(p_c, ssm_prm, w_branch_ssm[l].astype(BF16), n_ctx))
        xr = merge(xr, y_f, y_b, r, k, v, out_gate, z_ssm, p_r, lay["g_rw"][0], lay["g_ssm"][0], gate_m, cgate_m,
                   rw_ln_w[l], rw_ln_b[l], rw_r_k[l], w_branch_rw[l].astype(BF16), w_out[l].astype(BF16), n_ctx)

        h, logits = router(xr, norm_ffn_g[l], scale_f, shift_f, cscale_f, cshift_f, w_router_p, n_ctx)
        y_pairs, gates = moe(h, logits[:, :n_exp], b_router, exp_w1[l].astype(BF16), exp_w3[l].astype(BF16),
                             exp_w2[l].astype(BF16))
        xr = combine(xr, y_pairs, gates, gate_f, cgate_f, norm_final_g, n_ctx, final=(l == depth - 1))
    return xr[:, n_ctx:]
```

```python
import functools
import math

import jax
import jax.numpy as jnp
from jax import lax
from jax.experimental import pallas as pl
from jax.experimental.pallas import tpu as pltpu

F32 = jnp.float32
BF16 = jnp.bfloat16
HIGHEST = lax.Precision.HIGHEST

GRID_W = 64
RW_HEADS = 16
RW_HEAD_DIM = 64
RW_GN_EPS = 64e-5
SSM_HEAD_DIM = 64
SSM_GROUPS = 8
SSM_STATE = 128
SSM_CONV = 5
SSM_NORM_EPS = 1e-5
N_EXPERT_GROUPS = 4
TOP_K = 2
NORM_EPS = 1e-6

LANES = 128
WKV_CHUNK = 64
WKV_PAIRS = 4
SSD_CHUNK = 128
MOE_ROWS = 256
VMEM_LIMIT = 56 * 1024 * 1024


def _dot(a, b, precision=None):
    return jnp.dot(a, b, preferred_element_type=F32, precision=precision)


def _dot_nt(a, b, precision=None):
    return lax.dot_general(a, b, (((1,), (1,)), ((), ())), preferred_element_type=F32, precision=precision)


def _dot_tn(a, b, precision=None):
    return lax.dot_general(a, b, (((0,), (0,)), ((), ())), preferred_element_type=F32, precision=precision)


def _row_tile(n_rows_per_sample, limit=1088):
    for tm in (1088, 544, 512, 272, 256, 128, 64, 32, 16):
        if tm <= limit and n_rows_per_sample % tm == 0:
            return tm
    raise ValueError(n_rows_per_sample)


def _col_tile(n_cols, limit=1536):
    best = LANES
    for k in range(1, n_cols // LANES + 1):
        tn = k * LANES
        if n_cols % tn == 0 and tn <= limit:
            best = tn
    return best


def _bf16_pieces(x, n):
    pieces = []
    for _ in range(n):
        p = x.astype(BF16)
        pieces.append(p)
        x = x - p.astype(F32)
    return pieces


def _group_sums(x, group):
    span = max(group, LANES)
    rr = lax.broadcasted_iota(jnp.int32, (span, span), 0) // group
    cc = lax.broadcasted_iota(jnp.int32, (span, span), 1) // group
    ones = (rr == cc).astype(BF16)
    pieces = _bf16_pieces(x, 3)
    cols = []
    for j in range(x.shape[1] // span):
        sl = slice(j * span, (j + 1) * span)
        cols.append(_dot(pieces[0][:, sl], ones) + _dot(pieces[1][:, sl], ones) + _dot(pieces[2][:, sl], ones))
    return cols[0] if len(cols) == 1 else jnp.concatenate(cols, axis=1)


HALO = 16
ACT = BF16


def _row_from(x, before, after, offset):
    tm = x.shape[0]
    rows = lax.broadcasted_iota(jnp.int32, (tm, 1), 0)
    k = abs(offset)
    if offset < 0:
        out = pltpu.roll(x, k, 0)
        for j in range(k):
            out = jnp.where(rows == j, before[HALO - k + j:HALO - k + j + 1], out)
    else:
        out = pltpu.roll(x, tm - k, 0)
        for j in range(k):
            out = jnp.where(rows == tm - k + j, after[j:j + 1], out)
    return out


def _same_segment(tm, tiles_per_sample, n_ctx, t_total, offset):
    t = (pl.program_id(0) % tiles_per_sample) * tm + lax.broadcasted_iota(jnp.int32, (tm, 1), 0)
    src = t + offset
    return (src >= 0) & (src < t_total) & ((src >= n_ctx) == (t >= n_ctx))


def _halo_specs(tm, n_rows, width, col_block):
    last = n_rows // HALO - 1
    return (pl.BlockSpec((HALO, width), lambda i: (jnp.maximum(i * (tm // HALO) - 1, 0), col_block)),
            pl.BlockSpec((HALO, width), lambda i: (jnp.minimum((i + 1) * (tm // HALO), last), col_block)))


def _mirrored_chunk(s, n_ctx_chunks, n_chunks):
    return jnp.where(s < n_ctx_chunks, n_ctx_chunks - 1 - s, n_chunks - 1 + n_ctx_chunks - s)


def _modulated(x, g, sc_l, sh_l, sc_c, sh_c, is_ctx):
    y = x * lax.rsqrt(jnp.mean(x * x, axis=-1, keepdims=True) + NORM_EPS) * g
    return y * (1.0 + jnp.where(is_ctx, sc_c, sc_l)) + jnp.where(is_ctx, sh_c, sh_l)


def _ctx_rows(tm, tiles_per_sample, n_ctx):
    row0 = (pl.program_id(0) % tiles_per_sample) * tm
    return row0 + lax.broadcasted_iota(jnp.int32, (tm, 1), 0) < n_ctx


def _norm_proj_kernel(x_ref, g_ref, scl_ref, shl_ref, scc_ref, shc_ref, w_ref, o_ref, h_ref, *,
                      tm, tiles_per_sample, n_ctx):
    @pl.when(pl.program_id(1) == 0)
    def _():
        is_ctx = _ctx_rows(tm, tiles_per_sample, n_ctx)
        h = _modulated(x_ref[...], g_ref[...], scl_ref[0], shl_ref[0], scc_ref[...], shc_ref[...], is_ctx)
        h_ref[...] = h.astype(h_ref.dtype)

    o_ref[...] = _dot(h_ref[...], w_ref[...]).astype(o_ref.dtype)


def norm_proj(x, gain, scale_l, shift_l, scale_c, shift_c, w, n_ctx, out_dtype=F32):
    bsz, t, d = x.shape
    n = w.shape[1]
    tm = _row_tile(t)
    tps = t // tm
    tn = _col_tile(n)
    row = lambda v: v.reshape(1, d)
    out = pl.pallas_call(
        functools.partial(_norm_proj_kernel, tm=tm, tiles_per_sample=tps, n_ctx=n_ctx),
        grid=(bsz * tps, n // tn),
        in_specs=[
            pl.BlockSpec((tm, d), lambda i, j: (i, 0)),
            pl.BlockSpec((1, d), lambda i, j: (0, 0)),
            pl.BlockSpec((1, 1, d), lambda i, j: (i // tps, 0, 0)),
            pl.BlockSpec((1, 1, d), lambda i, j: (i // tps, 0, 0)),
            pl.BlockSpec((1, d), lambda i, j: (0, 0)),
            pl.BlockSpec((1, d), lambda i, j: (0, 0)),
            pl.BlockSpec((d, tn), lambda i, j: (0, j)),
        ],
        out_specs=pl.BlockSpec((tm, tn), lambda i, j: (i, j)),
        out_shape=jax.ShapeDtypeStruct((bsz * t, n), out_dtype),
        scratch_shapes=[pltpu.VMEM((tm, d), w.dtype)],
        compiler_params=pltpu.CompilerParams(dimension_semantics=("parallel", "arbitrary"),
                                             vmem_limit_bytes=VMEM_LIMIT),
        name="norm_proj",
    )(x.reshape(bsz * t, d), row(gain), scale_l.reshape(bsz, 1, d), shift_l.reshape(bsz, 1, d),
      row(scale_c), row(shift_c), w)
    return out.reshape(bsz, t, n)


def _matmul_kernel(x_ref, w_ref, o_ref, *, act, precision):
    x = x_ref[...]
    if act == "tanh":
        x = jnp.tanh(x)
    elif act == "sigmoid":
        x = jax.nn.sigmoid(x)
    o_ref[...] = _dot(x.astype(w_ref.dtype), w_ref[...], precision).astype(o_ref.dtype)


def matmul(x, w, act=None, precision=None, out_dtype=F32, name="matmul"):
    m, k = x.shape
    n = w.shape[1]
    tm = _row_tile(m) if m >= 16 else m
    tn = _col_tile(n)
    return pl.pallas_call(
        functools.partial(_matmul_kernel, act=act, precision=precision),
        grid=(m // tm, n // tn),
        in_specs=[pl.BlockSpec((tm, k), lambda i, j: (i, 0)), pl.BlockSpec((k, tn), lambda i, j: (0, j))],
        out_specs=pl.BlockSpec((tm, tn), lambda i, j: (i, j)),
        out_shape=jax.ShapeDtypeStruct((m, n), out_dtype),
        compiler_params=pltpu.CompilerParams(dimension_semantics=("parallel", "parallel"),
                                             vmem_limit_bytes=VMEM_LIMIT),
        name=name,
    )(x, w)


def _rwkv_prep_kernel(m_ref, mb_ref, ma_ref, s_ref, sb_ref, sa_ref, mu_m_ref, mu_s_ref, kk_ref, bias_ref,
                      wl_ref, g2_ref, r_ref, k_ref, v_ref, kko_ref, lwf_ref, lwb_ref, arf_ref, arb_ref, og_ref, *,
                      tm, tiles_per_sample, n_ctx, t_total, width, n_decay):
    has_prev = _same_segment(tm, tiles_per_sample, n_ctx, t_total, -1)
    has_next = _same_segment(tm, tiles_per_sample, n_ctx, t_total, 1)

    def shifted(x_ref, before_ref, after_ref, mu_ref):
        x, before, after = (ref[...].astype(F32) for ref in (x_ref, before_ref, after_ref))
        near =(jnp.where(has_prev, _row_from(x, before, after, -1), 0.0)
                + jnp.where(has_next, _row_from(x, before, after, 1), 0.0))
        return x + mu_ref[...] * (0.5 * near - x)

    main = shifted(m_ref, mb_ref, ma_ref, mu_m_ref)
    small = shifted(s_ref, sb_ref, sa_ref, mu_s_ref)
    k = main[:, width:2 * width]
    r_ref[...] = main[:, :width].astype(r_ref.dtype)
    k_ref[...] = k.astype(k_ref.dtype)
    v_ref[...] = main[:, 2 * width:].astype(v_ref.dtype)
    kk = k * kk_ref[...]
    kko_ref[...] = (kk * lax.rsqrt(jnp.maximum(_group_sums(kk * kk, RW_HEAD_DIM), 1e-24))).astype(kko_ref.dtype)

    half = small.shape[1] // 2
    lora = small[:, :half]
    lane = lax.broadcasted_iota(jnp.int32, lora.shape, 1)
    heads = _dot(jnp.where(lane < n_decay, jnp.tanh(lora), lora).astype(BF16), wl_ref[...]) + bias_ref[...]
    for d, (lw_ref, ar_ref) in enumerate(((lwf_ref, arf_ref), (lwb_ref, arb_ref))):
        w_pre = heads[:, d * width:(d + 1) * width]
        lw_ref[...] = -jnp.exp(-jax.nn.softplus(-w_pre) - 0.5)
        ar_ref[...] = jax.nn.sigmoid(heads[:, (2 + d) * width:(3 + d) * width])
    og_ref[...] = _dot(jax.nn.sigmoid(small[:, half:]).astype(BF16), g2_ref[...])


def rwkv_prep(p, lay, prm, n_ctx):
    bsz, t, _ = p.shape
    width = RW_HEADS * RW_HEAD_DIM
    m = bsz * t
    tm = _row_tile(t, 272)
    tps = t // tm
    p2 = p.reshape(m, -1)
    main_w = 3 * width
    small0 = lay["lora"][0]
    small_w = p2.shape[1] - small0
    half = small_w // 2
    dl, al, gl = prm["w2"].shape[1], prm["a2"].shape[1], prm["g2"].shape[0]
    assert lay["r"][0] == 0 and lay["g"][0] == small0 + half and small0 % small_w == 0
    mu = prm["shift_mu"]
    pad1 = lambda a, n: jnp.pad(a, (0, n - a.shape[0]))
    mu_main = jnp.concatenate([mu["r"], mu["k"], mu["v"]]).reshape(1, main_w)
    mu_small = jnp.concatenate([pad1(mu["lora"], half), pad1(mu["g"], half)]).reshape(1, small_w)
    wl = jnp.zeros((half, 4 * width), F32)
    for j, (blk, rows0, nrows) in enumerate(((prm["w2"][0], 0, dl), (prm["w2"][1], dl, dl),
                                             (prm["a2"][0], 2 * dl, al), (prm["a2"][1], 2 * dl + al, al))):
        wl = wl.at[rows0:rows0 + nrows, j * width:(j + 1) * width].set(blk)
    bias = jnp.concatenate([prm["w0"][0], prm["w0"][1], prm["a0"][0], prm["a0"][1]]).reshape(1, 4 * width)
    g2 = jnp.pad(prm["g2"], ((0, half - gl), (0, 0))).astype(BF16)

    mb, ma = _halo_specs(tm, m, main_w, 0)
    sb, sa = _halo_specs(tm, m, small_w, small0 // small_w)
    const = lambda a: pl.BlockSpec(a.shape, lambda i: (0,) * a.ndim)
    out_spec = pl.BlockSpec((tm, width), lambda i: (i, 0))
    outs = pl.pallas_call(
        functools.partial(_rwkv_prep_kernel, tm=tm, tiles_per_sample=tps, n_ctx=n_ctx, t_total=t, width=width,
                          n_decay=2 * dl),
        grid=(m // tm,),
        in_specs=[pl.BlockSpec((tm, main_w), lambda i: (i, 0)), mb, ma,
                  pl.BlockSpec((tm, small_w), lambda i: (i, small0 // small_w)), sb, sa,
                  const(mu_main), const(mu_small), pl.BlockSpec((1, width), lambda i: (0, 0)), const(bias),
                  pl.BlockSpec(wl.shape, lambda i: (0, 0)), const(g2)],
        out_specs=[out_spec] * 9,
        out_shape=[jax.ShapeDtypeStruct((m, width), ACT)] * 4 + [jax.ShapeDtypeStruct((m, width), F32)] * 5,
        compiler_params=pltpu.CompilerParams(dimension_semantics=("parallel",), vmem_limit_bytes=VMEM_LIMIT),
        name="rwkv_prep",
    )(p2, p2, p2, p2, p2, p2, mu_main, mu_small, prm["k_k"].reshape(1, width), bias, wl.astype(BF16), g2)
    return [o.reshape(bsz, t, width) for o in outs]


def _wkv_chunks(chains):
    c = chains[0][0].shape[0]
    n = 2 * c
    rev = [ch[8] for ch in chains]
    each = lambda f, *cols: [f(*a) for a in zip(*cols)]
    bf = lambda x: x.astype(BF16)
    cat0 = lambda *xs: jnp.concatenate([bf(x) for x in xs], axis=0)
    cat1 = lambda *xs: jnp.concatenate([bf(x) for x in xs], axis=1)

    rc = lax.broadcasted_iota(jnp.int32, (c, c), 0)
    cc = lax.broadcasted_iota(jnp.int32, (c, c), 1)
    seen = {False: (rc >= cc).astype(F32), True: (rc <= cc).astype(F32)}
    r2 = lax.broadcasted_iota(jnp.int32, (n, n), 0)
    c2 = lax.broadcasted_iota(jnp.int32, (n, n), 1)
    before = {False: r2 > c2, True: r2 < c2}
    upto = {False: r2 >= c2, True: r2 <= c2}
    eye = jnp.where(r2 == c2, 1.0, 0.0)
    first = lax.broadcasted_iota(jnp.int32, (c, LANES), 1) < RW_HEAD_DIM
    stack = lambda x: jnp.concatenate([jnp.where(first, x, 0.0), jnp.where(first, 0.0, x)], axis=0)
    ones = jnp.ones((c, LANES), F32)

    seen3 = {v_: jnp.concatenate([bf(m)] * 3, axis=1) for v_, m in seen.items()}
    cum = [_dot(seen3[ch[8]], jnp.concatenate(_bf16_pieces(ch[4], 3), axis=0)) for ch in chains]
    total = [_dot_tn(ch[4], ones, HIGHEST) for ch in chains]
    e_neg = each(lambda q: jnp.exp(-q), cum)
    e_end = [jnp.exp(q[0:1] if ch[8] else q[c - 1:c]) for q, ch in zip(cum, chains)]
    rs = [stack(ch[0] * jnp.exp(q)) for ch, q in zip(chains, cum)]
    as_ = [stack(-ch[3] * jnp.exp(q - ch[4])) for ch, q in zip(chains, cum)]
    bs = [stack(ch[3] * ch[5] * en) for ch, en in zip(chains, e_neg)]
    ks = [stack(ch[1] * (1.0 + (ch[5] - 1.0) * ch[6]) * en) for ch, en in zip(chains, e_neg)]
    vs = [stack(ch[2]) for ch in chains]
    h0 = [ch[7] for ch in chains]

    pair = each(lambda a, b, c_, d: _dot_nt(cat0(a, b), cat0(c_, d)), rs, as_, bs, ks)
    m_rb = [jnp.where(upto[v_], p[:n, :n], 0.0) for p, v_ in zip(pair, rev)]
    m_rk = [jnp.where(upto[v_], p[:n, n:], 0.0) for p, v_ in zip(pair, rev)]
    l_ab = [jnp.where(before[v_], p[n:, :n], 0.0) for p, v_ in zip(pair, rev)]
    m_ak = [jnp.where(before[v_], p[n:, n:], 0.0) for p, v_ in zip(pair, rev)]

    inv = each(lambda l: eye + l, l_ab)
    power = each(lambda l: _dot(bf(l), bf(l)), l_ab)
    steps = int(math.log2(c)) - 1
    for it in range(steps):
        if it < steps - 1:
            both = each(lambda i, p: _dot(cat0(i, p), bf(p)), inv, power)
            inv = each(lambda i, b: i + b[:n], inv, both)
            power = each(lambda b: b[n:], both)
        else:
            inv = each(lambda i, p: i + _dot(bf(i), bf(p)), inv, power)

    state_and_v = each(cat0, h0, vs)
    w = each(lambda a, m, sv: _dot(cat1(a, m), sv), as_, m_ak, state_and_v)
    y0 = each(lambda r_, m, sv: _dot(cat1(r_, m), sv), rs, m_rk, state_and_v)
    u = each(lambda i, w_: _dot(bf(i), bf(w_)), inv, w)
    y = each(lambda y_, m, u_: y_ + _dot(bf(m), bf(u_)), y0, m_rb, u)
    decayed = each(lambda b, k_, e: jnp.concatenate([b * e, k_ * e], axis=0).T, bs, ks, e_end)
    h_new = each(lambda h, t_, dc, u_, v_: h * jnp.exp(t_) + _dot(bf(dc), cat0(u_, v_)), h0, total, decayed, u, vs)
    return [q[:c] + q[c:] for q in y], h_new


def _wkv_kernel(rf_ref, kf_ref, vf_ref, kkf_ref, lwf_ref, arf_ref, rb_ref, kb_ref, vb_ref, kkb_ref, lwb_ref,
                arb_ref, ka_ref, yf_ref, yb_ref, h_ref, *, pairs):
    @pl.when(pl.program_id(2) == 0)
    def _():
        h_ref[...] = jnp.zeros_like(h_ref)

    dirs = ((rf_ref, kf_ref, vf_ref, kkf_ref, lwf_ref, arf_ref, yf_ref),
            (rb_ref, kb_ref, vb_ref, kkb_ref, lwb_ref, arb_ref, yb_ref))
    chains, outs = [], []
    for d, (r_ref, k_ref, v_ref, kk_ref, lw_ref, ar_ref, y_ref) in enumerate(dirs):
        for p in range(pairs):
            lanes = slice(p * LANES, (p + 1) * LANES)
            chains.append((r_ref[0, :, lanes].astype(F32), k_ref[0, :, lanes].astype(F32),
                           v_ref[0, :, lanes].astype(F32), kk_ref[0, :, lanes].astype(F32),
                           lw_ref[0, :, lanes], ar_ref[0, :, lanes], ka_ref[:, lanes], h_ref[d, p], d == 1))
            outs.append((y_ref, lanes, d, p))
    ys, hs = _wkv_chunks(chains)
    for (y_ref, lanes, d, p), y, h_new in zip(outs, ys, hs):
        y_ref[0, :, lanes] = y.astype(y_ref.dtype)
        h_ref[d, p] = h_new


def wkv7(r, k, v, kk, lw, ar, k_a, n_ctx):
    bsz, t, width = r.shape
    c = WKV_CHUNK
    n_cc, n_chunks = n_ctx // c, t // c
    pairs = WKV_PAIRS
    wb = pairs * LANES
    fwd = pl.BlockSpec((1, c, wb), lambda i, p, s: (i, s, p))
    bwd = pl.BlockSpec((1, c, wb), lambda i, p, s: (i, _mirrored_chunk(s, n_cc, n_chunks), p))
    return pl.pallas_call(
        functools.partial(_wkv_kernel, pairs=pairs),
        grid=(bsz, width // wb, n_chunks),
        in_specs=[fwd] * 6 + [bwd] * 6 + [pl.BlockSpec((1, wb), lambda i, p, s: (0, p))],
        out_specs=[fwd, bwd],
        out_shape=[jax.ShapeDtypeStruct((bsz, t, width), ACT)] * 2,
        scratch_shapes=[pltpu.VMEM((2, pairs, LANES, LANES), F32)],
        compiler_params=pltpu.CompilerParams(dimension_semantics=("parallel", "parallel", "arbitrary"),
                                             vmem_limit_bytes=VMEM_LIMIT),
        name="wkv7",
    )(r, k, v, kk, lw[0], ar[0], r, k, v, kk, lw[1], ar[1], k_a.reshape(1, width))


def _conv_kernel(x_ref, before_ref, after_ref, w_ref, b_ref, o_ref, *, tm, tiles_per_sample, n_ctx, t_total):
    x, before, after = (ref[...].astype(F32) for ref in (x_ref, before_ref, after_ref))
    half = SSM_CONV // 2
    acc = x * w_ref[half:half + 1] + b_ref[...]
    for o in range(SSM_CONV):
        if o != half:
            ok = _same_segment(tm, tiles_per_sample, n_ctx, t_total, o - half)
            acc = acc + jnp.where(ok, _row_from(x, before, after, o - half), 0.0) * w_ref[o:o + 1]
    o_ref[...] = jax.nn.silu(acc).astype(o_ref.dtype)


def conv_silu(p, col0, conv_w, conv_b, n_ctx):
    bsz, t, _ = p.shape
    taps, ch = conv_w.shape
    tm = _row_tile(t, 272)
    tps = t // tm
    m = bsz * t
    p2 = p.reshape(m, -1)
    before, after = _halo_specs(tm, m, ch, col0 // ch)
    w_pad = jnp.concatenate([conv_w, jnp.zeros((-taps % 8, ch), F32)], axis=0)
    const = lambda a: pl.BlockSpec(a.shape, lambda i: (0,) * a.ndim)
    out = pl.pallas_call(
        functools.partial(_conv_kernel, tm=tm, tiles_per_sample=tps, n_ctx=n_ctx, t_total=t),
        grid=(m // tm,),
        in_specs=[pl.BlockSpec((tm, ch), lambda i: (i, col0 // ch)), before, after, const(w_pad),
                  pl.BlockSpec((1, ch), lambda i: (0, 0))],
        out_specs=pl.BlockSpec((tm, ch), lambda i: (i, 0)),
        out_shape=jax.ShapeDtypeStruct((m, ch), ACT),
        compiler_params=pltpu.CompilerParams(dimension_semantics=("parallel",), vmem_limit_bytes=VMEM_LIMIT),
        name="conv_silu",
    )(p2, p2, p2, w_pad, conv_b.reshape(1, ch))
    return out.reshape(bsz, t, ch)


def _ssd_kernel(xf_ref, dtf_ref, xb_ref, dtb_ref, bias_ref, arow_ref, tabw_ref, tabx_ref, yf_ref, yb_ref,
                s_ref, ct_ref, cp_ref, *, inner, heads_per_group):
    L, hp = SSD_CHUNK, SSM_HEAD_DIM
    e_heads = heads_per_group
    width = e_heads * hp
    gn = SSM_GROUPS * SSM_STATE
    n_heads = SSM_GROUPS * e_heads

    @pl.when(pl.program_id(1) == 0)
    def _():
        s_ref[...] = jnp.zeros_like(s_ref)

    rl = lax.broadcasted_iota(jnp.int32, (L, L), 0)
    cl = lax.broadcasted_iota(jnp.int32, (L, L), 1)
    upto = (rl >= cl, rl <= cl)
    dirs = ((xf_ref, dtf_ref, yf_ref), (xb_ref, dtb_ref, yb_ref))
    for d, (_, dt_ref, _) in enumerate(dirs):
        dt_all = jax.nn.softplus(dt_ref[0].astype(F32) + bias_ref[...])
        cum = _dot(upto[d].astype(F32), dt_all * arow_ref[...], HIGHEST)
        ct_ref[d] = cum.T
        cp_ref[d] = jnp.concatenate(_bf16_pieces(cum, 3) + _bf16_pieces(dt_all, 2), axis=1)

    lane = lax.broadcasted_iota(jnp.int32, (L, LANES), 1)
    lane_head = lax.broadcasted_iota(jnp.int32, (L, width), 1) // hp

    def group(g, carry):
        two = range(2)
        xcol = pl.ds(pl.multiple_of(g * width, width), width)
        bcol = pl.ds(pl.multiple_of(inner + g * SSM_STATE, SSM_STATE), SSM_STATE)
        ccol = pl.ds(pl.multiple_of(inner + gn + g * SSM_STATE, SSM_STATE), SSM_STATE)
        x = [dirs[d][0][0, :, xcol].astype(F32) for d in two]
        bm = [dirs[d][0][0, :, bcol].astype(F32) for d in two]
        cmb = [dirs[d][0][0, :, ccol].astype(BF16) for d in two]
        tabw = [tabw_ref[d, g] for d in two]
        tabx = [tabx_ref[d, g] for d in two]
        cum_w = [_dot(cp_ref[d, :, :3 * LANES], tabw[d]) for d in two]
        dt_x = [_dot(cp_ref[d, :, 3 * LANES:], tabx[d]) for d in two]
        s0 = [s_ref[d, g] for d in two]
        cb = [_dot_nt(cmb[d], bm[d].astype(BF16)) for d in two]
        y_off = [_dot(cmb[d], s0[d].astype(BF16)) for d in two]
        bt = [bm[d].T.astype(BF16) for d in two]
        cum_x = [jnp.concatenate(
            [jnp.where(lane < hp, cum_w[d][:, (2 * q) * LANES:(2 * q + 1) * LANES],
                       cum_w[d][:, (2 * q + 1) * LANES:(2 * q + 2) * LANES]) for q in range(e_heads // 2)],
            axis=1) for d in two]
        tot_x = [cum_x[d][0:1] if d == 1 else cum_x[d][L - 1:L] for d in two]
        xdt = [x[d] * dt_x[d] for d in two]
        lmats, xmasked = [], []
        for d in two:
            for e in range(e_heads):
                row = d * n_heads + g * e_heads + e
                seg = cum_w[d][:, e * LANES:(e + 1) * LANES] - ct_ref[d, pl.ds(row, 1), :]
                lmats.append((cb[d] * jnp.where(upto[d], jnp.exp(jnp.minimum(seg, 0.0)), 0.0)).astype(BF16))
                xmasked.append(jnp.where(lane_head == e, xdt[d], 0.0).astype(BF16))
        y_diag = [_dot(m, xm) for m, xm in zip(lmats, xmasked)]
        s_add = [_dot(bt[d], (xdt[d] * jnp.exp(tot_x[d] - cum_x[d])).astype(BF16)) for d in two]
        for d in two:
            y = y_off[d] * jnp.exp(cum_x[d])
            for e in range(e_heads):
                y = y + y_diag[d * e_heads + e]
            dirs[d][2][0, :, xcol] = y.astype(dirs[d][2].dtype)
            s_ref[d, g] = s0[d] * jnp.exp(tot_x[d]) + s_add[d]
        return carry

    lax.fori_loop(0, SSM_GROUPS, group, 0, unroll=2)


def ssd(xbc, p_c, dt_col, dt_bias, a_log, n_ctx, inner):
    bsz, t, xw = xbc.shape
    L = SSD_CHUNK
    n_heads = dt_bias.shape[1]
    e_heads = n_heads // SSM_GROUPS
    width = e_heads * SSM_HEAD_DIM
    n_cc, n_chunks = n_ctx // L, t // L
    pad = LANES - 2 * n_heads
    bias = jnp.concatenate([dt_bias[0], dt_bias[1], jnp.zeros((pad,), F32)]).reshape(1, LANES)
    arow = jnp.concatenate([-jnp.exp(a_log[0]), -jnp.exp(a_log[1]), jnp.zeros((pad,), F32)]).reshape(1, LANES)
    head_row = (jnp.arange(2)[:, None, None, None] * n_heads + jnp.arange(SSM_GROUPS)[None, :, None, None] * e_heads)
    src = jnp.arange(LANES)[None, None, :, None]
    tabw = (src == head_row + jnp.arange(e_heads * LANES)[None, None, None, :] // LANES).astype(BF16)
    tabx = (src == head_row + jnp.arange(width)[None, None, None, :] // SSM_HEAD_DIM).astype(BF16)
    tabw = jnp.concatenate([tabw] * 3, axis=2)
    tabx = jnp.concatenate([tabx] * 2, axis=2)

    mirrored = lambda s: _mirrored_chunk(s, n_cc, n_chunks)
    const = lambda a: pl.BlockSpec(a.shape, lambda i, s: (0,) * a.ndim)
    return pl.pallas_call(
        functools.partial(_ssd_kernel, inner=inner, heads_per_group=e_heads),
        grid=(bsz, n_chunks),
        in_specs=[
            pl.BlockSpec((1, L, xw), lambda i, s: (i, s, 0)),
            pl.BlockSpec((1, L, LANES), lambda i, s: (i, s, dt_col // LANES)),
            pl.BlockSpec((1, L, xw), lambda i, s: (i, mirrored(s), 0)),
            pl.BlockSpec((1, L, LANES), lambda i, s: (i, mirrored(s), dt_col // LANES)),
            const(bias), const(arow), const(tabw), const(tabx),
        ],
        out_specs=[pl.BlockSpec((1, L, inner), lambda i, s: (i, s, 0)),
                   pl.BlockSpec((1, L, inner), lambda i, s: (i, mirrored(s), 0))],
        out_shape=[jax.ShapeDtypeStruct((bsz, t, inner), ACT)] * 2,
        scratch_shapes=[pltpu.VMEM((2, SSM_GROUPS, SSM_STATE, width), F32),
                        pltpu.VMEM((2, LANES, L), F32),
                        pltpu.VMEM((2, L, 5 * LANES), BF16)],
        compiler_params=pltpu.CompilerParams(dimension_semantics=("parallel", "arbitrary"),
                                             vmem_limit_bytes=VMEM_LIMIT),
        name="ssd",
    )(xbc, p_c, xbc, p_c, bias, arow, tabw, tabx)


def _merge_kernel(x_ref, yf_ref, yb_ref, r_ref, k_ref, v_ref, og_ref, z_ref, grw_ref, gssm_ref, gl_ref, gc_ref,
                  lnw_ref, lnb_ref, rk_ref, wrw_ref, wout_ref, o_ref, *, tm, tiles_per_sample, n_ctx):
    f32 = lambda ref: ref[...].astype(F32)
    y = f32(yf_ref) + f32(yb_ref)
    inv_n = 1.0 / RW_HEAD_DIM
    centred = y - _group_sums(y, RW_HEAD_DIM) * inv_n
    var = _group_sums(centred * centred, RW_HEAD_DIM) * inv_n
    y = centred * lax.rsqrt(var + RW_GN_EPS) * lnw_ref[...] + lnb_ref[...]
    y = y + _group_sums(f32(r_ref) * f32(k_ref) * rk_ref[...], RW_HEAD_DIM) * f32(v_ref)
    y_rw = y * og_ref[...]
    t1 = _dot(y_rw.astype(wrw_ref.dtype), wrw_ref[...])
    merged = jax.nn.sigmoid(f32(grw_ref)) * t1 + jax.nn.sigmoid(f32(gssm_ref)) * f32(z_ref)
    mix = _dot(merged.astype(wout_ref.dtype), wout_ref[...])
    gate = jnp.where(_ctx_rows(tm, tiles_per_sample, n_ctx), gc_ref[...], gl_ref[0])
    o_ref[...] = x_ref[...] + gate * mix


def merge(x, y_f, y_b, r, k, v, out_gate, z_ssm, proj, col_grw, col_gssm, gate_l, gate_c, ln_w, ln_b, r_k,
          w_rw, w_out, n_ctx):
    bsz, t, d = x.shape
    tm = _row_tile(t, 272)
    tps = t // tm
    m = bsz * t
    rows = lambda blk: pl.BlockSpec((tm, d), lambda i: (i, blk))
    const = lambda shape: pl.BlockSpec(shape, lambda i: (0,) * len(shape))
    flat = lambda a: a.reshape(m, -1)
    out = pl.pallas_call(
        functools.partial(_merge_kernel, tm=tm, tiles_per_sample=tps, n_ctx=n_ctx),
        grid=(m // tm,),
        in_specs=[rows(0)] * 8 + [rows(col_grw // d), rows(col_gssm // d),
                                  pl.BlockSpec((1, 1, d), lambda i: (i // tps, 0, 0)), const((1, d)),
                                  const((1, d)), const((1, d)), const((1, d)),
                                  const(w_rw.shape), const(w_out.shape)],
        out_specs=rows(0),
        out_shape=jax.ShapeDtypeStruct((m, d), F32),
        compiler_params=pltpu.CompilerParams(dimension_semantics=("parallel",), vmem_limit_bytes=VMEM_LIMIT),
        name="merge",
    )(flat(x), flat(y_f), flat(y_b), flat(r), flat(k), flat(v), flat(out_gate), flat(z_ssm), flat(proj), flat(proj),
      gate_l.reshape(bsz, 1, d), gate_c.reshape(1, d), ln_w.reshape(1, d), ln_b.reshape(1, d), r_k.reshape(1, d),
      w_rw, w_out)
    return out.reshape(bsz, t, d)


def _ssm_out_kernel(yf_ref, yb_ref, xs_ref, z_ref, dskip_ref, nw_ref, w_ref, o_ref, *, group):
    f32 = lambda ref: ref[...].astype(F32)
    y = (f32(yf_ref) + f32(yb_ref) + dskip_ref[...] * f32(xs_ref)) * jax.nn.silu(f32(z_ref))
    ms = _group_sums(y * y, group) * (1.0 / group)
    y = y * lax.rsqrt(ms + SSM_NORM_EPS) * nw_ref[...]
    o_ref[...] = _dot(y.astype(w_ref.dtype), w_ref[...]).astype(o_ref.dtype)


def ssm_out(y_f, y_b, xbc, p_c, z_col, d_skip, norm_w, w):
    bsz, t, inner = y_f.shape
    m = bsz * t
    d = w.shape[1]
    tm = _row_tile(t, 272)
    rows = lambda blk: pl.BlockSpec((tm, inner), lambda i: (i, blk))
    const = lambda shape: pl.BlockSpec(shape, lambda i: (0,) * len(shape))
    flat = lambda a: a.reshape(m, -1)
    out = pl.pallas_call(
        functools.partial(_ssm_out_kernel, group=inner // SSM_GROUPS),
        grid=(m // tm,),
        in_specs=[rows(0), rows(0), rows(0), rows(z_col // inner), const((1, inner)), const((1, inner)),
                  const(w.shape)],
        out_specs=pl.BlockSpec((tm, d), lambda i: (i, 0)),
        out_shape=jax.ShapeDtypeStruct((m, d), ACT),
        compiler_params=pltpu.CompilerParams(dimension_semantics=("parallel",), vmem_limit_bytes=VMEM_LIMIT),
        name="ssm_out",
    )(flat(y_f), flat(y_b), flat(xbc), flat(p_c), d_skip.reshape(1, inner), norm_w.reshape(1, inner), w)
    return out.reshape(bsz, t, d)


def _router_kernel(x_ref, g_ref, scl_ref, shl_ref, scc_ref, shc_ref, wr_ref, h_ref, logit_ref, *,
                   tm, tiles_per_sample, n_ctx):
    is_ctx = _ctx_rows(tm, tiles_per_sample, n_ctx)
    h = _modulated(x_ref[...], g_ref[...], scl_ref[0], shl_ref[0], scc_ref[...], shc_ref[...], is_ctx)
    bits = pltpu.bitcast(h.astype(BF16).astype(F32), jnp.uint32)
    half = h.shape[1] // 2
    h_ref[...] = (bits[:, :half] >> 16) | (bits[:, half:] & jnp.uint32(0xFFFF0000))
    logit_ref[...] = _dot(h, wr_ref[...], HIGHEST)


def router(x, gain, scale_l, shift_l, scale_c, shift_c, w_router_padded, n_ctx):
    bsz, t, d = x.shape
    tm = _row_tile(t, 544)
    tps = t // tm
    m = bsz * t
    row = lambda v: v.reshape(1, d)
    const = lambda shape: pl.BlockSpec(shape, lambda i: (0,) * len(shape))
    per_sample = pl.BlockSpec((1, 1, d), lambda i: (i // tps, 0, 0))
    return pl.pallas_call(
        functools.partial(_router_kernel, tm=tm, tiles_per_sample=tps, n_ctx=n_ctx),
        grid=(m // tm,),
        in_specs=[pl.BlockSpec((tm, d), lambda i: (i, 0)), const((1, d)), per_sample, per_sample,
                  const((1, d)), const((1, d)), const(w_router_padded.shape)],
        out_specs=[pl.BlockSpec((tm, d // 2), lambda i: (i, 0)), pl.BlockSpec((tm, LANES), lambda i: (i, 0))],
        out_shape=[jax.ShapeDtypeStruct((m, d // 2), jnp.uint32), jax.ShapeDtypeStruct((m, LANES), F32)],
        compiler_params=pltpu.CompilerParams(dimension_semantics=("parallel",), vmem_limit_bytes=VMEM_LIMIT),
        name="router",
    )(x.reshape(m, d), row(gain), scale_l.reshape(bsz, 1, d), shift_l.reshape(bsz, 1, d),
      row(scale_c), row(shift_c), w_router_padded)


def _experts_kernel(be_ref, nb_ref, x_ref, w1_ref, w3_ref, w2_ref, o_ref, w1b_ref, w3b_ref, w2b_ref):
    i = pl.program_id(0)

    @pl.when((i == 0) | (be_ref[i] != be_ref[jnp.maximum(i - 1, 0)]))
    def _():
        w1b_ref[...] = w1_ref[0, 0].astype(BF16)
        w3b_ref[...] = w3_ref[0, 0].astype(BF16)
        w2b_ref[...] = w2_ref[0, 0].astype(BF16)

    @pl.when(i < nb_ref[0])
    def _():
        packed = x_ref[...]
        x = jnp.concatenate([pltpu.bitcast(packed << 16, F32),
                             pltpu.bitcast(packed & jnp.uint32(0xFFFF0000), F32)], axis=1).astype(BF16)
        hidden = jax.nn.silu(_dot(x, w1b_ref[...])) * _dot(x, w3b_ref[...])
        o_ref[...] = _dot(hidden.astype(BF16), w2b_ref[...])

    @pl.when(i >= nb_ref[0])
    def _():
        o_ref[...] = jnp.zeros_like(o_ref)


def experts(xb, block_expert, n_used, w1, w3, w2, layer):
    n_rows = xb.shape[0]
    d, de = w1.shape[2:]
    rows = MOE_ROWS
    n_blocks = n_rows // rows
    return pl.pallas_call(
        _experts_kernel,
        grid_spec=pltpu.PrefetchScalarGridSpec(
            num_scalar_prefetch=2,
            grid=(n_blocks,),
            in_specs=[
                pl.BlockSpec((rows, d // 2), lambda i, be, nb: (i, 0)),
                pl.BlockSpec((1, 1, d, de), lambda i, be, nb: (layer, be[i], 0, 0)),
                pl.BlockSpec((1, 1, d, de), lambda i, be, nb: (layer, be[i], 0, 0)),
                pl.BlockSpec((1, 1, de, d), lambda i, be, nb: (layer, be[i], 0, 0)),
            ],
            out_specs=pl.BlockSpec((rows, d), lambda i, be, nb: (i, 0)),
            scratch_shapes=[pltpu.VMEM((d, de), BF16), pltpu.VMEM((d, de), BF16), pltpu.VMEM((de, d), BF16)],
        ),
        out_shape=jax.ShapeDtypeStruct((n_rows, d), F32),
        compiler_params=pltpu.CompilerParams(dimension_semantics=("arbitrary",), vmem_limit_bytes=VMEM_LIMIT),
        name="experts",
    )(block_expert, n_used, xb, w1, w3, w2)


def _combine_kernel(x_ref, y_ref, gate_ref, gl_ref, gc_ref, fin_ref, o_ref, *, tm, tiles_per_sample, n_ctx, final):
    d = x_ref.shape[1]
    g = gate_ref[...]
    f = y_ref[:, :d] * g[:, 0:1] + y_ref[:, d:] * g[:, 1:2]
    mod = jnp.where(_ctx_rows(tm, tiles_per_sample, n_ctx), gc_ref[...], gl_ref[0])
    out = x_ref[...] + mod * f
    if final:
        out = out * lax.rsqrt(jnp.mean(out * out, axis=-1, keepdims=True) + NORM_EPS) * fin_ref[...]
    o_ref[...] = out


def combine(x, y_pairs, gates, gate_l, gate_c, final_gain, n_ctx, final):
    bsz, t, d = x.shape
    m = bsz * t
    tm = _row_tile(t, 544)
    tps = t // tm
    const = lambda shape: pl.BlockSpec(shape, lambda i: (0,) * len(shape))
    out = pl.pallas_call(
        functools.partial(_combine_kernel, tm=tm, tiles_per_sample=tps, n_ctx=n_ctx, final=final),
        grid=(m // tm,),
        in_specs=[pl.BlockSpec((tm, d), lambda i: (i, 0)), pl.BlockSpec((tm, TOP_K * d), lambda i: (i, 0)),
                  pl.BlockSpec((tm, LANES), lambda i: (i, 0)), pl.BlockSpec((1, 1, d), lambda i: (i // tps, 0, 0)),
                  const((1, d)), const((1, d))],
        out_specs=pl.BlockSpec((tm, d), lambda i: (i, 0)),
        out_shape=jax.ShapeDtypeStruct((m, d), F32),
        compiler_params=pltpu.CompilerParams(dimension_semantics=("parallel",), vmem_limit_bytes=VMEM_LIMIT),
        name="combine",
    )(x.reshape(m, d), y_pairs.reshape(m, TOP_K * d), gates, gate_l.reshape(bsz, 1, d), gate_c.reshape(1, d),
      final_gain.reshape(1, d))
    return out.reshape(bsz, t, d)


def _top2(vals):
    idx = jnp.arange(vals.shape[-1], dtype=jnp.int32)
    i1 = jnp.argmax(vals, axis=-1).astype(jnp.int32)
    v1 = jnp.max(vals, axis=-1)
    rest = jnp.where(idx == i1[..., None], -jnp.inf, vals)
    i2 = jnp.argmax(rest, axis=-1).astype(jnp.int32)
    v2 = jnp.max(rest, axis=-1)
    return v1, i1, v2, i2


def moe(h, logits, b_router, w1, w3, w2, layer):
    n_tok, d = h.shape
    n_exp = b_router.shape[0]
    epg = n_exp // N_EXPERT_GROUPS
    scores = jax.nn.sigmoid(logits)
    biased = (scores + b_router.astype(F32)).reshape(n_tok, N_EXPERT_GROUPS, epg)
    g1, _, g2, _ = _top2(biased)
    top_group = jnp.argmax(g1 + g2, axis=-1).astype(jnp.int32)
    in_top = jnp.arange(N_EXPERT_GROUPS, dtype=jnp.int32)[None, :, None] == top_group[:, None, None]
    in_group = jnp.sum(jnp.where(in_top, biased, 0.0), axis=1)
    _, l1, _, l2 = _top2(in_group)
    expert = top_group[:, None] * epg + jnp.stack([l1, l2], axis=-1)
    picked = expert[:, :, None] == jnp.arange(n_exp, dtype=jnp.int32)[None, None, :]
    gate = jnp.sum(jnp.where(picked, scores[:, None, :], 0.0), axis=-1)
    gate = gate / jnp.sum(gate, axis=-1, keepdims=True)

    n_assign = n_tok * TOP_K
    flat_e = expert.reshape(-1).astype(jnp.int32)
    onehot = picked.reshape(n_assign, n_exp).astype(jnp.int32)
    rank = jnp.sum((jnp.cumsum(onehot, axis=0) - onehot) * onehot, axis=-1)
    counts = jnp.sum(onehot, axis=0)
    padded = (counts + MOE_ROWS - 1) // MOE_ROWS * MOE_ROWS
    pad_end = jnp.cumsum(padded)
    dest = ((pad_end - padded)[flat_e] + rank).astype(jnp.int32)
    n_blocks = -(-n_assign // MOE_ROWS) + n_exp
    slot_token = jnp.zeros((n_blocks * MOE_ROWS,), jnp.int32).at[dest].set(
        jnp.arange(n_assign, dtype=jnp.int32) // TOP_K)
    block_expert = jnp.clip(
        jnp.sum(jnp.arange(n_blocks, dtype=jnp.int32)[:, None] >= (pad_end // MOE_ROWS)[None, :], axis=-1),
        0, n_exp - 1).astype(jnp.int32)
    n_used = (pad_end[-1:] // MOE_ROWS).astype(jnp.int32)
    yb = experts(h[slot_token], block_expert, n_used, w1, w3, w2, layer)
    return yb[dest], jnp.pad(gate, ((0, 0), (0, LANES - TOP_K)))


def _ssm_branch(p, prm, w_branch, n_ctx):
    inner = prm["norm_w"].shape[0]
    xbc_w = prm["conv_w"].shape[1]
    xbc = conv_silu(p, 0, prm["conv_w"], prm["conv_b"], n_ctx)
    y_f, y_b = ssd(xbc, p, xbc_w + inner, prm["dt_bias"], prm["a_log"], n_ctx, inner)
    d_skip = jnp.repeat(prm["d"][0] + prm["d"][1], SSM_HEAD_DIM)
    return ssm_out(y_f, y_b, xbc, p, xbc_w, d_skip, prm["norm_w"], w_branch)


def _pad_cols(w, n):
    return jnp.pad(w, ((0, 0), (0, n - w.shape[1])))


def kernel(x, c, ctx, c_ctx, w_mod, b_mod, norm_mix_g, w_in, rw_shift_mu, rw_w0, rw_w2, rw_a0, rw_a2, rw_g2,
           rw_k_k, rw_k_a, rw_r_k, rw_ln_w, rw_ln_b, ssm_conv_w, ssm_conv_b, ssm_dt_bias, ssm_a_log, ssm_d,
           ssm_norm_w, w_branch_rw, w_branch_ssm, w_out, norm_ffn_g, w_router, b_router, exp_w1, exp_w3, exp_w2,
           norm_final_g):
    bsz, n_lat, d = x.shape
    depth = w_in.shape[0]
    n_ctx = ctx.shape[1]
    rows = n_lat // GRID_W
    t = n_ctx + n_lat
    width = RW_HEADS * RW_HEAD_DIM
    dl, al, gl = rw_w2.shape[2], rw_a2.shape[2], rw_g2.shape[1]
    inner = ssm_norm_w.shape[1]
    xbc_w = ssm_conv_w.shape[2]
    n_heads = ssm_dt_bias.shape[2]
    n_exp = w_router.shape[1]

    def to_c(a):
        ch = a.shape[-1]
        lat = a[:, n_ctx:].reshape(bsz, rows, GRID_W, ch).transpose(0, 2, 1, 3).reshape(bsz, n_lat, ch)
        return jnp.concatenate([a[:, :n_ctx], lat], axis=1)

    def to_r(a):
        ch = a.shape[-1]
        lat = a[:, n_ctx:].reshape(bsz, GRID_W, rows, ch).transpose(0, 2, 1, 3).reshape(bsz, n_lat, ch)
        return jnp.concatenate([a[:, :n_ctx], lat], axis=1)

    o = 0
    src = {}
    for name, size in (("r", width), ("w_f", dl), ("w_b", dl), ("k", width), ("v", width), ("a_f", al),
                       ("a_b", al), ("g", gl), ("z", inner), ("xbc", xbc_w), ("dt", 2 * n_heads),
                       ("g_rw", d), ("g_ssm", d)):
        src[name] = (o, size)
        o += size
    take = lambda a, name: a[..., src[name][0]:src[name][0] + src[name][1]]
    lora_w = 2 * dl + 2 * al
    lora_pad = -(-lora_w // LANES) * LANES
    g_pad = -(-gl // LANES) * LANES
    dt_pad = -(-2 * n_heads // LANES) * LANES
    lay = {"r": (0, width), "k": (width, width), "v": (2 * width, width), "g_rw": (3 * width, d),
           "g_ssm": (3 * width + d, d), "lora": (3 * width + 2 * d, lora_w),
           "g": (3 * width + 2 * d + lora_pad, gl)}

    silu_c = jax.nn.silu(c)
    silu_cc = jax.nn.silu(c_ctx)[None, :]
    act = jnp.concatenate([silu_c, silu_cc, jnp.zeros((-(bsz + 1) % 8, d), F32)], axis=0)
    w_router_p = _pad_cols(w_router.astype(F32), LANES)

    xr = jnp.concatenate([ctx, x], axis=1)
    for l in range(depth):
        mod = matmul(act, w_mod[l], precision=HIGHEST, name="modulation")[:bsz + 1] + b_mod[l]
        shift_m, scale_m, gate_m, shift_f, scale_f, gate_f = jnp.split(mod[:bsz], 6, axis=-1)
        cshift_m, cscale_m, cgate_m, cshift_f, cscale_f, cgate_f = jnp.split(mod[bsz], 6, axis=-1)

        wl = w_in[l]
        w_r = jnp.concatenate(
            [take(wl, "r"), take(wl, "k"), take(wl, "v"), take(wl, "g_rw"), take(wl, "g_ssm"),
             _pad_cols(jnp.concatenate([take(wl, n) for n in ("w_f", "w_b", "a_f", "a_b")], axis=1), lora_pad),
             _pad_cols(take(wl, "g"), g_pad)], axis=1).astype(BF16)
        w_c = jnp.concatenate([take(wl, "xbc"), take(wl, "z"), _pad_cols(take(wl, "dt"), dt_pad)],
                              axis=1).astype(BF16)
        mods = (norm_mix_g[l], scale_m, shift_m, cscale_m, cshift_m)
        p_r = norm_proj(xr, *mods, w_r, n_ctx, out_dtype=ACT)
        p_c = norm_proj(to_c(xr), *mods, w_c, n_ctx, out_dtype=ACT)

        mu = rw_shift_mu[l]
        rw_prm = dict(
            shift_mu={"r": take(mu, "r"), "k": take(mu, "k"), "v": take(mu, "v"), "g": take(mu, "g"),
                      "lora": jnp.concatenate([take(mu, n) for n in ("w_f", "w_b", "a_f", "a_b")])},
            w0=rw_w0[l], w2=rw_w2[l], a0=rw_a0[l], a2=rw_a2[l], g2=rw_g2[l],
            k_k=rw_k_k[l], k_a=rw_k_a[l], r_k=rw_r_k[l], ln_w=rw_ln_w[l], ln_b=rw_ln_b[l])
        r, k, v, kk, lw_f, lw_b, ar_f, ar_b, out_gate = rwkv_prep(p_r, lay, rw_prm, n_ctx)
        y_f, y_b = wkv7(r, k, v, kk, (lw_f, lw_b), (ar_f, ar_b), rw_k_a[l], n_ctx)
        ssm_prm = dict(conv_w=ssm_conv_w[l], conv_b=ssm_conv_b[l], dt_bias=ssm_dt_bias[l], a_log=ssm_a_log[l],
                       d=ssm_d[l], norm_w=ssm_norm_w[l])
        z_ssm = to_r(_ssm_branch(p_c, ssm_prm, w_branch_ssm[l].astype(BF16), n_ctx))
        xr = merge(xr, y_f, y_b, r, k, v, out_gate, z_ssm, p_r, lay["g_rw"][0], lay["g_ssm"][0], gate_m, cgate_m,
                   rw_ln_w[l], rw_ln_b[l], rw_r_k[l], w_branch_rw[l].astype(BF16), w_out[l].astype(BF16), n_ctx)

        h, logits = router(xr, norm_ffn_g[l], scale_f, shift_f, cscale_f, cshift_f, w_router_p, n_ctx)
        y_pairs, gates = moe(h, logits[:, :n_exp], b_router, exp_w1, exp_w3, exp_w2, l)
        xr = combine(xr, y_pairs, gates, gate_f, cgate_f, norm_final_g, n_ctx, final=(l == depth - 1))
    return xr[:, n_ctx:]
```

```python
import functools
import math

import jax
import jax.numpy as jnp
from jax import lax
from jax.experimental import pallas as pl
from jax.experimental.pallas import tpu as pltpu

F32 = jnp.float32
BF16 = jnp.bfloat16
HIGHEST = lax.Precision.HIGHEST

GRID_W = 64
RW_HEADS = 16
RW_HEAD_DIM = 64
RW_GN_EPS = 64e-5
SSM_HEAD_DIM = 64
SSM_GROUPS = 8
SSM_STATE = 128
SSM_CONV = 5
SSM_NORM_EPS = 1e-5
N_EXPERT_GROUPS = 4
TOP_K = 2
NORM_EPS = 1e-6

LANES = 128
WKV_CHUNK = 64
WKV_PAIRS = 8
SSD_CHUNK = 128
MOE_ROWS = 256
VMEM_LIMIT = 56 * 1024 * 1024


def _dot(a, b, precision=None):
    return jnp.dot(a, b, preferred_element_type=F32, precision=precision)


def _dot_nt(a, b, precision=None):
    return lax.dot_general(a, b, (((1,), (1,)), ((), ())), preferred_element_type=F32, precision=precision)


def _dot_tn(a, b, precision=None):
    return lax.dot_general(a, b, (((0,), (0,)), ((), ())), preferred_element_type=F32, precision=precision)


def _row_tile(n_rows_per_sample, limit=1088):
    for tm in (1088, 544, 512, 272, 256, 128, 64, 32, 16):
        if tm <= limit and n_rows_per_sample % tm == 0:
            return tm
    raise ValueError(n_rows_per_sample)


def _col_tile(n_cols, limit=1536):
    best = LANES
    for k in range(1, n_cols // LANES + 1):
        tn = k * LANES
        if n_cols % tn == 0 and tn <= limit:
            best = tn
    return best


def _bf16_pieces(x, n):
    pieces = []
    for _ in range(n):
        p = x.astype(BF16)
        pieces.append(p)
        x = x - p.astype(F32)
    return pieces


def _group_sums(x, group):
    span = max(group, LANES)
    rr = lax.broadcasted_iota(jnp.int32, (span, span), 0) // group
    cc = lax.broadcasted_iota(jnp.int32, (span, span), 1) // group
    ones = (rr == cc).astype(BF16)
    pieces = _bf16_pieces(x, 3)
    cols = []
    for j in range(x.shape[1] // span):
        sl = slice(j * span, (j + 1) * span)
        cols.append(_dot(pieces[0][:, sl], ones) + _dot(pieces[1][:, sl], ones) + _dot(pieces[2][:, sl], ones))
    return cols[0] if len(cols) == 1 else jnp.concatenate(cols, axis=1)


HALO = 16
ACT = BF16


def _seg_tile(n_ctx, n_lat, limit):
    for tm in (1024, 512, 256, 128, 64, 32, 16):
        if tm <= limit and n_ctx % tm == 0 and n_lat % tm == 0:
            return tm
    raise ValueError((n_ctx, n_lat))


def _segment_taps(x_ref, before_ref, after_ref, offsets, tiles_per_sample, ctx_tiles):
    tm = x_ref.shape[0]
    ti = pl.program_id(0) % tiles_per_sample
    first = (ti == 0) | (ti == ctx_tiles)
    last = (ti == ctx_tiles - 1) | (ti == tiles_per_sample - 1)
    x = x_ref[...].astype(F32)
    before = before_ref[...].astype(F32) * jnp.where(first, 0.0, 1.0)
    after = after_ref[...].astype(F32) * jnp.where(last, 0.0, 1.0)
    ext = jnp.concatenate([before, x, after], axis=0)
    n = tm + 2 * HALO
    return x, [pltpu.roll(ext, (-o) % n, 0)[HALO:HALO + tm] for o in offsets]


def _halo_specs(tm, n_rows, width, col_block):
    last = n_rows // HALO - 1
    return (pl.BlockSpec((HALO, width), lambda i: (jnp.maximum(i * (tm // HALO) - 1, 0), col_block)),
            pl.BlockSpec((HALO, width), lambda i: (jnp.minimum((i + 1) * (tm // HALO), last), col_block)))


def _mirrored_chunk(s, n_ctx_chunks, n_chunks):
    return jnp.where(s < n_ctx_chunks, n_ctx_chunks - 1 - s, n_chunks - 1 + n_ctx_chunks - s)


def _modulated(x, g, sc_l, sh_l, sc_c, sh_c, is_ctx):
    y = x * lax.rsqrt(jnp.mean(x * x, axis=-1, keepdims=True) + NORM_EPS) * g
    return y * (1.0 + jnp.where(is_ctx, sc_c, sc_l)) + jnp.where(is_ctx, sh_c, sh_l)


def _ctx_rows(tm, tiles_per_sample, n_ctx):
    row0 = (pl.program_id(0) % tiles_per_sample) * tm
    return row0 + lax.broadcasted_iota(jnp.int32, (tm, 1), 0) < n_ctx


def _mod_norm_kernel(x_ref, g_ref, scl_ref, shl_ref, scc_ref, shc_ref, h_ref, *, tm, tiles_per_sample, n_ctx):
    is_ctx = _ctx_rows(tm, tiles_per_sample, n_ctx)
    h = _modulated(x_ref[...], g_ref[...], scl_ref[0], shl_ref[0], scc_ref[...], shc_ref[...], is_ctx)
    h_ref[...] = h.astype(h_ref.dtype)


def mod_norm(x, gain, scale_l, shift_l, scale_c, shift_c, n_ctx):
    bsz, t, d = x.shape
    tm = _row_tile(t)
    tps = t // tm
    row = lambda v: v.reshape(1, d)
    const = pl.BlockSpec((1, d), lambda i: (0, 0))
    per_sample = pl.BlockSpec((1, 1, d), lambda i: (i // tps, 0, 0))
    out = pl.pallas_call(
        functools.partial(_mod_norm_kernel, tm=tm, tiles_per_sample=tps, n_ctx=n_ctx),
        grid=(bsz * tps,),
        in_specs=[pl.BlockSpec((tm, d), lambda i: (i, 0)), const, per_sample, per_sample, const, const],
        out_specs=pl.BlockSpec((tm, d), lambda i: (i, 0)),
        out_shape=jax.ShapeDtypeStruct((bsz * t, d), BF16),
        compiler_params=pltpu.CompilerParams(dimension_semantics=("parallel",), vmem_limit_bytes=VMEM_LIMIT),
        name="mod_norm",
    )(x.reshape(bsz * t, d), row(gain), scale_l.reshape(bsz, 1, d), shift_l.reshape(bsz, 1, d),
      row(scale_c), row(shift_c))
    return out.reshape(bsz, t, d)


def _project_kernel(h_ref, w_ref, o_ref):
    o_ref[...] = _dot(h_ref[...], w_ref[...]).astype(o_ref.dtype)


def project(h, w, out_dtype):
    bsz, t, d = h.shape
    n = w.shape[1]
    m = bsz * t
    tm = _row_tile(t)
    tn = _col_tile(n)
    out = pl.pallas_call(
        _project_kernel,
        grid=(n // tn, m // tm),
        in_specs=[pl.BlockSpec((tm, d), lambda j, i: (i, 0)), pl.BlockSpec((d, tn), lambda j, i: (0, j))],
        out_specs=pl.BlockSpec((tm, tn), lambda j, i: (i, j)),
        out_shape=jax.ShapeDtypeStruct((m, n), out_dtype),
        compiler_params=pltpu.CompilerParams(dimension_semantics=("parallel", "parallel"),
                                             vmem_limit_bytes=VMEM_LIMIT),
        name="project",
    )(h.reshape(m, d), w)
    return out.reshape(bsz, t, n)


def _matmul_kernel(x_ref, w_ref, o_ref, *, act, precision):
    x = x_ref[...]
    if act == "tanh":
        x = jnp.tanh(x)
    elif act == "sigmoid":
        x = jax.nn.sigmoid(x)
    o_ref[...] = _dot(x.astype(w_ref.dtype), w_ref[...], precision).astype(o_ref.dtype)


def matmul(x, w, act=None, precision=None, out_dtype=F32, name="matmul"):
    m, k = x.shape
    n = w.shape[1]
    tm = _row_tile(m) if m >= 16 else m
    tn = _col_tile(n)
    return pl.pallas_call(
        functools.partial(_matmul_kernel, act=act, precision=precision),
        grid=(m // tm, n // tn),
        in_specs=[pl.BlockSpec((tm, k), lambda i, j: (i, 0)), pl.BlockSpec((k, tn), lambda i, j: (0, j))],
        out_specs=pl.BlockSpec((tm, tn), lambda i, j: (i, j)),
        out_shape=jax.ShapeDtypeStruct((m, n), out_dtype),
        compiler_params=pltpu.CompilerParams(dimension_semantics=("parallel", "parallel"),
                                             vmem_limit_bytes=VMEM_LIMIT),
        name=name,
    )(x, w)


def _rwkv_prep_kernel(m_ref, mb_ref, ma_ref, s_ref, sb_ref, sa_ref, mu_m_ref, mu_s_ref, kk_ref, bias_ref,
                      wl_ref, g2_ref, r_ref, k_ref, v_ref, kko_ref, lwf_ref, lwb_ref, arf_ref, arb_ref, og_ref, *,
                      tiles_per_sample, ctx_tiles, width, n_decay):
    def shifted(x_ref, before_ref, after_ref, mu_ref):
        x, (prev, nxt) = _segment_taps(x_ref, before_ref, after_ref, (-1, 1), tiles_per_sample, ctx_tiles)
        return x + mu_ref[...] * (0.5 * (prev + nxt) - x)

    main = shifted(m_ref, mb_ref, ma_ref, mu_m_ref)
    small = shifted(s_ref, sb_ref, sa_ref, mu_s_ref)
    k = main[:, width:2 * width]
    r_ref[...] = main[:, :width].astype(r_ref.dtype)
    k_ref[...] = k.astype(k_ref.dtype)
    v_ref[...] = main[:, 2 * width:].astype(v_ref.dtype)
    kk = k * kk_ref[...]
    kko_ref[...] = (kk * lax.rsqrt(jnp.maximum(_group_sums(kk * kk, RW_HEAD_DIM), 1e-24))).astype(kko_ref.dtype)

    half = small.shape[1] // 2
    lora = small[:, :half]
    lane = lax.broadcasted_iota(jnp.int32, lora.shape, 1)
    heads = _dot(jnp.where(lane < n_decay, jnp.tanh(lora), lora).astype(BF16), wl_ref[...]) + bias_ref[...]
    for d, (lw_ref, ar_ref) in enumerate(((lwf_ref, arf_ref), (lwb_ref, arb_ref))):
        w_pre = heads[:, d * width:(d + 1) * width]
        lw_ref[...] = -jnp.exp(-jax.nn.softplus(-w_pre) - 0.5)
        ar_ref[...] = jax.nn.sigmoid(heads[:, (2 + d) * width:(3 + d) * width])
    og_ref[...] = _dot(jax.nn.sigmoid(small[:, half:]).astype(BF16), g2_ref[...])


def rwkv_prep(p, lay, prm, n_ctx):
    bsz, t, _ = p.shape
    width = RW_HEADS * RW_HEAD_DIM
    m = bsz * t
    tm = _seg_tile(n_ctx, t - n_ctx, 256)
    tps = t // tm
    p2 = p.reshape(m, -1)
    main_w = 3 * width
    small0 = lay["lora"][0]
    small_w = p2.shape[1] - small0
    half = small_w // 2
    dl, al, gl = prm["w2"].shape[1], prm["a2"].shape[1], prm["g2"].shape[0]
    assert lay["r"][0] == 0 and lay["g"][0] == small0 + half and small0 % small_w == 0
    mu = prm["shift_mu"]
    pad1 = lambda a, n: jnp.pad(a, (0, n - a.shape[0]))
    mu_main = jnp.concatenate([mu["r"], mu["k"], mu["v"]]).reshape(1, main_w)
    mu_small = jnp.concatenate([pad1(mu["lora"], half), pad1(mu["g"], half)]).reshape(1, small_w)
    wl = jnp.zeros((half, 4 * width), F32)
    for j, (blk, rows0, nrows) in enumerate(((prm["w2"][0], 0, dl), (prm["w2"][1], dl, dl),
                                             (prm["a2"][0], 2 * dl, al), (prm["a2"][1], 2 * dl + al, al))):
        wl = wl.at[rows0:rows0 + nrows, j * width:(j + 1) * width].set(blk)
    bias = jnp.concatenate([prm["w0"][0], prm["w0"][1], prm["a0"][0], prm["a0"][1]]).reshape(1, 4 * width)
    g2 = jnp.pad(prm["g2"], ((0, half - gl), (0, 0))).astype(BF16)

    mb, ma = _halo_specs(tm, m, main_w, 0)
    sb, sa = _halo_specs(tm, m, small_w, small0 // small_w)
    const = lambda a: pl.BlockSpec(a.shape, lambda i: (0,) * a.ndim)
    out_spec = pl.BlockSpec((tm, width), lambda i: (i, 0))
    outs = pl.pallas_call(
        functools.partial(_rwkv_prep_kernel, tiles_per_sample=tps, ctx_tiles=n_ctx // tm, width=width,
                          n_decay=2 * dl),
        grid=(m // tm,),
        in_specs=[pl.BlockSpec((tm, main_w), lambda i: (i, 0)), mb, ma,
                  pl.BlockSpec((tm, small_w), lambda i: (i, small0 // small_w)), sb, sa,
                  const(mu_main), const(mu_small), pl.BlockSpec((1, width), lambda i: (0, 0)), const(bias),
                  pl.BlockSpec(wl.shape, lambda i: (0, 0)), const(g2)],
        out_specs=[out_spec] * 9,
        out_shape=[jax.ShapeDtypeStruct((m, width), ACT)] * 4 + [jax.ShapeDtypeStruct((m, width), F32)] * 5,
        compiler_params=pltpu.CompilerParams(dimension_semantics=("parallel",), vmem_limit_bytes=VMEM_LIMIT),
        name="rwkv_prep",
    )(p2, p2, p2, p2, p2, p2, mu_main, mu_small, prm["k_k"].reshape(1, width), bias, wl.astype(BF16), g2)
    return [o.reshape(bsz, t, width) for o in outs]


def _wkv_chunks(chains):
    c = chains[0][0].shape[0]
    n = 2 * c
    rev = [ch[8] for ch in chains]
    each = lambda f, *cols: [f(*a) for a in zip(*cols)]
    bf = lambda x: x.astype(BF16)
    cat0 = lambda *xs: jnp.concatenate([bf(x) for x in xs], axis=0)
    cat1 = lambda *xs: jnp.concatenate([bf(x) for x in xs], axis=1)

    rc = lax.broadcasted_iota(jnp.int32, (c, c), 0)
    cc = lax.broadcasted_iota(jnp.int32, (c, c), 1)
    seen = {False: (rc >= cc).astype(F32), True: (rc <= cc).astype(F32)}
    r2 = lax.broadcasted_iota(jnp.int32, (n, n), 0)
    c2 = lax.broadcasted_iota(jnp.int32, (n, n), 1)
    before = {False: r2 > c2, True: r2 < c2}
    upto = {False: r2 >= c2, True: r2 <= c2}
    eye = jnp.where(r2 == c2, 1.0, 0.0)
    first = lax.broadcasted_iota(jnp.int32, (c, LANES), 1) < RW_HEAD_DIM
    stack = lambda x: jnp.concatenate([jnp.where(first, x, 0.0), jnp.where(first, 0.0, x)], axis=0)
    ones = jnp.ones((c, LANES), F32)

    seen3 = {v_: jnp.concatenate([bf(m)] * 3, axis=1) for v_, m in seen.items()}
    cum = [_dot(seen3[ch[8]], jnp.concatenate(_bf16_pieces(ch[4], 3), axis=0)) for ch in chains]
    total = [_dot_tn(ch[4], ones, HIGHEST) for ch in chains]
    e_neg = each(lambda q: jnp.exp(-q), cum)
    e_end = [jnp.exp(q[0:1] if ch[8] else q[c - 1:c]) for q, ch in zip(cum, chains)]
    rs = [stack(ch[0] * jnp.exp(q)) for ch, q in zip(chains, cum)]
    as_ = [stack(-ch[3] * jnp.exp(q - ch[4])) for ch, q in zip(chains, cum)]
    bs = [stack(ch[3] * ch[5] * en) for ch, en in zip(chains, e_neg)]
    ks = [stack(ch[1] * (1.0 + (ch[5] - 1.0) * ch[6]) * en) for ch, en in zip(chains, e_neg)]
    vs = [stack(ch[2]) for ch in chains]
    h0 = [ch[7] for ch in chains]

    pair = each(lambda a, b, c_, d: _dot_nt(cat0(a, b), cat0(c_, d)), rs, as_, bs, ks)
    m_rb = [jnp.where(upto[v_], p[:n, :n], 0.0) for p, v_ in zip(pair, rev)]
    m_rk = [jnp.where(upto[v_], p[:n, n:], 0.0) for p, v_ in zip(pair, rev)]
    l_ab = [jnp.where(before[v_], p[n:, :n], 0.0) for p, v_ in zip(pair, rev)]
    m_ak = [jnp.where(before[v_], p[n:, n:], 0.0) for p, v_ in zip(pair, rev)]

    inv = each(lambda l: eye + l, l_ab)
    power = each(lambda l: _dot(bf(l), bf(l)), l_ab)
    steps = int(math.log2(c)) - 1
    for it in range(steps):
        if it < steps - 1:
            both = each(lambda i, p: _dot(cat0(i, p), bf(p)), inv, power)
            inv = each(lambda i, b: i + b[:n], inv, both)
            power = each(lambda b: b[n:], both)
        else:
            inv = each(lambda i, p: i + _dot(bf(i), bf(p)), inv, power)

    state_and_v = each(cat0, h0, vs)
    w = each(lambda a, m, sv: _dot(cat1(a, m), sv), as_, m_ak, state_and_v)
    y0 = each(lambda r_, m, sv: _dot(cat1(r_, m), sv), rs, m_rk, state_and_v)
    u = each(lambda i, w_: _dot(bf(i), bf(w_)), inv, w)
    y = each(lambda y_, m, u_: y_ + _dot(bf(m), bf(u_)), y0, m_rb, u)
    decayed = each(lambda b, k_, e: jnp.concatenate([b * e, k_ * e], axis=0).T, bs, ks, e_end)
    h_new = each(lambda h, t_, dc, u_, v_: h * jnp.exp(t_) + _dot(bf(dc), cat0(u_, v_)), h0, total, decayed, u, vs)
    return [q[:c] + q[c:] for q in y], h_new


def _wkv_kernel(rf_ref, kf_ref, vf_ref, kkf_ref, lwf_ref, arf_ref, rb_ref, kb_ref, vb_ref, kkb_ref, lwb_ref,
                arb_ref, ka_ref, yf_ref, yb_ref, h_ref, *, pairs):
    @pl.when(pl.program_id(2) == 0)
    def _():
        h_ref[...] = jnp.zeros_like(h_ref)

    dirs = ((rf_ref, kf_ref, vf_ref, kkf_ref, lwf_ref, arf_ref, yf_ref),
            (rb_ref, kb_ref, vb_ref, kkb_ref, lwb_ref, arb_ref, yb_ref))
    chains, outs = [], []
    for d, (r_ref, k_ref, v_ref, kk_ref, lw_ref, ar_ref, y_ref) in enumerate(dirs):
        for p in range(pairs):
            lanes = slice(p * LANES, (p + 1) * LANES)
            chains.append((r_ref[0, :, lanes].astype(F32), k_ref[0, :, lanes].astype(F32),
                           v_ref[0, :, lanes].astype(F32), kk_ref[0, :, lanes].astype(F32),
                           lw_ref[0, :, lanes], ar_ref[0, :, lanes], ka_ref[:, lanes], h_ref[d, p], d == 1))
            outs.append((y_ref, lanes, d, p))
    ys, hs = _wkv_chunks(chains)
    for (y_ref, lanes, d, p), y, h_new in zip(outs, ys, hs):
        y_ref[0, :, lanes] = y.astype(y_ref.dtype)
        h_ref[d, p] = h_new


def wkv7(r, k, v, kk, lw, ar, k_a, n_ctx):
    bsz, t, width = r.shape
    c = WKV_CHUNK
    n_cc, n_chunks = n_ctx // c, t // c
    pairs = WKV_PAIRS
    wb = pairs * LANES
    fwd = pl.BlockSpec((1, c, wb), lambda i, p, s: (i, s, p))
    bwd = pl.BlockSpec((1, c, wb), lambda i, p, s: (i, _mirrored_chunk(s, n_cc, n_chunks), p))
    return pl.pallas_call(
        functools.partial(_wkv_kernel, pairs=pairs),
        grid=(bsz, width // wb, n_chunks),
        in_specs=[fwd] * 6 + [bwd] * 6 + [pl.BlockSpec((1, wb), lambda i, p, s: (0, p))],
        out_specs=[fwd, bwd],
        out_shape=[jax.ShapeDtypeStruct((bsz, t, width), ACT)] * 2,
        scratch_shapes=[pltpu.VMEM((2, pairs, LANES, LANES), F32)],
        compiler_params=pltpu.CompilerParams(dimension_semantics=("parallel", "parallel", "arbitrary"),
                                             vmem_limit_bytes=VMEM_LIMIT),
        name="wkv7",
    )(r, k, v, kk, lw[0], ar[0], r, k, v, kk, lw[1], ar[1], k_a.reshape(1, width))


def _conv_kernel(x_ref, before_ref, after_ref, w_ref, b_ref, o_ref, *, tiles_per_sample, ctx_tiles):
    half = SSM_CONV // 2
    others = [o for o in range(SSM_CONV) if o != half]
    x, taps = _segment_taps(x_ref, before_ref, after_ref, [o - half for o in others], tiles_per_sample, ctx_tiles)
    acc = x * w_ref[half:half + 1] + b_ref[...]
    for o, tap in zip(others, taps):
        acc = acc + tap * w_ref[o:o + 1]
    o_ref[...] = jax.nn.silu(acc).astype(o_ref.dtype)


def conv_silu(p, col0, conv_w, conv_b, n_ctx):
    bsz, t, _ = p.shape
    taps, ch = conv_w.shape
    tm = _seg_tile(n_ctx, t - n_ctx, 256)
    tps = t // tm
    m = bsz * t
    p2 = p.reshape(m, -1)
    before, after = _halo_specs(tm, m, ch, col0 // ch)
    w_pad = jnp.concatenate([conv_w, jnp.zeros((-taps % 8, ch), F32)], axis=0)
    const = lambda a: pl.BlockSpec(a.shape, lambda i: (0,) * a.ndim)
    out = pl.pallas_call(
        functools.partial(_conv_kernel, tiles_per_sample=tps, ctx_tiles=n_ctx // tm),
        grid=(m // tm,),
        in_specs=[pl.BlockSpec((tm, ch), lambda i: (i, col0 // ch)), before, after, const(w_pad),
                  pl.BlockSpec((1, ch), lambda i: (0, 0))],
        out_specs=pl.BlockSpec((tm, ch), lambda i: (i, 0)),
        out_shape=jax.ShapeDtypeStruct((m, ch), ACT),
        compiler_params=pltpu.CompilerParams(dimension_semantics=("parallel",), vmem_limit_bytes=VMEM_LIMIT),
        name="conv_silu",
    )(p2, p2, p2, w_pad, conv_b.reshape(1, ch))
    return out.reshape(bsz, t, ch)


def _ssd_kernel(xf_ref, dtf_ref, xb_ref, dtb_ref, bias_ref, arow_ref, tabw_ref, tabx_ref, yf_ref, yb_ref,
                s_ref, ct_ref, cp_ref, *, inner, heads_per_group):
    L, hp = SSD_CHUNK, SSM_HEAD_DIM
    e_heads = heads_per_group
    width = e_heads * hp
    gn = SSM_GROUPS * SSM_STATE
    n_heads = SSM_GROUPS * e_heads

    @pl.when(pl.program_id(1) == 0)
    def _():
        s_ref[...] = jnp.zeros_like(s_ref)

    rl = lax.broadcasted_iota(jnp.int32, (L, L), 0)
    cl = lax.broadcasted_iota(jnp.int32, (L, L), 1)
    upto = (rl >= cl, rl <= cl)
    dirs = ((xf_ref, dtf_ref, yf_ref), (xb_ref, dtb_ref, yb_ref))
    for d, (_, dt_ref, _) in enumerate(dirs):
        dt_all = jax.nn.softplus(dt_ref[0].astype(F32) + bias_ref[...])
        cum = _dot(upto[d].astype(F32), dt_all * arow_ref[...], HIGHEST)
        ct_ref[d] = cum.T
        cp_ref[d] = jnp.concatenate(_bf16_pieces(cum, 3) + _bf16_pieces(dt_all, 2), axis=1)

    lane = lax.broadcasted_iota(jnp.int32, (L, LANES), 1)

    def group(g, carry):
        two = range(2)
        xcol = pl.ds(pl.multiple_of(g * width, width), width)
        bcol = pl.ds(pl.multiple_of(inner + g * SSM_STATE, SSM_STATE), SSM_STATE)
        ccol = pl.ds(pl.multiple_of(inner + gn + g * SSM_STATE, SSM_STATE), SSM_STATE)
        x = [dirs[d][0][0, :, xcol].astype(F32) for d in two]
        bm = [dirs[d][0][0, :, bcol].astype(F32) for d in two]
        cmb = [dirs[d][0][0, :, ccol].astype(BF16) for d in two]
        tabw = [tabw_ref[d, g] for d in two]
        tabx = [tabx_ref[d, g] for d in two]
        cum_w = [_dot(cp_ref[d, :, :3 * LANES], tabw[d]) for d in two]
        dt_x = [_dot(cp_ref[d, :, 3 * LANES:], tabx[d]) for d in two]
        s0 = [s_ref[d, g] for d in two]
        cb = [_dot_nt(cmb[d], bm[d].astype(BF16)) for d in two]
        y_off = [_dot(cmb[d], s0[d].astype(BF16)) for d in two]
        bt = [bm[d].T.astype(BF16) for d in two]
        cum_x = [jnp.concatenate(
            [jnp.where(lane < hp, cum_w[d][:, (2 * q) * LANES:(2 * q + 1) * LANES],
                       cum_w[d][:, (2 * q + 1) * LANES:(2 * q + 2) * LANES]) for q in range(e_heads // 2)],
            axis=1) for d in two]
        tot_x = [cum_x[d][0:1] if d == 1 else cum_x[d][L - 1:L] for d in two]
        xdt = [x[d] * dt_x[d] for d in two]
        cbm = [jnp.where(upto[d], cb[d], 0.0) for d in two]
        lhs, rhs = [], []
        for d in two:
            for q in range(e_heads // 2):
                mats = []
                for e in (2 * q, 2 * q + 1):
                    row = d * n_heads + g * e_heads + e
                    seg = cum_w[d][:, e * LANES:(e + 1) * LANES] - ct_ref[d, pl.ds(row, 1), :]
                    mats.append((cbm[d] * jnp.exp(jnp.minimum(seg, 0.0))).astype(BF16))
                slab = xdt[d][:, q * LANES:(q + 1) * LANES]
                lhs.append(jnp.concatenate(mats, axis=1))
                rhs.append(jnp.concatenate([jnp.where(lane < hp, slab, 0.0), jnp.where(lane < hp, 0.0, slab)],
                                           axis=0).astype(BF16))
        y_diag = [_dot(a, b) for a, b in zip(lhs, rhs)]
        s_add = [_dot(bt[d], (xdt[d] * jnp.exp(tot_x[d] - cum_x[d])).astype(BF16)) for d in two]
        for d in two:
            slabs = y_diag[d * (e_heads // 2):(d + 1) * (e_heads // 2)]
            y = y_off[d] * jnp.exp(cum_x[d]) + jnp.concatenate(slabs, axis=1)
            dirs[d][2][0, :, xcol] = y.astype(dirs[d][2].dtype)
            s_ref[d, g] = s0[d] * jnp.exp(tot_x[d]) + s_add[d]
        return carry

    lax.fori_loop(0, SSM_GROUPS, group, 0, unroll=2)


def ssd(xbc, p_c, dt_col, dt_bias, a_log, n_ctx, inner):
    bsz, t, xw = xbc.shape
    L = SSD_CHUNK
    n_heads = dt_bias.shape[1]
    e_heads = n_heads // SSM_GROUPS
    width = e_heads * SSM_HEAD_DIM
    n_cc, n_chunks = n_ctx // L, t // L
    pad = LANES - 2 * n_heads
    bias = jnp.concatenate([dt_bias[0], dt_bias[1], jnp.zeros((pad,), F32)]).reshape(1, LANES)
    arow = jnp.concatenate([-jnp.exp(a_log[0]), -jnp.exp(a_log[1]), jnp.zeros((pad,), F32)]).reshape(1, LANES)
    head_row = (jnp.arange(2)[:, None, None, None] * n_heads + jnp.arange(SSM_GROUPS)[None, :, None, None] * e_heads)
    src = jnp.arange(LANES)[None, None, :, None]
    tabw = (src == head_row + jnp.arange(e_heads * LANES)[None, None, None, :] // LANES).astype(BF16)
    tabx = (src == head_row + jnp.arange(width)[None, None, None, :] // SSM_HEAD_DIM).astype(BF16)
    tabw = jnp.concatenate([tabw] * 3, axis=2)
    tabx = jnp.concatenate([tabx] * 2, axis=2)

    mirrored = lambda s: _mirrored_chunk(s, n_cc, n_chunks)
    const = lambda a: pl.BlockSpec(a.shape, lambda i, s: (0,) * a.ndim)
    return pl.pallas_call(
        functools.partial(_ssd_kernel, inner=inner, heads_per_group=e_heads),
        grid=(bsz, n_chunks),
        in_specs=[
            pl.BlockSpec((1, L, xw), lambda i, s: (i, s, 0)),
            pl.BlockSpec((1, L, LANES), lambda i, s: (i, s, dt_col // LANES)),
            pl.BlockSpec((1, L, xw), lambda i, s: (i, mirrored(s), 0)),
            pl.BlockSpec((1, L, LANES), lambda i, s: (i, mirrored(s), dt_col // LANES)),
            const(bias), const(arow), const(tabw), const(tabx),
        ],
        out_specs=[pl.BlockSpec((1, L, inner), lambda i, s: (i, s, 0)),
                   pl.BlockSpec((1, L, inner), lambda i, s: (i, mirrored(s), 0))],
        out_shape=[jax.ShapeDtypeStruct((bsz, t, inner), ACT)] * 2,
        scratch_shapes=[pltpu.VMEM((2, SSM_GROUPS, SSM_STATE, width), F32),
                        pltpu.VMEM((2, LANES, L), F32),
                        pltpu.VMEM((2, L, 5 * LANES), BF16)],
        compiler_params=pltpu.CompilerParams(dimension_semantics=("parallel", "arbitrary"),
                                             vmem_limit_bytes=VMEM_LIMIT),
        name="ssd",
    )(xbc, p_c, xbc, p_c, bias, arow, tabw, tabx)


def _merge_kernel(x_ref, yf_ref, yb_ref, r_ref, k_ref, v_ref, og_ref, z_ref, grw_ref, gssm_ref, gl_ref, gc_ref,
                  lnw_ref, lnb_ref, rk_ref, wrw_ref, wout_ref, o_ref, *, tm, tiles_per_sample, n_ctx):
    f32 = lambda ref: ref[...].astype(F32)
    y = f32(yf_ref) + f32(yb_ref)
    inv_n = 1.0 / RW_HEAD_DIM
    centred = y - _group_sums(y, RW_HEAD_DIM) * inv_n
    var = _group_sums(centred * centred, RW_HEAD_DIM) * inv_n
    y = centred * lax.rsqrt(var + RW_GN_EPS) * lnw_ref[...] + lnb_ref[...]
    y = y + _group_sums(f32(r_ref) * f32(k_ref) * rk_ref[...], RW_HEAD_DIM) * f32(v_ref)
    y_rw = y * og_ref[...]
    t1 = _dot(y_rw.astype(wrw_ref.dtype), wrw_ref[...])
    merged = jax.nn.sigmoid(f32(grw_ref)) * t1 + jax.nn.sigmoid(f32(gssm_ref)) * f32(z_ref)
    mix = _dot(merged.astype(wout_ref.dtype), wout_ref[...])
    gate = jnp.where(_ctx_rows(tm, tiles_per_sample, n_ctx), gc_ref[...], gl_ref[0])
    o_ref[...] = x_ref[...] + gate * mix


def merge(x, y_f, y_b, r, k, v, out_gate, z_ssm, proj, col_grw, col_gssm, gate_l, gate_c, ln_w, ln_b, r_k,
          w_rw, w_out, n_ctx):
    bsz, t, d = x.shape
    tm = _row_tile(t, 272)
    tps = t // tm
    m = bsz * t
    rows = lambda blk: pl.BlockSpec((tm, d), lambda i: (i, blk))
    const = lambda shape: pl.BlockSpec(shape, lambda i: (0,) * len(shape))
    flat = lambda a: a.reshape(m, -1)
    out = pl.pallas_call(
        functools.partial(_merge_kernel, tm=tm, tiles_per_sample=tps, n_ctx=n_ctx),
        grid=(m // tm,),
        in_specs=[rows(0)] * 8 + [rows(col_grw // d), rows(col_gssm // d),
                                  pl.BlockSpec((1, 1, d), lambda i: (i // tps, 0, 0)), const((1, d)),
                                  const((1, d)), const((1, d)), const((1, d)),
                                  const(w_rw.shape), const(w_out.shape)],
        out_specs=rows(0),
        out_shape=jax.ShapeDtypeStruct((m, d), F32),
        compiler_params=pltpu.CompilerParams(dimension_semantics=("parallel",), vmem_limit_bytes=VMEM_LIMIT),
        name="merge",
    )(flat(x), flat(y_f), flat(y_b), flat(r), flat(k), flat(v), flat(out_gate), flat(z_ssm), flat(proj), flat(proj),
      gate_l.reshape(bsz, 1, d), gate_c.reshape(1, d), ln_w.reshape(1, d), ln_b.reshape(1, d), r_k.reshape(1, d),
      w_rw, w_out)
    return out.reshape(bsz, t, d)


def _ssm_out_kernel(yf_ref, yb_ref, xs_ref, z_ref, dskip_ref, nw_ref, w_ref, o_ref, *, group):
    f32 = lambda ref: ref[...].astype(F32)
    y = (f32(yf_ref) + f32(yb_ref) + dskip_ref[...] * f32(xs_ref)) * jax.nn.silu(f32(z_ref))
    ms = _group_sums(y * y, group) * (1.0 / group)
    y = y * lax.rsqrt(ms + SSM_NORM_EPS) * nw_ref[...]
    o_ref[...] = _dot(y.astype(w_ref.dtype), w_ref[...]).astype(o_ref.dtype)


def ssm_out(y_f, y_b, xbc, p_c, z_col, d_skip, norm_w, w):
    bsz, t, inner = y_f.shape
    m = bsz * t
    d = w.shape[1]
    tm = _row_tile(t, 272)
    rows = lambda blk: pl.BlockSpec((tm, inner), lambda i: (i, blk))
    const = lambda shape: pl.BlockSpec(shape, lambda i: (0,) * len(shape))
    flat = lambda a: a.reshape(m, -1)
    out = pl.pallas_call(
        functools.partial(_ssm_out_kernel, group=inner // SSM_GROUPS),
        grid=(m // tm,),
        in_specs=[rows(0), rows(0), rows(0), rows(z_col // inner), const((1, inner)), const((1, inner)),
                  const(w.shape)],
        out_specs=pl.BlockSpec((tm, d), lambda i: (i, 0)),
        out_shape=jax.ShapeDtypeStruct((m, d), ACT),
        compiler_params=pltpu.CompilerParams(dimension_semantics=("parallel",), vmem_limit_bytes=VMEM_LIMIT),
        name="ssm_out",
    )(flat(y_f), flat(y_b), flat(xbc), flat(p_c), d_skip.reshape(1, inner), norm_w.reshape(1, inner), w)
    return out.reshape(bsz, t, d)


def _router_kernel(x_ref, g_ref, scl_ref, shl_ref, scc_ref, shc_ref, wr_ref, h_ref, logit_ref, *,
                   tm, tiles_per_sample, n_ctx):
    is_ctx = _ctx_rows(tm, tiles_per_sample, n_ctx)
    h = _modulated(x_ref[...], g_ref[...], scl_ref[0], shl_ref[0], scc_ref[...], shc_ref[...], is_ctx)
    bits = pltpu.bitcast(h.astype(BF16).astype(F32), jnp.uint32)
    half = h.shape[1] // 2
    h_ref[...] = (bits[:, :half] >> 16) | (bits[:, half:] & jnp.uint32(0xFFFF0000))
    logit_ref[...] = _dot(h, wr_ref[...], HIGHEST)


def router(x, gain, scale_l, shift_l, scale_c, shift_c, w_router_padded, n_ctx):
    bsz, t, d = x.shape
    tm = _row_tile(t, 544)
    tps = t // tm
    m = bsz * t
    row = lambda v: v.reshape(1, d)
    const = lambda shape: pl.BlockSpec(shape, lambda i: (0,) * len(shape))
    per_sample = pl.BlockSpec((1, 1, d), lambda i: (i // tps, 0, 0))
    return pl.pallas_call(
        functools.partial(_router_kernel, tm=tm, tiles_per_sample=tps, n_ctx=n_ctx),
        grid=(m // tm,),
        in_specs=[pl.BlockSpec((tm, d), lambda i: (i, 0)), const((1, d)), per_sample, per_sample,
                  const((1, d)), const((1, d)), const(w_router_padded.shape)],
        out_specs=[pl.BlockSpec((tm, d // 2), lambda i: (i, 0)), pl.BlockSpec((tm, LANES), lambda i: (i, 0))],
        out_shape=[jax.ShapeDtypeStruct((m, d // 2), jnp.uint32), jax.ShapeDtypeStruct((m, LANES), F32)],
        compiler_params=pltpu.CompilerParams(dimension_semantics=("parallel",), vmem_limit_bytes=VMEM_LIMIT),
        name="router",
    )(x.reshape(m, d), row(gain), scale_l.reshape(bsz, 1, d), shift_l.reshape(bsz, 1, d),
      row(scale_c), row(shift_c), w_router_padded)


def _experts_kernel(be_ref, nb_ref, x_ref, w1_ref, w3_ref, w2_ref, o_ref, w1b_ref, w3b_ref, w2b_ref):
    i = pl.program_id(0)

    @pl.when((i == 0) | (be_ref[i] != be_ref[jnp.maximum(i - 1, 0)]))
    def _():
        w1b_ref[...] = w1_ref[0, 0].astype(BF16)
        w3b_ref[...] = w3_ref[0, 0].astype(BF16)
        w2b_ref[...] = w2_ref[0, 0].astype(BF16)

    @pl.when(i < nb_ref[0])
    def _():
        packed = x_ref[...]
        x = jnp.concatenate([pltpu.bitcast(packed << 16, F32),
                             pltpu.bitcast(packed & jnp.uint32(0xFFFF0000), F32)], axis=1).astype(BF16)
        hidden = jax.nn.silu(_dot(x, w1b_ref[...])) * _dot(x, w3b_ref[...])
        o_ref[...] = _dot(hidden.astype(BF16), w2b_ref[...])

    @pl.when(i >= nb_ref[0])
    def _():
        o_ref[...] = jnp.zeros_like(o_ref)


def experts(xb, block_expert, n_used, w1, w3, w2, layer):
    n_rows = xb.shape[0]
    d, de = w1.shape[2:]
    rows = MOE_ROWS
    n_blocks = n_rows // rows
    return pl.pallas_call(
        _experts_kernel,
        grid_spec=pltpu.PrefetchScalarGridSpec(
            num_scalar_prefetch=2,
            grid=(n_blocks,),
            in_specs=[
                pl.BlockSpec((rows, d // 2), lambda i, be, nb: (i, 0)),
                pl.BlockSpec((1, 1, d, de), lambda i, be, nb: (layer, be[i], 0, 0)),
                pl.BlockSpec((1, 1, d, de), lambda i, be, nb: (layer, be[i], 0, 0)),
                pl.BlockSpec((1, 1, de, d), lambda i, be, nb: (layer, be[i], 0, 0)),
            ],
            out_specs=pl.BlockSpec((rows, d), lambda i, be, nb: (i, 0)),
            scratch_shapes=[pltpu.VMEM((d, de), BF16), pltpu.VMEM((d, de), BF16), pltpu.VMEM((de, d), BF16)],
        ),
        out_shape=jax.ShapeDtypeStruct((n_rows, d), F32),
        compiler_params=pltpu.CompilerParams(dimension_semantics=("arbitrary",), vmem_limit_bytes=VMEM_LIMIT),
        name="experts",
    )(block_expert, n_used, xb, w1, w3, w2)


def _combine_kernel(x_ref, y_ref, gate_ref, gl_ref, gc_ref, fin_ref, o_ref, *, tm, tiles_per_sample, n_ctx, final):
    d = x_ref.shape[1]
    g = gate_ref[...]
    f = y_ref[:, :d] * g[:, 0:1] + y_ref[:, d:] * g[:, 1:2]
    mod = jnp.where(_ctx_rows(tm, tiles_per_sample, n_ctx), gc_ref[...], gl_ref[0])
    out = x_ref[...] + mod * f
    if final:
        out = out * lax.rsqrt(jnp.mean(out * out, axis=-1, keepdims=True) + NORM_EPS) * fin_ref[...]
    o_ref[...] = out


def combine(x, y_pairs, gates, gate_l, gate_c, final_gain, n_ctx, final):
    bsz, t, d = x.shape
    m = bsz * t
    tm = _row_tile(t, 544)
    tps = t // tm
    const = lambda shape: pl.BlockSpec(shape, lambda i: (0,) * len(shape))
    out = pl.pallas_call(
        functools.partial(_combine_kernel, tm=tm, tiles_per_sample=tps, n_ctx=n_ctx, final=final),
        grid=(m // tm,),
        in_specs=[pl.BlockSpec((tm, d), lambda i: (i, 0)), pl.BlockSpec((tm, TOP_K * d), lambda i: (i, 0)),
                  pl.BlockSpec((tm, LANES), lambda i: (i, 0)), pl.BlockSpec((1, 1, d), lambda i: (i // tps, 0, 0)),
                  const((1, d)), const((1, d))],
        out_specs=pl.BlockSpec((tm, d), lambda i: (i, 0)),
        out_shape=jax.ShapeDtypeStruct((m, d), F32),
        compiler_params=pltpu.CompilerParams(dimension_semantics=("parallel",), vmem_limit_bytes=VMEM_LIMIT),
        name="combine",
    )(x.reshape(m, d), y_pairs.reshape(m, TOP_K * d), gates, gate_l.reshape(bsz, 1, d), gate_c.reshape(1, d),
      final_gain.reshape(1, d))
    return out.reshape(bsz, t, d)


def _top2(vals):
    idx = jnp.arange(vals.shape[-1], dtype=jnp.int32)
    i1 = jnp.argmax(vals, axis=-1).astype(jnp.int32)
    v1 = jnp.max(vals, axis=-1)
    rest = jnp.where(idx == i1[..., None], -jnp.inf, vals)
    i2 = jnp.argmax(rest, axis=-1).astype(jnp.int32)
    v2 = jnp.max(rest, axis=-1)
    return v1, i1, v2, i2


def moe(h, logits, b_router, w1, w3, w2, layer):
    n_tok, d = h.shape
    n_exp = b_router.shape[0]
    epg = n_exp // N_EXPERT_GROUPS
    scores = jax.nn.sigmoid(logits)
    biased = (scores + b_router.astype(F32)).reshape(n_tok, N_EXPERT_GROUPS, epg)
    g1, _, g2, _ = _top2(biased)
    top_group = jnp.argmax(g1 + g2, axis=-1).astype(jnp.int32)
    in_top = jnp.arange(N_EXPERT_GROUPS, dtype=jnp.int32)[None, :, None] == top_group[:, None, None]
    in_group = jnp.sum(jnp.where(in_top, biased, 0.0), axis=1)
    _, l1, _, l2 = _top2(in_group)
    expert = top_group[:, None] * epg + jnp.stack([l1, l2], axis=-1)
    picked = expert[:, :, None] == jnp.arange(n_exp, dtype=jnp.int32)[None, None, :]
    gate = jnp.sum(jnp.where(picked, scores[:, None, :], 0.0), axis=-1)
    gate = gate / jnp.sum(gate, axis=-1, keepdims=True)

    n_assign = n_tok * TOP_K
    flat_e = expert.reshape(-1).astype(jnp.int32)
    onehot = picked.reshape(n_assign, n_exp).astype(jnp.int32)
    rank = jnp.sum((jnp.cumsum(onehot, axis=0) - onehot) * onehot, axis=-1)
    counts = jnp.sum(onehot, axis=0)
    padded = (counts + MOE_ROWS - 1) // MOE_ROWS * MOE_ROWS
    pad_end = jnp.cumsum(padded)
    dest = ((pad_end - padded)[flat_e] + rank).astype(jnp.int32)
    n_blocks = -(-n_assign // MOE_ROWS) + n_exp
    slot_token = jnp.zeros((n_blocks * MOE_ROWS,), jnp.int32).at[dest].set(
        jnp.arange(n_assign, dtype=jnp.int32) // TOP_K)
    block_expert = jnp.clip(
        jnp.sum(jnp.arange(n_blocks, dtype=jnp.int32)[:, None] >= (pad_end // MOE_ROWS)[None, :], axis=-1),
        0, n_exp - 1).astype(jnp.int32)
    n_used = (pad_end[-1:] // MOE_ROWS).astype(jnp.int32)
    yb = experts(h[slot_token], block_expert, n_used, w1, w3, w2, layer)
    return yb[dest], jnp.pad(gate, ((0, 0), (0, LANES - TOP_K)))


def _ssm_branch(p, prm, w_branch, n_ctx):
    inner = prm["norm_w"].shape[0]
    xbc_w = prm["conv_w"].shape[1]
    xbc = conv_silu(p, 0, prm["conv_w"], prm["conv_b"], n_ctx)
    y_f, y_b = ssd(xbc, p, xbc_w + inner, prm["dt_bias"], prm["a_log"], n_ctx, inner)
    d_skip = jnp.repeat(prm["d"][0] + prm["d"][1], SSM_HEAD_DIM)
    return ssm_out(y_f, y_b, xbc, p, xbc_w, d_skip, prm["norm_w"], w_branch)


def _pad_cols(w, n):
    return jnp.pad(w, ((0, 0), (0, n - w.shape[1])))


def kernel(x, c, ctx, c_ctx, w_mod, b_mod, norm_mix_g, w_in, rw_shift_mu, rw_w0, rw_w2, rw_a0, rw_a2, rw_g2,
           rw_k_k, rw_k_a, rw_r_k, rw_ln_w, rw_ln_b, ssm_conv_w, ssm_conv_b, ssm_dt_bias, ssm_a_log, ssm_d,
           ssm_norm_w, w_branch_rw, w_branch_ssm, w_out, norm_ffn_g, w_router, b_router, exp_w1, exp_w3, exp_w2,
           norm_final_g):
    bsz, n_lat, d = x.shape
    depth = w_in.shape[0]
    n_ctx = ctx.shape[1]
    rows = n_lat // GRID_W
    t = n_ctx + n_lat
    width = RW_HEADS * RW_HEAD_DIM
    dl, al, gl = rw_w2.shape[2], rw_a2.shape[2], rw_g2.shape[1]
    inner = ssm_norm_w.shape[1]
    xbc_w = ssm_conv_w.shape[2]
    n_heads = ssm_dt_bias.shape[2]
    n_exp = w_router.shape[1]

    def to_c(a):
        ch = a.shape[-1]
        lat = a[:, n_ctx:].reshape(bsz, rows, GRID_W, ch).transpose(0, 2, 1, 3).reshape(bsz, n_lat, ch)
        return jnp.concatenate([a[:, :n_ctx], lat], axis=1)

    def to_r(a):
        ch = a.shape[-1]
        lat = a[:, n_ctx:].reshape(bsz, GRID_W, rows, ch).transpose(0, 2, 1, 3).reshape(bsz, n_lat, ch)
        return jnp.concatenate([a[:, :n_ctx], lat], axis=1)

    o = 0
    src = {}
    for name, size in (("r", width), ("w_f", dl), ("w_b", dl), ("k", width), ("v", width), ("a_f", al),
                       ("a_b", al), ("g", gl), ("z", inner), ("xbc", xbc_w), ("dt", 2 * n_heads),
                       ("g_rw", d), ("g_ssm", d)):
        src[name] = (o, size)
        o += size
    take = lambda a, name: a[..., src[name][0]:src[name][0] + src[name][1]]
    lora_w = 2 * dl + 2 * al
    lora_pad = -(-lora_w // LANES) * LANES
    g_pad = -(-gl // LANES) * LANES
    dt_pad = -(-2 * n_heads // LANES) * LANES
    lay = {"r": (0, width), "k": (width, width), "v": (2 * width, width), "g_rw": (3 * width, d),
           "g_ssm": (3 * width + d, d), "lora": (3 * width + 2 * d, lora_w),
           "g": (3 * width + 2 * d + lora_pad, gl)}

    silu_c = jax.nn.silu(c)
    silu_cc = jax.nn.silu(c_ctx)[None, :]
    act = jnp.concatenate([silu_c, silu_cc, jnp.zeros((-(bsz + 1) % 8, d), F32)], axis=0)
    w_router_p = _pad_cols(w_router.astype(F32), LANES)

    xr = jnp.concatenate([ctx, x], axis=1)
    for l in range(depth):
        mod = matmul(act, w_mod[l], precision=HIGHEST, name="modulation")[:bsz + 1] + b_mod[l]
        shift_m, scale_m, gate_m, shift_f, scale_f, gate_f = jnp.split(mod[:bsz], 6, axis=-1)
        cshift_m, cscale_m, cgate_m, cshift_f, cscale_f, cgate_f = jnp.split(mod[bsz], 6, axis=-1)

        wl = w_in[l]
        w_r = jnp.concatenate(
            [take(wl, "r"), take(wl, "k"), take(wl, "v"), take(wl, "g_rw"), take(wl, "g_ssm"),
             _pad_cols(jnp.concatenate([take(wl, n) for n in ("w_f", "w_b", "a_f", "a_b")], axis=1), lora_pad),
             _pad_cols(take(wl, "g"), g_pad)], axis=1).astype(BF16)
        w_c = jnp.concatenate([take(wl, "xbc"), take(wl, "z"), _pad_cols(take(wl, "dt"), dt_pad)],
                              axis=1).astype(BF16)
        mods = (norm_mix_g[l], scale_m, shift_m, cscale_m, cshift_m)
        h_mix = mod_norm(xr, *mods, n_ctx)
        p_r = project(h_mix, w_r, ACT)
        p_c = project(to_c(h_mix), w_c, ACT)

        mu = rw_shift_mu[l]
        rw_prm = dict(
            shift_mu={"r": take(mu, "r"), "k": take(mu, "k"), "v": take(mu, "v"), "g": take(mu, "g"),
                      "lora": jnp.concatenate([take(mu, n) for n in ("w_f", "w_b", "a_f", "a_b")])},
            w0=rw_w0[l], w2=rw_w2[l], a0=rw_a0[l], a2=rw_a2[l], g2=rw_g2[l],
            k_k=rw_k_k[l], k_a=rw_k_a[l], r_k=rw_r_k[l], ln_w=rw_ln_w[l], ln_b=rw_ln_b[l])
        r, k, v, kk, lw_f, lw_b, ar_f, ar_b, out_gate = rwkv_prep(p_r, lay, rw_prm, n_ctx)
        y_f, y_b = wkv7(r, k, v, kk, (lw_f, lw_b), (ar_f, ar_b), rw_k_a[l], n_ctx)
        ssm_prm = dict(conv_w=ssm_conv_w[l], conv_b=ssm_conv_b[l], dt_bias=ssm_dt_bias[l], a_log=ssm_a_log[l],
                       d=ssm_d[l], norm_w=ssm_norm_w[l])
        z_ssm = to_r(_ssm_branch(p_c, ssm_prm, w_branch_ssm[l].astype(BF16), n_ctx))
        xr = merge(xr, y_f, y_b, r, k, v, out_gate, z_ssm, p_r, lay["g_rw"][0], lay["g_ssm"][0], gate_m, cgate_m,
                   rw_ln_w[l], rw_ln_b[l], rw_r_k[l], w_branch_rw[l].astype(BF16), w_out[l].astype(BF16), n_ctx)

        h, logits = router(xr, norm_ffn_g[l], scale_f, shift_f, cscale_f, cshift_f, w_router_p, n_ctx)
        y_pairs, gates = moe(h, logits[:, :n_exp], b_router, exp_w1, exp_w3, exp_w2, l)
        xr = combine(xr, y_pairs, gates, gate_f, cgate_f, norm_final_g, n_ctx, final=(l == depth - 1))
    return xr[:, n_ctx:]
```

```python
import functools
import math

import jax
import jax.numpy as jnp
from jax import lax
from jax.experimental import pallas as pl
from jax.experimental.pallas import tpu as pltpu

F32 = jnp.float32
BF16 = jnp.bfloat16
HIGHEST = lax.Precision.HIGHEST

GRID_W = 64
RW_HEADS = 16
RW_HEAD_DIM = 64
RW_GN_EPS = 64e-5
SSM_HEAD_DIM = 64
SSM_GROUPS = 8
SSM_STATE = 128
SSM_CONV = 5
SSM_NORM_EPS = 1e-5
N_EXPERT_GROUPS = 4
TOP_K = 2
NORM_EPS = 1e-6

LANES = 128
WKV_CHUNK = 64
WKV_PAIRS = 8
SSD_CHUNK = 128
MOE_ROWS = 256
VMEM_LIMIT = 56 * 1024 * 1024


def _dot(a, b, precision=None):
    return jnp.dot(a, b, preferred_element_type=F32, precision=precision)


def _dot_nt(a, b, precision=None):
    return lax.dot_general(a, b, (((1,), (1,)), ((), ())), preferred_element_type=F32, precision=precision)


def _dot_tn(a, b, precision=None):
    return lax.dot_general(a, b, (((0,), (0,)), ((), ())), preferred_element_type=F32, precision=precision)


def _row_tile(n_rows_per_sample, limit=1088):
    for tm in (1088, 544, 512, 272, 256, 128, 64, 32, 16):
        if tm <= limit and n_rows_per_sample % tm == 0:
            return tm
    raise ValueError(n_rows_per_sample)


def _col_tile(n_cols, limit=1536):
    best = LANES
    for k in range(1, n_cols // LANES + 1):
        tn = k * LANES
        if n_cols % tn == 0 and tn <= limit:
            best = tn
    return best


def _bf16_pieces(x, n):
    pieces = []
    for _ in range(n):
        p = x.astype(BF16)
        pieces.append(p)
        x = x - p.astype(F32)
    return pieces


def _group_sums(x, group):
    span = max(group, LANES)
    rr = lax.broadcasted_iota(jnp.int32, (span, span), 0) // group
    cc = lax.broadcasted_iota(jnp.int32, (span, span), 1) // group
    ones = (rr == cc).astype(BF16)
    pieces = _bf16_pieces(x, 3)
    cols = []
    for j in range(x.shape[1] // span):
        sl = slice(j * span, (j + 1) * span)
        cols.append(_dot(pieces[0][:, sl], ones) + _dot(pieces[1][:, sl], ones) + _dot(pieces[2][:, sl], ones))
    return cols[0] if len(cols) == 1 else jnp.concatenate(cols, axis=1)


HALO = 16
ACT = BF16


def _seg_tile(n_ctx, n_lat, limit):
    for tm in (1024, 512, 256, 128, 64, 32, 16):
        if tm <= limit and n_ctx % tm == 0 and n_lat % tm == 0:
            return tm
    raise ValueError((n_ctx, n_lat))


def _segment_taps(x_ref, before_ref, after_ref, offsets, tiles_per_sample, ctx_tiles):
    tm = x_ref.shape[0]
    ti = pl.program_id(0) % tiles_per_sample
    first = (ti == 0) | (ti == ctx_tiles)
    last = (ti == ctx_tiles - 1) | (ti == tiles_per_sample - 1)
    x = x_ref[...].astype(F32)
    before = before_ref[...].astype(F32) * jnp.where(first, 0.0, 1.0)
    after = after_ref[...].astype(F32) * jnp.where(last, 0.0, 1.0)
    ext = jnp.concatenate([before, x, after], axis=0)
    n = tm + 2 * HALO
    return x, [pltpu.roll(ext, (-o) % n, 0)[HALO:HALO + tm] for o in offsets]


def _halo_specs(tm, n_rows, width, col_block):
    last = n_rows // HALO - 1
    return (pl.BlockSpec((HALO, width), lambda i: (jnp.maximum(i * (tm // HALO) - 1, 0), col_block)),
            pl.BlockSpec((HALO, width), lambda i: (jnp.minimum((i + 1) * (tm // HALO), last), col_block)))


def _mirrored_chunk(s, n_ctx_chunks, n_chunks):
    return jnp.where(s < n_ctx_chunks, n_ctx_chunks - 1 - s, n_chunks - 1 + n_ctx_chunks - s)


def _modulated(x, g, sc_l, sh_l, sc_c, sh_c, is_ctx):
    y = x * lax.rsqrt(jnp.mean(x * x, axis=-1, keepdims=True) + NORM_EPS) * g
    return y * (1.0 + jnp.where(is_ctx, sc_c, sc_l)) + jnp.where(is_ctx, sh_c, sh_l)


def _ctx_rows(tm, tiles_per_sample, n_ctx):
    row0 = (pl.program_id(0) % tiles_per_sample) * tm
    return row0 + lax.broadcasted_iota(jnp.int32, (tm, 1), 0) < n_ctx


def _mod_norm_kernel(x_ref, g_ref, scl_ref, shl_ref, scc_ref, shc_ref, h_ref, *, tm, tiles_per_sample, n_ctx):
    is_ctx = _ctx_rows(tm, tiles_per_sample, n_ctx)
    h = _modulated(x_ref[...], g_ref[...], scl_ref[0], shl_ref[0], scc_ref[...], shc_ref[...], is_ctx)
    h_ref[...] = h.astype(h_ref.dtype)


def mod_norm(x, gain, scale_l, shift_l, scale_c, shift_c, n_ctx):
    bsz, t, d = x.shape
    tm = _row_tile(t)
    tps = t // tm
    row = lambda v: v.reshape(1, d)
    const = pl.BlockSpec((1, d), lambda i: (0, 0))
    per_sample = pl.BlockSpec((1, 1, d), lambda i: (i // tps, 0, 0))
    out = pl.pallas_call(
        functools.partial(_mod_norm_kernel, tm=tm, tiles_per_sample=tps, n_ctx=n_ctx),
        grid=(bsz * tps,),
        in_specs=[pl.BlockSpec((tm, d), lambda i: (i, 0)), const, per_sample, per_sample, const, const],
        out_specs=pl.BlockSpec((tm, d), lambda i: (i, 0)),
        out_shape=jax.ShapeDtypeStruct((bsz * t, d), BF16),
        compiler_params=pltpu.CompilerParams(dimension_semantics=("parallel",), vmem_limit_bytes=VMEM_LIMIT),
        name="mod_norm",
    )(x.reshape(bsz * t, d), row(gain), scale_l.reshape(bsz, 1, d), shift_l.reshape(bsz, 1, d),
      row(scale_c), row(shift_c))
    return out.reshape(bsz, t, d)


def _project_kernel(h_ref, w_ref, o_ref):
    o_ref[...] = _dot(h_ref[...], w_ref[...]).astype(o_ref.dtype)


def project(h, w, out_dtype):
    bsz, t, d = h.shape
    n = w.shape[1]
    m = bsz * t
    tm = _row_tile(t, 544)
    tn = _col_tile(n, limit=n)
    out = pl.pallas_call(
        _project_kernel,
        grid=(n // tn, m // tm),
        in_specs=[pl.BlockSpec((tm, d), lambda j, i: (i, 0)), pl.BlockSpec((d, tn), lambda j, i: (0, j))],
        out_specs=pl.BlockSpec((tm, tn), lambda j, i: (i, j)),
        out_shape=jax.ShapeDtypeStruct((m, n), out_dtype),
        compiler_params=pltpu.CompilerParams(dimension_semantics=("parallel", "parallel"),
                                             vmem_limit_bytes=VMEM_LIMIT),
        name="project",
    )(h.reshape(m, d), w)
    return out.reshape(bsz, t, n)


def _matmul_kernel(x_ref, w_ref, o_ref, *, act, precision):
    x = x_ref[...]
    if act == "tanh":
        x = jnp.tanh(x)
    elif act == "sigmoid":
        x = jax.nn.sigmoid(x)
    o_ref[...] = _dot(x.astype(w_ref.dtype), w_ref[...], precision).astype(o_ref.dtype)


def matmul(x, w, act=None, precision=None, out_dtype=F32, name="matmul"):
    m, k = x.shape
    n = w.shape[1]
    tm = _row_tile(m) if m >= 16 else m
    tn = _col_tile(n)
    return pl.pallas_call(
        functools.partial(_matmul_kernel, act=act, precision=precision),
        grid=(m // tm, n // tn),
        in_specs=[pl.BlockSpec((tm, k), lambda i, j: (i, 0)), pl.BlockSpec((k, tn), lambda i, j: (0, j))],
        out_specs=pl.BlockSpec((tm, tn), lambda i, j: (i, j)),
        out_shape=jax.ShapeDtypeStruct((m, n), out_dtype),
        compiler_params=pltpu.CompilerParams(dimension_semantics=("parallel", "parallel"),
                                             vmem_limit_bytes=VMEM_LIMIT),
        name=name,
    )(x, w)


def _rwkv_prep_kernel(m_ref, mb_ref, ma_ref, s_ref, sb_ref, sa_ref, mu_m_ref, mu_s_ref, kk_ref, bias_ref,
                      wl_ref, g2_ref, r_ref, k_ref, v_ref, kko_ref, lwf_ref, lwb_ref, arf_ref, arb_ref, og_ref, *,
                      tiles_per_sample, ctx_tiles, width, n_decay):
    def shifted(x_ref, before_ref, after_ref, mu_ref):
        x, (prev, nxt) = _segment_taps(x_ref, before_ref, after_ref, (-1, 1), tiles_per_sample, ctx_tiles)
        return x + mu_ref[...] * (0.5 * (prev + nxt) - x)

    main = shifted(m_ref, mb_ref, ma_ref, mu_m_ref)
    small = shifted(s_ref, sb_ref, sa_ref, mu_s_ref)
    k = main[:, width:2 * width]
    r_ref[...] = main[:, :width].astype(r_ref.dtype)
    k_ref[...] = k.astype(k_ref.dtype)
    v_ref[...] = main[:, 2 * width:].astype(v_ref.dtype)
    kk = k * kk_ref[...]
    kko_ref[...] = (kk * lax.rsqrt(jnp.maximum(_group_sums(kk * kk, RW_HEAD_DIM), 1e-24))).astype(kko_ref.dtype)

    half = small.shape[1] // 2
    lora = small[:, :half]
    lane = lax.broadcasted_iota(jnp.int32, lora.shape, 1)
    heads = _dot(jnp.where(lane < n_decay, jnp.tanh(lora), lora).astype(BF16), wl_ref[...]) + bias_ref[...]
    for d, (lw_ref, ar_ref) in enumerate(((lwf_ref, arf_ref), (lwb_ref, arb_ref))):
        w_pre = heads[:, d * width:(d + 1) * width]
        lw_ref[...] = -jnp.exp(-jax.nn.softplus(-w_pre) - 0.5)
        ar_ref[...] = jax.nn.sigmoid(heads[:, (2 + d) * width:(3 + d) * width])
    og_ref[...] = _dot(jax.nn.sigmoid(small[:, half:]).astype(BF16), g2_ref[...])


def rwkv_prep(p, lay, prm, n_ctx):
    bsz, t, _ = p.shape
    width = RW_HEADS * RW_HEAD_DIM
    m = bsz * t
    tm = _seg_tile(n_ctx, t - n_ctx, 256)
    tps = t // tm
    p2 = p.reshape(m, -1)
    main_w = 3 * width
    small0 = lay["lora"][0]
    small_w = p2.shape[1] - small0
    half = small_w // 2
    dl, al, gl = prm["w2"].shape[1], prm["a2"].shape[1], prm["g2"].shape[0]
    assert lay["r"][0] == 0 and lay["g"][0] == small0 + half and small0 % small_w == 0
    mu = prm["shift_mu"]
    pad1 = lambda a, n: jnp.pad(a, (0, n - a.shape[0]))
    mu_main = jnp.concatenate([mu["r"], mu["k"], mu["v"]]).reshape(1, main_w)
    mu_small = jnp.concatenate([pad1(mu["lora"], half), pad1(mu["g"], half)]).reshape(1, small_w)
    wl = jnp.zeros((half, 4 * width), F32)
    for j, (blk, rows0, nrows) in enumerate(((prm["w2"][0], 0, dl), (prm["w2"][1], dl, dl),
                                             (prm["a2"][0], 2 * dl, al), (prm["a2"][1], 2 * dl + al, al))):
        wl = wl.at[rows0:rows0 + nrows, j * width:(j + 1) * width].set(blk)
    bias = jnp.concatenate([prm["w0"][0], prm["w0"][1], prm["a0"][0], prm["a0"][1]]).reshape(1, 4 * width)
    g2 = jnp.pad(prm["g2"], ((0, half - gl), (0, 0))).astype(BF16)

    mb, ma = _halo_specs(tm, m, main_w, 0)
    sb, sa = _halo_specs(tm, m, small_w, small0 // small_w)
    const = lambda a: pl.BlockSpec(a.shape, lambda i: (0,) * a.ndim)
    out_spec = pl.BlockSpec((tm, width), lambda i: (i, 0))
    outs = pl.pallas_call(
        functools.partial(_rwkv_prep_kernel, tiles_per_sample=tps, ctx_tiles=n_ctx // tm, width=width,
                          n_decay=2 * dl),
        grid=(m // tm,),
        in_specs=[pl.BlockSpec((tm, main_w), lambda i: (i, 0)), mb, ma,
                  pl.BlockSpec((tm, small_w), lambda i: (i, small0 // small_w)), sb, sa,
                  const(mu_main), const(mu_small), pl.BlockSpec((1, width), lambda i: (0, 0)), const(bias),
                  pl.BlockSpec(wl.shape, lambda i: (0, 0)), const(g2)],
        out_specs=[out_spec] * 9,
        out_shape=[jax.ShapeDtypeStruct((m, width), ACT)] * 4 + [jax.ShapeDtypeStruct((m, width), F32)] * 5,
        compiler_params=pltpu.CompilerParams(dimension_semantics=("parallel",), vmem_limit_bytes=VMEM_LIMIT),
        name="rwkv_prep",
    )(p2, p2, p2, p2, p2, p2, mu_main, mu_small, prm["k_k"].reshape(1, width), bias, wl.astype(BF16), g2)
    return [o.reshape(bsz, t, width) for o in outs]


def _wkv_chunks(chains):
    c = chains[0][0].shape[0]
    n = 2 * c
    rev = [ch[8] for ch in chains]
    each = lambda f, *cols: [f(*a) for a in zip(*cols)]
    bf = lambda x: x.astype(BF16)
    cat0 = lambda *xs: jnp.concatenate([bf(x) for x in xs], axis=0)
    cat1 = lambda *xs: jnp.concatenate([bf(x) for x in xs], axis=1)

    rc = lax.broadcasted_iota(jnp.int32, (c, c), 0)
    cc = lax.broadcasted_iota(jnp.int32, (c, c), 1)
    seen = {False: (rc >= cc).astype(F32), True: (rc <= cc).astype(F32)}
    r2 = lax.broadcasted_iota(jnp.int32, (n, n), 0)
    c2 = lax.broadcasted_iota(jnp.int32, (n, n), 1)
    before = {False: r2 > c2, True: r2 < c2}
    upto = {False: r2 >= c2, True: r2 <= c2}
    eye = jnp.where(r2 == c2, 1.0, 0.0)
    first = lax.broadcasted_iota(jnp.int32, (c, LANES), 1) < RW_HEAD_DIM
    stack = lambda x: jnp.concatenate([jnp.where(first, x, 0.0), jnp.where(first, 0.0, x)], axis=0)
    rk = lax.broadcasted_iota(jnp.int32, (LANES, LANES), 0)
    ck = lax.broadcasted_iota(jnp.int32, (LANES, LANES), 1)

    seen3 = {v_: jnp.concatenate([bf(m)] * 3, axis=1) for v_, m in seen.items()}
    cum = [_dot(seen3[ch[8]], jnp.concatenate(_bf16_pieces(ch[4], 3), axis=0)) for ch in chains]
    e_neg = each(lambda q: jnp.exp(-q), cum)
    e_end = [jnp.exp(q[0:1] if ch[8] else q[c - 1:c]) for q, ch in zip(cum, chains)]
    e_end_col = each(lambda e: jnp.sum(jnp.where(rk == ck, e, 0.0), axis=1, keepdims=True), e_end)
    rs = [stack(ch[0] * jnp.exp(q)) for ch, q in zip(chains, cum)]
    as_ = [stack(-ch[3] * jnp.exp(q - ch[4])) for ch, q in zip(chains, cum)]
    bs = [stack(ch[3] * ch[5] * en) for ch, en in zip(chains, e_neg)]
    ks = [stack(ch[1] * (1.0 + (ch[5] - 1.0) * ch[6]) * en) for ch, en in zip(chains, e_neg)]
    vs = [stack(ch[2]) for ch in chains]
    h0 = [ch[7] for ch in chains]

    pair = each(lambda a, b, c_, d: _dot_nt(cat0(a, b), cat0(c_, d)), rs, as_, bs, ks)
    m_rb = [jnp.where(upto[v_], p[:n, :n], 0.0) for p, v_ in zip(pair, rev)]
    m_rk = [jnp.where(upto[v_], p[:n, n:], 0.0) for p, v_ in zip(pair, rev)]
    l_ab = [jnp.where(before[v_], p[n:, :n], 0.0) for p, v_ in zip(pair, rev)]
    m_ak = [jnp.where(before[v_], p[n:, n:], 0.0) for p, v_ in zip(pair, rev)]

    inv = each(lambda l: eye + l, l_ab)
    power = each(lambda l: _dot(bf(l), bf(l)), l_ab)
    steps = int(math.log2(c)) - 1
    for it in range(steps):
        if it < steps - 1:
            both = each(lambda i, p: _dot(cat0(i, p), bf(p)), inv, power)
            inv = each(lambda i, b: i + b[:n], inv, both)
            power = each(lambda b: b[n:], both)
        else:
            inv = each(lambda i, p: i + _dot(bf(i), bf(p)), inv, power)

    state_and_v = each(cat0, h0, vs)
    w = each(lambda a, m, sv: _dot(cat1(a, m), sv), as_, m_ak, state_and_v)
    y0 = each(lambda r_, m, sv: _dot(cat1(r_, m), sv), rs, m_rk, state_and_v)
    u = each(lambda i, w_: _dot(bf(i), bf(w_)), inv, w)
    y = each(lambda y_, m, u_: y_ + _dot(bf(m), bf(u_)), y0, m_rb, u)
    decayed = each(lambda b, k_, e: jnp.concatenate([b * e, k_ * e], axis=0).T, bs, ks, e_end)
    h_new = each(lambda h, ec, dc, u_, v_: h * ec + _dot(bf(dc), cat0(u_, v_)), h0, e_end_col, decayed, u, vs)
    return [q[:c] + q[c:] for q in y], h_new


def _wkv_kernel(rf_ref, kf_ref, vf_ref, kkf_ref, lwf_ref, arf_ref, rb_ref, kb_ref, vb_ref, kkb_ref, lwb_ref,
                arb_ref, ka_ref, yf_ref, yb_ref, h_ref, *, pairs):
    @pl.when(pl.program_id(2) == 0)
    def _():
        h_ref[...] = jnp.zeros_like(h_ref)

    dirs = ((rf_ref, kf_ref, vf_ref, kkf_ref, lwf_ref, arf_ref, yf_ref),
            (rb_ref, kb_ref, vb_ref, kkb_ref, lwb_ref, arb_ref, yb_ref))
    chains, outs = [], []
    for d, (r_ref, k_ref, v_ref, kk_ref, lw_ref, ar_ref, y_ref) in enumerate(dirs):
        for p in range(pairs):
            lanes = slice(p * LANES, (p + 1) * LANES)
            chains.append((r_ref[0, :, lanes].astype(F32), k_ref[0, :, lanes].astype(F32),
                           v_ref[0, :, lanes].astype(F32), kk_ref[0, :, lanes].astype(F32),
                           lw_ref[0, :, lanes], ar_ref[0, :, lanes], ka_ref[:, lanes], h_ref[d, p], d == 1))
            outs.append((y_ref, lanes, d, p))
    ys, hs = _wkv_chunks(chains)
    for (y_ref, lanes, d, p), y, h_new in zip(outs, ys, hs):
        y_ref[0, :, lanes] = y.astype(y_ref.dtype)
        h_ref[d, p] = h_new


def wkv7(r, k, v, kk, lw, ar, k_a, n_ctx):
    bsz, t, width = r.shape
    c = WKV_CHUNK
    n_cc, n_chunks = n_ctx // c, t // c
    pairs = WKV_PAIRS
    wb = pairs * LANES
    fwd = pl.BlockSpec((1, c, wb), lambda i, p, s: (i, s, p))
    bwd = pl.BlockSpec((1, c, wb), lambda i, p, s: (i, _mirrored_chunk(s, n_cc, n_chunks), p))
    return pl.pallas_call(
        functools.partial(_wkv_kernel, pairs=pairs),
        grid=(bsz, width // wb, n_chunks),
        in_specs=[fwd] * 6 + [bwd] * 6 + [pl.BlockSpec((1, wb), lambda i, p, s: (0, p))],
        out_specs=[fwd, bwd],
        out_shape=[jax.ShapeDtypeStruct((bsz, t, width), ACT)] * 2,
        scratch_shapes=[pltpu.VMEM((2, pairs, LANES, LANES), F32)],
        compiler_params=pltpu.CompilerParams(dimension_semantics=("parallel", "parallel", "arbitrary"),
                                             vmem_limit_bytes=VMEM_LIMIT),
        name="wkv7",
    )(r, k, v, kk, lw[0], ar[0], r, k, v, kk, lw[1], ar[1], k_a.reshape(1, width))


def _conv_kernel(x_ref, before_ref, after_ref, w_ref, b_ref, o_ref, *, tiles_per_sample, ctx_tiles):
    half = SSM_CONV // 2
    others = [o for o in range(SSM_CONV) if o != half]
    x, taps = _segment_taps(x_ref, before_ref, after_ref, [o - half for o in others], tiles_per_sample, ctx_tiles)
    acc = x * w_ref[half:half + 1] + b_ref[...]
    for o, tap in zip(others, taps):
        acc = acc + tap * w_ref[o:o + 1]
    o_ref[...] = jax.nn.silu(acc).astype(o_ref.dtype)


def conv_silu(p, col0, conv_w, conv_b, n_ctx):
    bsz, t, _ = p.shape
    taps, ch = conv_w.shape
    tm = _seg_tile(n_ctx, t - n_ctx, 256)
    tps = t // tm
    m = bsz * t
    p2 = p.reshape(m, -1)
    before, after = _halo_specs(tm, m, ch, col0 // ch)
    w_pad = jnp.concatenate([conv_w, jnp.zeros((-taps % 8, ch), F32)], axis=0)
    const = lambda a: pl.BlockSpec(a.shape, lambda i: (0,) * a.ndim)
    out = pl.pallas_call(
        functools.partial(_conv_kernel, tiles_per_sample=tps, ctx_tiles=n_ctx // tm),
        grid=(m // tm,),
        in_specs=[pl.BlockSpec((tm, ch), lambda i: (i, col0 // ch)), before, after, const(w_pad),
                  pl.BlockSpec((1, ch), lambda i: (0, 0))],
        out_specs=pl.BlockSpec((tm, ch), lambda i: (i, 0)),
        out_shape=jax.ShapeDtypeStruct((m, ch), ACT),
        compiler_params=pltpu.CompilerParams(dimension_semantics=("parallel",), vmem_limit_bytes=VMEM_LIMIT),
        name="conv_silu",
    )(p2, p2, p2, w_pad, conv_b.reshape(1, ch))
    return out.reshape(bsz, t, ch)


def _ssd_kernel(xf_ref, dtf_ref, xb_ref, dtb_ref, bias_ref, arow_ref, tabw_ref, tabx_ref, yf_ref, yb_ref,
                s_ref, ct_ref, cp_ref, *, inner, heads_per_group):
    L, hp = SSD_CHUNK, SSM_HEAD_DIM
    e_heads = heads_per_group
    width = e_heads * hp
    gn = SSM_GROUPS * SSM_STATE
    n_heads = SSM_GROUPS * e_heads

    @pl.when(pl.program_id(1) == 0)
    def _():
        s_ref[...] = jnp.zeros_like(s_ref)

    rl = lax.broadcasted_iota(jnp.int32, (L, L), 0)
    cl = lax.broadcasted_iota(jnp.int32, (L, L), 1)
    upto = (rl >= cl, rl <= cl)
    dirs = ((xf_ref, dtf_ref, yf_ref), (xb_ref, dtb_ref, yb_ref))
    for d, (_, dt_ref, _) in enumerate(dirs):
        dt_all = jax.nn.softplus(dt_ref[0].astype(F32) + bias_ref[...])
        cum = _dot(upto[d].astype(F32), dt_all * arow_ref[...], HIGHEST)
        ct_ref[d] = cum.T
        cp_ref[d] = jnp.concatenate(_bf16_pieces(cum, 3) + _bf16_pieces(dt_all, 2), axis=1)

    lane = lax.broadcasted_iota(jnp.int32, (L, LANES), 1)

    def group(g, carry):
        two = range(2)
        xcol = pl.ds(pl.multiple_of(g * width, width), width)
        bcol = pl.ds(pl.multiple_of(inner + g * SSM_STATE, SSM_STATE), SSM_STATE)
        ccol = pl.ds(pl.multiple_of(inner + gn + g * SSM_STATE, SSM_STATE), SSM_STATE)
        x = [dirs[d][0][0, :, xcol].astype(F32) for d in two]
        bm = [dirs[d][0][0, :, bcol].astype(F32) for d in two]
        cmb = [dirs[d][0][0, :, ccol].astype(BF16) for d in two]
        tabw = [tabw_ref[d, g] for d in two]
        tabx = [tabx_ref[d, g] for d in two]
        cum_w = [_dot(cp_ref[d, :, :3 * LANES], tabw[d]) for d in two]
        dt_x = [_dot(cp_ref[d, :, 3 * LANES:], tabx[d]) for d in two]
        s0 = [s_ref[d, g] for d in two]
        cb = [_dot_nt(cmb[d], bm[d].astype(BF16)) for d in two]
        y_off = [_dot(cmb[d], s0[d].astype(BF16)) for d in two]
        bt = [bm[d].T.astype(BF16) for d in two]
        cum_x = [jnp.concatenate(
            [jnp.where(lane < hp, cum_w[d][:, (2 * q) * LANES:(2 * q + 1) * LANES],
                       cum_w[d][:, (2 * q + 1) * LANES:(2 * q + 2) * LANES]) for q in range(e_heads // 2)],
            axis=1) for d in two]
        tot_x = [cum_x[d][0:1] if d == 1 else cum_x[d][L - 1:L] for d in two]
        xdt = [x[d] * dt_x[d] for d in two]
        cbm = [jnp.where(upto[d], cb[d], 0.0) for d in two]
        lhs, rhs = [], []
        for d in two:
            for q in range(e_heads // 2):
                mats = []
                for e in (2 * q, 2 * q + 1):
                    row = d * n_heads + g * e_heads + e
                    seg = cum_w[d][:, e * LANES:(e + 1) * LANES] - ct_ref[d, pl.ds(row, 1), :]
                    mats.append((cbm[d] * jnp.exp(jnp.minimum(seg, 0.0))).astype(BF16))
                slab = xdt[d][:, q * LANES:(q + 1) * LANES]
                lhs.append(jnp.concatenate(mats, axis=1))
                rhs.append(jnp.concatenate([jnp.where(lane < hp, slab, 0.0), jnp.where(lane < hp, 0.0, slab)],
                                           axis=0).astype(BF16))
        y_diag = [_dot(a, b) for a, b in zip(lhs, rhs)]
        s_add = [_dot(bt[d], (xdt[d] * jnp.exp(tot_x[d] - cum_x[d])).astype(BF16)) for d in two]
        for d in two:
            slabs = y_diag[d * (e_heads // 2):(d + 1) * (e_heads // 2)]
            y = y_off[d] * jnp.exp(cum_x[d]) + jnp.concatenate(slabs, axis=1)
            dirs[d][2][0, :, xcol] = y.astype(dirs[d][2].dtype)
            s_ref[d, g] = s0[d] * jnp.exp(tot_x[d]) + s_add[d]
        return carry

    lax.fori_loop(0, SSM_GROUPS, group, 0, unroll=2)


def ssd(xbc, p_c, dt_col, dt_bias, a_log, n_ctx, inner):
    bsz, t, xw = xbc.shape
    L = SSD_CHUNK
    n_heads = dt_bias.shape[1]
    e_heads = n_heads // SSM_GROUPS
    width = e_heads * SSM_HEAD_DIM
    n_cc, n_chunks = n_ctx // L, t // L
    pad = LANES - 2 * n_heads
    bias = jnp.concatenate([dt_bias[0], dt_bias[1], jnp.zeros((pad,), F32)]).reshape(1, LANES)
    arow = jnp.concatenate([-jnp.exp(a_log[0]), -jnp.exp(a_log[1]), jnp.zeros((pad,), F32)]).reshape(1, LANES)
    head_row = (jnp.arange(2)[:, None, None, None] * n_heads + jnp.arange(SSM_GROUPS)[None, :, None, None] * e_heads)
    src = jnp.arange(LANES)[None, None, :, None]
    tabw = (src == head_row + jnp.arange(e_heads * LANES)[None, None, None, :] // LANES).astype(BF16)
    tabx = (src == head_row + jnp.arange(width)[None, None, None, :] // SSM_HEAD_DIM).astype(BF16)
    tabw = jnp.concatenate([tabw] * 3, axis=2)
    tabx = jnp.concatenate([tabx] * 2, axis=2)

    mirrored = lambda s: _mirrored_chunk(s, n_cc, n_chunks)
    const = lambda a: pl.BlockSpec(a.shape, lambda i, s: (0,) * a.ndim)
    return pl.pallas_call(
        functools.partial(_ssd_kernel, inner=inner, heads_per_group=e_heads),
        grid=(bsz, n_chunks),
        in_specs=[
            pl.BlockSpec((1, L, xw), lambda i, s: (i, s, 0)),
            pl.BlockSpec((1, L, LANES), lambda i, s: (i, s, dt_col // LANES)),
            pl.BlockSpec((1, L, xw), lambda i, s: (i, mirrored(s), 0)),
            pl.BlockSpec((1, L, LANES), lambda i, s: (i, mirrored(s), dt_col // LANES)),
            const(bias), const(arow), const(tabw), const(tabx),
        ],
        out_specs=[pl.BlockSpec((1, L, inner), lambda i, s: (i, s, 0)),
                   pl.BlockSpec((1, L, inner), lambda i, s: (i, mirrored(s), 0))],
        out_shape=[jax.ShapeDtypeStruct((bsz, t, inner), ACT)] * 2,
        scratch_shapes=[pltpu.VMEM((2, SSM_GROUPS, SSM_STATE, width), F32),
                        pltpu.VMEM((2, LANES, L), F32),
                        pltpu.VMEM((2, L, 5 * LANES), BF16)],
        compiler_params=pltpu.CompilerParams(dimension_semantics=("parallel", "arbitrary"),
                                             vmem_limit_bytes=VMEM_LIMIT),
        name="ssd",
    )(xbc, p_c, xbc, p_c, bias, arow, tabw, tabx)


def _merge_kernel(x_ref, yf_ref, yb_ref, r_ref, k_ref, v_ref, og_ref, z_ref, grw_ref, gssm_ref, gl_ref, gc_ref,
                  lnw_ref, lnb_ref, rk_ref, wrw_ref, wout_ref, o_ref, *, tm, tiles_per_sample, n_ctx):
    f32 = lambda ref: ref[...].astype(F32)
    y = f32(yf_ref) + f32(yb_ref)
    inv_n = 1.0 / RW_HEAD_DIM
    centred = y - _group_sums(y, RW_HEAD_DIM) * inv_n
    var = _group_sums(centred * centred, RW_HEAD_DIM) * inv_n
    y = centred * lax.rsqrt(var + RW_GN_EPS) * lnw_ref[...] + lnb_ref[...]
    y = y + _group_sums(f32(r_ref) * f32(k_ref) * rk_ref[...], RW_HEAD_DIM) * f32(v_ref)
    y_rw = y * og_ref[...]
    t1 = _dot(y_rw.astype(wrw_ref.dtype), wrw_ref[...])
    merged = jax.nn.sigmoid(f32(grw_ref)) * t1 + jax.nn.sigmoid(f32(gssm_ref)) * f32(z_ref)
    mix = _dot(merged.astype(wout_ref.dtype), wout_ref[...])
    gate = jnp.where(_ctx_rows(tm, tiles_per_sample, n_ctx), gc_ref[...], gl_ref[0])
    o_ref[...] = x_ref[...] + gate * mix


def merge(x, y_f, y_b, r, k, v, out_gate, z_ssm, proj, col_grw, col_gssm, gate_l, gate_c, ln_w, ln_b, r_k,
          w_rw, w_out, n_ctx):
    bsz, t, d = x.shape
    tm = _row_tile(t, 272)
    tps = t // tm
    m = bsz * t
    rows = lambda blk: pl.BlockSpec((tm, d), lambda i: (i, blk))
    const = lambda shape: pl.BlockSpec(shape, lambda i: (0,) * len(shape))
    flat = lambda a: a.reshape(m, -1)
    out = pl.pallas_call(
        functools.partial(_merge_kernel, tm=tm, tiles_per_sample=tps, n_ctx=n_ctx),
        grid=(m // tm,),
        in_specs=[rows(0)] * 8 + [rows(col_grw // d), rows(col_gssm // d),
                                  pl.BlockSpec((1, 1, d), lambda i: (i // tps, 0, 0)), const((1, d)),
                                  const((1, d)), const((1, d)), const((1, d)),
                                  const(w_rw.shape), const(w_out.shape)],
        out_specs=rows(0),
        out_shape=jax.ShapeDtypeStruct((m, d), F32),
        compiler_params=pltpu.CompilerParams(dimension_semantics=("parallel",), vmem_limit_bytes=VMEM_LIMIT),
        name="merge",
    )(flat(x), flat(y_f), flat(y_b), flat(r), flat(k), flat(v), flat(out_gate), flat(z_ssm), flat(proj), flat(proj),
      gate_l.reshape(bsz, 1, d), gate_c.reshape(1, d), ln_w.reshape(1, d), ln_b.reshape(1, d), r_k.reshape(1, d),
      w_rw, w_out)
    return out.reshape(bsz, t, d)


def _ssm_out_kernel(yf_ref, yb_ref, xs_ref, z_ref, dskip_ref, nw_ref, w_ref, o_ref, *, group):
    f32 = lambda ref: ref[...].astype(F32)
    y = (f32(yf_ref) + f32(yb_ref) + dskip_ref[...] * f32(xs_ref)) * jax.nn.silu(f32(z_ref))
    ms = _group_sums(y * y, group) * (1.0 / group)
    y = y * lax.rsqrt(ms + SSM_NORM_EPS) * nw_ref[...]
    o_ref[...] = _dot(y.astype(w_ref.dtype), w_ref[...]).astype(o_ref.dtype)


def ssm_out(y_f, y_b, xbc, p_c, z_col, d_skip, norm_w, w):
    bsz, t, inner = y_f.shape
    m = bsz * t
    d = w.shape[1]
    tm = _row_tile(t, 272)
    rows = lambda blk: pl.BlockSpec((tm, inner), lambda i: (i, blk))
    const = lambda shape: pl.BlockSpec(shape, lambda i: (0,) * len(shape))
    flat = lambda a: a.reshape(m, -1)
    out = pl.pallas_call(
        functools.partial(_ssm_out_kernel, group=inner // SSM_GROUPS),
        grid=(m // tm,),
        in_specs=[rows(0), rows(0), rows(0), rows(z_col // inner), const((1, inner)), const((1, inner)),
                  const(w.shape)],
        out_specs=pl.BlockSpec((tm, d), lambda i: (i, 0)),
        out_shape=jax.ShapeDtypeStruct((m, d), ACT),
        compiler_params=pltpu.CompilerParams(dimension_semantics=("parallel",), vmem_limit_bytes=VMEM_LIMIT),
        name="ssm_out",
    )(flat(y_f), flat(y_b), flat(xbc), flat(p_c), d_skip.reshape(1, inner), norm_w.reshape(1, inner), w)
    return out.reshape(bsz, t, d)


def _router_kernel(x_ref, g_ref, scl_ref, shl_ref, scc_ref, shc_ref, wr_ref, h_ref, logit_ref, *,
                   tm, tiles_per_sample, n_ctx):
    is_ctx = _ctx_rows(tm, tiles_per_sample, n_ctx)
    h = _modulated(x_ref[...], g_ref[...], scl_ref[0], shl_ref[0], scc_ref[...], shc_ref[...], is_ctx)
    bits = pltpu.bitcast(h.astype(BF16).astype(F32), jnp.uint32)
    half = h.shape[1] // 2
    h_ref[...] = (bits[:, :half] >> 16) | (bits[:, half:] & jnp.uint32(0xFFFF0000))
    logit_ref[...] = _dot(h, wr_ref[...], HIGHEST)


def router(x, gain, scale_l, shift_l, scale_c, shift_c, w_router_padded, n_ctx):
    bsz, t, d = x.shape
    tm = _row_tile(t, 544)
    tps = t // tm
    m = bsz * t
    row = lambda v: v.reshape(1, d)
    const = lambda shape: pl.BlockSpec(shape, lambda i: (0,) * len(shape))
    per_sample = pl.BlockSpec((1, 1, d), lambda i: (i // tps, 0, 0))
    return pl.pallas_call(
        functools.partial(_router_kernel, tm=tm, tiles_per_sample=tps, n_ctx=n_ctx),
        grid=(m // tm,),
        in_specs=[pl.BlockSpec((tm, d), lambda i: (i, 0)), const((1, d)), per_sample, per_sample,
                  const((1, d)), const((1, d)), const(w_router_padded.shape)],
        out_specs=[pl.BlockSpec((tm, d // 2), lambda i: (i, 0)), pl.BlockSpec((tm, LANES), lambda i: (i, 0))],
        out_shape=[jax.ShapeDtypeStruct((m, d // 2), jnp.uint32), jax.ShapeDtypeStruct((m, LANES), F32)],
        compiler_params=pltpu.CompilerParams(dimension_semantics=("parallel",), vmem_limit_bytes=VMEM_LIMIT),
        name="router",
    )(x.reshape(m, d), row(gain), scale_l.reshape(bsz, 1, d), shift_l.reshape(bsz, 1, d),
      row(scale_c), row(shift_c), w_router_padded)


def _experts_kernel(be_ref, nb_ref, x_ref, w1_ref, w3_ref, w2_ref, o_ref, w1b_ref, w3b_ref, w2b_ref):
    i = pl.program_id(0)

    @pl.when((i == 0) | (be_ref[i] != be_ref[jnp.maximum(i - 1, 0)]))
    def _():
        w1b_ref[...] = w1_ref[0, 0].astype(BF16)
        w3b_ref[...] = w3_ref[0, 0].astype(BF16)
        w2b_ref[...] = w2_ref[0, 0].astype(BF16)

    @pl.when(i < nb_ref[0])
    def _():
        packed = x_ref[...]
        x = jnp.concatenate([pltpu.bitcast(packed << 16, F32),
                             pltpu.bitcast(packed & jnp.uint32(0xFFFF0000), F32)], axis=1).astype(BF16)
        hidden = jax.nn.silu(_dot(x, w1b_ref[...])) * _dot(x, w3b_ref[...])
        o_ref[...] = _dot(hidden.astype(BF16), w2b_ref[...])

    @pl.when(i >= nb_ref[0])
    def _():
        o_ref[...] = jnp.zeros_like(o_ref)


def experts(xb, block_expert, n_used, w1, w3, w2, layer):
    n_rows = xb.shape[0]
    d, de = w1.shape[2:]
    rows = MOE_ROWS
    n_blocks = n_rows // rows
    return pl.pallas_call(
        _experts_kernel,
        grid_spec=pltpu.PrefetchScalarGridSpec(
            num_scalar_prefetch=2,
            grid=(n_blocks,),
            in_specs=[
                pl.BlockSpec((rows, d // 2), lambda i, be, nb: (i, 0)),
                pl.BlockSpec((1, 1, d, de), lambda i, be, nb: (layer, be[i], 0, 0)),
                pl.BlockSpec((1, 1, d, de), lambda i, be, nb: (layer, be[i], 0, 0)),
                pl.BlockSpec((1, 1, de, d), lambda i, be, nb: (layer, be[i], 0, 0)),
            ],
            out_specs=pl.BlockSpec((rows, d), lambda i, be, nb: (i, 0)),
            scratch_shapes=[pltpu.VMEM((d, de), BF16), pltpu.VMEM((d, de), BF16), pltpu.VMEM((de, d), BF16)],
        ),
        out_shape=jax.ShapeDtypeStruct((n_rows, d), F32),
        compiler_params=pltpu.CompilerParams(dimension_semantics=("arbitrary",), vmem_limit_bytes=VMEM_LIMIT),
        name="experts",
    )(block_expert, n_used, xb, w1, w3, w2)


def _combine_kernel(x_ref, y_ref, gate_ref, gl_ref, gc_ref, fin_ref, o_ref, *, tm, tiles_per_sample, n_ctx, final):
    d = x_ref.shape[1]
    g = gate_ref[...]
    f = y_ref[:, :d] * g[:, 0:1] + y_ref[:, d:] * g[:, 1:2]
    mod = jnp.where(_ctx_rows(tm, tiles_per_sample, n_ctx), gc_ref[...], gl_ref[0])
    out = x_ref[...] + mod * f
    if final:
        out = out * lax.rsqrt(jnp.mean(out * out, axis=-1, keepdims=True) + NORM_EPS) * fin_ref[...]
    o_ref[...] = out


def combine(x, y_pairs, gates, gate_l, gate_c, final_gain, n_ctx, final):
    bsz, t, d = x.shape
    m = bsz * t
    tm = _row_tile(t, 544)
    tps = t // tm
    const = lambda shape: pl.BlockSpec(shape, lambda i: (0,) * len(shape))
    out = pl.pallas_call(
        functools.partial(_combine_kernel, tm=tm, tiles_per_sample=tps, n_ctx=n_ctx, final=final),
        grid=(m // tm,),
        in_specs=[pl.BlockSpec((tm, d), lambda i: (i, 0)), pl.BlockSpec((tm, TOP_K * d), lambda i: (i, 0)),
                  pl.BlockSpec((tm, LANES), lambda i: (i, 0)), pl.BlockSpec((1, 1, d), lambda i: (i // tps, 0, 0)),
                  const((1, d)), const((1, d))],
        out_specs=pl.BlockSpec((tm, d), lambda i: (i, 0)),
        out_shape=jax.ShapeDtypeStruct((m, d), F32),
        compiler_params=pltpu.CompilerParams(dimension_semantics=("parallel",), vmem_limit_bytes=VMEM_LIMIT),
        name="combine",
    )(x.reshape(m, d), y_pairs.reshape(m, TOP_K * d), gates, gate_l.reshape(bsz, 1, d), gate_c.reshape(1, d),
      final_gain.reshape(1, d))
    return out.reshape(bsz, t, d)


def _top2(vals):
    idx = jnp.arange(vals.shape[-1], dtype=jnp.int32)
    i1 = jnp.argmax(vals, axis=-1).astype(jnp.int32)
    v1 = jnp.max(vals, axis=-1)
    rest = jnp.where(idx == i1[..., None], -jnp.inf, vals)
    i2 = jnp.argmax(rest, axis=-1).astype(jnp.int32)
    v2 = jnp.max(rest, axis=-1)
    return v1, i1, v2, i2


def moe(h, logits, b_router, w1, w3, w2, layer):
    n_tok, d = h.shape
    n_exp = b_router.shape[0]
    epg = n_exp // N_EXPERT_GROUPS
    scores = jax.nn.sigmoid(logits)
    biased = (scores + b_router.astype(F32)).reshape(n_tok, N_EXPERT_GROUPS, epg)
    g1, _, g2, _ = _top2(biased)
    top_group = jnp.argmax(g1 + g2, axis=-1).astype(jnp.int32)
    in_top = jnp.arange(N_EXPERT_GROUPS, dtype=jnp.int32)[None, :, None] == top_group[:, None, None]
    in_group = jnp.sum(jnp.where(in_top, biased, 0.0), axis=1)
    _, l1, _, l2 = _top2(in_group)
    expert = top_group[:, None] * epg + jnp.stack([l1, l2], axis=-1)
    picked = expert[:, :, None] == jnp.arange(n_exp, dtype=jnp.int32)[None, None, :]
    gate = jnp.sum(jnp.where(picked, scores[:, None, :], 0.0), axis=-1)
    gate = gate / jnp.sum(gate, axis=-1, keepdims=True)

    n_assign = n_tok * TOP_K
    flat_e = expert.reshape(-1).astype(jnp.int32)
    onehot = picked.reshape(n_assign, n_exp).astype(jnp.int32)
    rank = jnp.sum((jnp.cumsum(onehot, axis=0) - onehot) * onehot, axis=-1)
    counts = jnp.sum(onehot, axis=0)
    padded = (counts + MOE_ROWS - 1) // MOE_ROWS * MOE_ROWS
    pad_end = jnp.cumsum(padded)
    dest = ((pad_end - padded)[flat_e] + rank).astype(jnp.int32)
    n_blocks = -(-n_assign // MOE_ROWS) + n_exp
    slot_token = jnp.zeros((n_blocks * MOE_ROWS,), jnp.int32).at[dest].set(
        jnp.arange(n_assign, dtype=jnp.int32) // TOP_K)
    block_expert = jnp.clip(
        jnp.sum(jnp.arange(n_blocks, dtype=jnp.int32)[:, None] >= (pad_end // MOE_ROWS)[None, :], axis=-1),
        0, n_exp - 1).astype(jnp.int32)
    n_used = (pad_end[-1:] // MOE_ROWS).astype(jnp.int32)
    yb = experts(h[slot_token], block_expert, n_used, w1, w3, w2, layer)
    return yb[dest], jnp.pad(gate, ((0, 0), (0, LANES - TOP_K)))


def _ssm_branch(p, prm, w_branch, n_ctx):
    inner = prm["norm_w"].shape[0]
    xbc_w = prm["conv_w"].shape[1]
    xbc = conv_silu(p, 0, prm["conv_w"], prm["conv_b"], n_ctx)
    y_f, y_b = ssd(xbc, p, xbc_w + inner, prm["dt_bias"], prm["a_log"], n_ctx, inner)
    d_skip = jnp.repeat(prm["d"][0] + prm["d"][1], SSM_HEAD_DIM)
    return ssm_out(y_f, y_b, xbc, p, xbc_w, d_skip, prm["norm_w"], w_branch)


def _pad_cols(w, n):
    return jnp.pad(w, ((0, 0), (0, n - w.shape[1])))


def kernel(x, c, ctx, c_ctx, w_mod, b_mod, norm_mix_g, w_in, rw_shift_mu, rw_w0, rw_w2, rw_a0, rw_a2, rw_g2,
           rw_k_k, rw_k_a, rw_r_k, rw_ln_w, rw_ln_b, ssm_conv_w, ssm_conv_b, ssm_dt_bias, ssm_a_log, ssm_d,
           ssm_norm_w, w_branch_rw, w_branch_ssm, w_out, norm_ffn_g, w_router, b_router, exp_w1, exp_w3, exp_w2,
           norm_final_g):
    bsz, n_lat, d = x.shape
    depth = w_in.shape[0]
    n_ctx = ctx.shape[1]
    rows = n_lat // GRID_W
    t = n_ctx + n_lat
    width = RW_HEADS * RW_HEAD_DIM
    dl, al, gl = rw_w2.shape[2], rw_a2.shape[2], rw_g2.shape[1]
    inner = ssm_norm_w.shape[1]
    xbc_w = ssm_conv_w.shape[2]
    n_heads = ssm_dt_bias.shape[2]
    n_exp = w_router.shape[1]

    def to_c(a):
        ch = a.shape[-1]
        lat = a[:, n_ctx:].reshape(bsz, rows, GRID_W, ch).transpose(0, 2, 1, 3).reshape(bsz, n_lat, ch)
        return jnp.concatenate([a[:, :n_ctx], lat], axis=1)

    def to_r(a):
        ch = a.shape[-1]
        lat = a[:, n_ctx:].reshape(bsz, GRID_W, rows, ch).transpose(0, 2, 1, 3).reshape(bsz, n_lat, ch)
        return jnp.concatenate([a[:, :n_ctx], lat], axis=1)

    o = 0
    src = {}
    for name, size in (("r", width), ("w_f", dl), ("w_b", dl), ("k", width), ("v", width), ("a_f", al),
                       ("a_b", al), ("g", gl), ("z", inner), ("xbc", xbc_w), ("dt", 2 * n_heads),
                       ("g_rw", d), ("g_ssm", d)):
        src[name] = (o, size)
        o += size
    take = lambda a, name: a[..., src[name][0]:src[name][0] + src[name][1]]
    lora_w = 2 * dl + 2 * al
    lora_pad = -(-lora_w // LANES) * LANES
    g_pad = -(-gl // LANES) * LANES
    dt_pad = -(-2 * n_heads // LANES) * LANES
    lay = {"r": (0, width), "k": (width, width), "v": (2 * width, width), "g_rw": (3 * width, d),
           "g_ssm": (3 * width + d, d), "lora": (3 * width + 2 * d, lora_w),
           "g": (3 * width + 2 * d + lora_pad, gl)}

    silu_c = jax.nn.silu(c)
    silu_cc = jax.nn.silu(c_ctx)[None, :]
    act = jnp.concatenate([silu_c, silu_cc, jnp.zeros((-(bsz + 1) % 8, d), F32)], axis=0)
    w_router_p = _pad_cols(w_router.astype(F32), LANES)

    xr = jnp.concatenate([ctx, x], axis=1)
    for l in range(depth):
        mod = matmul(act, w_mod[l], precision=HIGHEST, name="modulation")[:bsz + 1] + b_mod[l]
        shift_m, scale_m, gate_m, shift_f, scale_f, gate_f = jnp.split(mod[:bsz], 6, axis=-1)
        cshift_m, cscale_m, cgate_m, cshift_f, cscale_f, cgate_f = jnp.split(mod[bsz], 6, axis=-1)

        wl = w_in[l]
        w_r = jnp.concatenate(
            [take(wl, "r"), take(wl, "k"), take(wl, "v"), take(wl, "g_rw"), take(wl, "g_ssm"),
             _pad_cols(jnp.concatenate([take(wl, n) for n in ("w_f", "w_b", "a_f", "a_b")], axis=1), lora_pad),
             _pad_cols(take(wl, "g"), g_pad)], axis=1).astype(BF16)
        w_c = jnp.concatenate([take(wl, "xbc"), take(wl, "z"), _pad_cols(take(wl, "dt"), dt_pad)],
                              axis=1).astype(BF16)
        mods = (norm_mix_g[l], scale_m, shift_m, cscale_m, cshift_m)
        h_mix = mod_norm(xr, *mods, n_ctx)
        p_r = project(h_mix, w_r, ACT)
        p_c = project(to_c(h_mix), w_c, ACT)

        mu = rw_shift_mu[l]
        rw_prm = dict(
            shift_mu={"r": take(mu, "r"), "k": take(mu, "k"), "v": take(mu, "v"), "g": take(mu, "g"),
                      "lora": jnp.concatenate([take(mu, n) for n in ("w_f", "w_b", "a_f", "a_b")])},
            w0=rw_w0[l], w2=rw_w2[l], a0=rw_a0[l], a2=rw_a2[l], g2=rw_g2[l],
            k_k=rw_k_k[l], k_a=rw_k_a[l], r_k=rw_r_k[l], ln_w=rw_ln_w[l], ln_b=rw_ln_b[l])
        r, k, v, kk, lw_f, lw_b, ar_f, ar_b, out_gate = rwkv_prep(p_r, lay, rw_prm, n_ctx)
        y_f, y_b = wkv7(r, k, v, kk, (lw_f, lw_b), (ar_f, ar_b), rw_k_a[l], n_ctx)
        ssm_prm = dict(conv_w=ssm_conv_w[l], conv_b=ssm_conv_b[l], dt_bias=ssm_dt_bias[l], a_log=ssm_a_log[l],
                       d=ssm_d[l], norm_w=ssm_norm_w[l])
        z_ssm = to_r(_ssm_branch(p_c, ssm_prm, w_branch_ssm[l].astype(BF16), n_ctx))
        xr = merge(xr, y_f, y_b, r, k, v, out_gate, z_ssm, p_r, lay["g_rw"][0], lay["g_ssm"][0], gate_m, cgate_m,
                   rw_ln_w[l], rw_ln_b[l], rw_r_k[l], w_branch_rw[l].astype(BF16), w_out[l].astype(BF16), n_ctx)

        h, logits = router(xr, norm_ffn_g[l], scale_f, shift_f, cscale_f, cshift_f, w_router_p, n_ctx)
        y_pairs, gates = moe(h, logits[:, :n_exp], b_router, exp_w1, exp_w3, exp_w2, l)
        xr = combine(xr, y_pairs, gates, gate_f, cgate_f, norm_final_g, n_ctx, final=(l == depth - 1))
    return xr[:, n_ctx:]
```

```python
import functools
import math

import jax
import jax.numpy as jnp
from jax import lax
from jax.experimental import pallas as pl
from jax.experimental.pallas import tpu as pltpu

F32 = jnp.float32
BF16 = jnp.bfloat16
HIGHEST = lax.Precision.HIGHEST

GRID_W = 64
RW_HEADS = 16
RW_HEAD_DIM = 64
RW_GN_EPS = 64e-5
SSM_HEAD_DIM = 64
SSM_GROUPS = 8
SSM_STATE = 128
SSM_CONV = 5
SSM_NORM_EPS = 1e-5
N_EXPERT_GROUPS = 4
TOP_K = 2
NORM_EPS = 1e-6

LANES = 128
WKV_CHUNK = 64
WKV_PAIRS = 8
SSD_CHUNK = 128
MOE_ROWS = 256
VMEM_LIMIT = 56 * 1024 * 1024


def _dot(a, b, precision=None):
    return jnp.dot(a, b, preferred_element_type=F32, precision=precision)


def _dot_nt(a, b, precision=None):
    return lax.dot_general(a, b, (((1,), (1,)), ((), ())), preferred_element_type=F32, precision=precision)


def _dot_tn(a, b, precision=None):
    return lax.dot_general(a, b, (((0,), (0,)), ((), ())), preferred_element_type=F32, precision=precision)


def _row_tile(n_rows_per_sample, limit=1088):
    for tm in (1088, 544, 512, 272, 256, 128, 64, 32, 16):
        if tm <= limit and n_rows_per_sample % tm == 0:
            return tm
    raise ValueError(n_rows_per_sample)


def _col_tile(n_cols, limit=1536):
    best = LANES
    for k in range(1, n_cols // LANES + 1):
        tn = k * LANES
        if n_cols % tn == 0 and tn <= limit:
            best = tn
    return best


def _bf16_pieces(x, n):
    pieces = []
    for _ in range(n):
        p = x.astype(BF16)
        pieces.append(p)
        x = x - p.astype(F32)
    return pieces


def _group_sums(x, group):
    span = max(group, LANES)
    rr = lax.broadcasted_iota(jnp.int32, (span, span), 0) // group
    cc = lax.broadcasted_iota(jnp.int32, (span, span), 1) // group
    ones = (rr == cc).astype(BF16)
    pieces = _bf16_pieces(x, 3)
    cols = []
    for j in range(x.shape[1] // span):
        sl = slice(j * span, (j + 1) * span)
        cols.append(_dot(pieces[0][:, sl], ones) + _dot(pieces[1][:, sl], ones) + _dot(pieces[2][:, sl], ones))
    return cols[0] if len(cols) == 1 else jnp.concatenate(cols, axis=1)


HALO = 16
ACT = BF16


def _seg_tile(n_ctx, n_lat, limit):
    for tm in (1024, 512, 256, 128, 64, 32, 16):
        if tm <= limit and n_ctx % tm == 0 and n_lat % tm == 0:
            return tm
    raise ValueError((n_ctx, n_lat))


def _segment_taps(x_ref, before_ref, after_ref, offsets, tiles_per_sample, ctx_tiles):
    tm = x_ref.shape[0]
    ti = pl.program_id(0) % tiles_per_sample
    first = (ti == 0) | (ti == ctx_tiles)
    last = (ti == ctx_tiles - 1) | (ti == tiles_per_sample - 1)
    x = x_ref[...].astype(F32)
    before = before_ref[...].astype(F32) * jnp.where(first, 0.0, 1.0)
    after = after_ref[...].astype(F32) * jnp.where(last, 0.0, 1.0)
    ext = jnp.concatenate([before, x, after], axis=0)
    n = tm + 2 * HALO
    return x, [pltpu.roll(ext, (-o) % n, 0)[HALO:HALO + tm] for o in offsets]


def _halo_specs(tm, n_rows, width, col_block):
    last = n_rows // HALO - 1
    return (pl.BlockSpec((HALO, width), lambda i: (jnp.maximum(i * (tm // HALO) - 1, 0), col_block)),
            pl.BlockSpec((HALO, width), lambda i: (jnp.minimum((i + 1) * (tm // HALO), last), col_block)))


def _mirrored_chunk(s, n_ctx_chunks, n_chunks):
    return jnp.where(s < n_ctx_chunks, n_ctx_chunks - 1 - s, n_chunks - 1 + n_ctx_chunks - s)


def _modulated(x, g, sc_l, sh_l, sc_c, sh_c, is_ctx):
    y = x * lax.rsqrt(jnp.mean(x * x, axis=-1, keepdims=True) + NORM_EPS) * g
    return y * (1.0 + jnp.where(is_ctx, sc_c, sc_l)) + jnp.where(is_ctx, sh_c, sh_l)


def _ctx_rows(tm, tiles_per_sample, n_ctx):
    row0 = (pl.program_id(0) % tiles_per_sample) * tm
    return row0 + lax.broadcasted_iota(jnp.int32, (tm, 1), 0) < n_ctx


def _mod_norm_kernel(x_ref, g_ref, scl_ref, shl_ref, scc_ref, shc_ref, h_ref, *, tm, tiles_per_sample, n_ctx):
    is_ctx = _ctx_rows(tm, tiles_per_sample, n_ctx)
    h = _modulated(x_ref[...], g_ref[...], scl_ref[0], shl_ref[0], scc_ref[...], shc_ref[...], is_ctx)
    h_ref[...] = h.astype(h_ref.dtype)


def mod_norm(x, gain, scale_l, shift_l, scale_c, shift_c, n_ctx):
    bsz, t, d = x.shape
    tm = _row_tile(t)
    tps = t // tm
    row = lambda v: v.reshape(1, d)
    const = pl.BlockSpec((1, d), lambda i: (0, 0))
    per_sample = pl.BlockSpec((1, 1, d), lambda i: (i // tps, 0, 0))
    out = pl.pallas_call(
        functools.partial(_mod_norm_kernel, tm=tm, tiles_per_sample=tps, n_ctx=n_ctx),
        grid=(bsz * tps,),
        in_specs=[pl.BlockSpec((tm, d), lambda i: (i, 0)), const, per_sample, per_sample, const, const],
        out_specs=pl.BlockSpec((tm, d), lambda i: (i, 0)),
        out_shape=jax.ShapeDtypeStruct((bsz * t, d), BF16),
        compiler_params=pltpu.CompilerParams(dimension_semantics=("parallel",), vmem_limit_bytes=VMEM_LIMIT),
        name="mod_norm",
    )(x.reshape(bsz * t, d), row(gain), scale_l.reshape(bsz, 1, d), shift_l.reshape(bsz, 1, d),
      row(scale_c), row(shift_c))
    return out.reshape(bsz, t, d)


def _project_kernel(h_ref, w_ref, o_ref):
    o_ref[...] = _dot(h_ref[...], w_ref[...]).astype(o_ref.dtype)


def project(h, w, out_dtype):
    bsz, t, d = h.shape
    n = w.shape[1]
    m = bsz * t
    tm = _row_tile(t, 544)
    tn = _col_tile(n, limit=n)
    out = pl.pallas_call(
        _project_kernel,
        grid=(n // tn, m // tm),
        in_specs=[pl.BlockSpec((tm, d), lambda j, i: (i, 0)), pl.BlockSpec((d, tn), lambda j, i: (0, j))],
        out_specs=pl.BlockSpec((tm, tn), lambda j, i: (i, j)),
        out_shape=jax.ShapeDtypeStruct((m, n), out_dtype),
        compiler_params=pltpu.CompilerParams(dimension_semantics=("parallel", "parallel"),
                                             vmem_limit_bytes=VMEM_LIMIT),
        name="project",
    )(h.reshape(m, d), w)
    return out.reshape(bsz, t, n)


def _matmul_kernel(x_ref, w_ref, o_ref, *, act, precision):
    x = x_ref[...]
    if act == "tanh":
        x = jnp.tanh(x)
    elif act == "sigmoid":
        x = jax.nn.sigmoid(x)
    o_ref[...] = _dot(x.astype(w_ref.dtype), w_ref[...], precision).astype(o_ref.dtype)


def matmul(x, w, act=None, precision=None, out_dtype=F32, name="matmul"):
    m, k = x.shape
    n = w.shape[1]
    tm = _row_tile(m) if m >= 16 else m
    tn = _col_tile(n)
    return pl.pallas_call(
        functools.partial(_matmul_kernel, act=act, precision=precision),
        grid=(m // tm, n // tn),
        in_specs=[pl.BlockSpec((tm, k), lambda i, j: (i, 0)), pl.BlockSpec((k, tn), lambda i, j: (0, j))],
        out_specs=pl.BlockSpec((tm, tn), lambda i, j: (i, j)),
        out_shape=jax.ShapeDtypeStruct((m, n), out_dtype),
        compiler_params=pltpu.CompilerParams(dimension_semantics=("parallel", "parallel"),
                                             vmem_limit_bytes=VMEM_LIMIT),
        name=name,
    )(x, w)


def _rwkv_prep_kernel(m_ref, mb_ref, ma_ref, s_ref, sb_ref, sa_ref, mu_m_ref, mu_s_ref, kk_ref, bias_ref,
                      wl_ref, g2_ref, r_ref, k_ref, v_ref, kko_ref, lwf_ref, lwb_ref, arf_ref, arb_ref, og_ref, *,
                      tiles_per_sample, ctx_tiles, width, n_decay):
    def shifted(x_ref, before_ref, after_ref, mu_ref):
        x, (prev, nxt) = _segment_taps(x_ref, before_ref, after_ref, (-1, 1), tiles_per_sample, ctx_tiles)
        return x + mu_ref[...] * (0.5 * (prev + nxt) - x)

    main = shifted(m_ref, mb_ref, ma_ref, mu_m_ref)
    small = shifted(s_ref, sb_ref, sa_ref, mu_s_ref)
    k = main[:, width:2 * width]
    r_ref[...] = main[:, :width].astype(r_ref.dtype)
    k_ref[...] = k.astype(k_ref.dtype)
    v_ref[...] = main[:, 2 * width:].astype(v_ref.dtype)
    kk = k * kk_ref[...]
    kko_ref[...] = (kk * lax.rsqrt(jnp.maximum(_group_sums(kk * kk, RW_HEAD_DIM), 1e-24))).astype(kko_ref.dtype)

    half = small.shape[1] // 2
    lora = small[:, :half]
    lane = lax.broadcasted_iota(jnp.int32, lora.shape, 1)
    heads = _dot(jnp.where(lane < n_decay, jnp.tanh(lora), lora).astype(BF16), wl_ref[...]) + bias_ref[...]
    for d, (lw_ref, ar_ref) in enumerate(((lwf_ref, arf_ref), (lwb_ref, arb_ref))):
        w_pre = heads[:, d * width:(d + 1) * width]
        lw_ref[...] = (-math.exp(-0.5)) * jax.nn.sigmoid(w_pre)
        ar_ref[...] = jax.nn.sigmoid(heads[:, (2 + d) * width:(3 + d) * width])
    og_ref[...] = _dot(jax.nn.sigmoid(small[:, half:]).astype(BF16), g2_ref[...])


def rwkv_prep(p, lay, prm, n_ctx):
    bsz, t, _ = p.shape
    width = RW_HEADS * RW_HEAD_DIM
    m = bsz * t
    tm = _seg_tile(n_ctx, t - n_ctx, 256)
    tps = t // tm
    p2 = p.reshape(m, -1)
    main_w = 3 * width
    small0 = lay["lora"][0]
    small_w = p2.shape[1] - small0
    half = small_w // 2
    dl, al, gl = prm["w2"].shape[1], prm["a2"].shape[1], prm["g2"].shape[0]
    assert lay["r"][0] == 0 and lay["g"][0] == small0 + half and small0 % small_w == 0
    mu = prm["shift_mu"]
    pad1 = lambda a, n: jnp.pad(a, (0, n - a.shape[0]))
    mu_main = jnp.concatenate([mu["r"], mu["k"], mu["v"]]).reshape(1, main_w)
    mu_small = jnp.concatenate([pad1(mu["lora"], half), pad1(mu["g"], half)]).reshape(1, small_w)
    wl = jnp.zeros((half, 4 * width), F32)
    for j, (blk, rows0, nrows) in enumerate(((prm["w2"][0], 0, dl), (prm["w2"][1], dl, dl),
                                             (prm["a2"][0], 2 * dl, al), (prm["a2"][1], 2 * dl + al, al))):
        wl = wl.at[rows0:rows0 + nrows, j * width:(j + 1) * width].set(blk)
    bias = jnp.concatenate([prm["w0"][0], prm["w0"][1], prm["a0"][0], prm["a0"][1]]).reshape(1, 4 * width)
    g2 = jnp.pad(prm["g2"], ((0, half - gl), (0, 0))).astype(BF16)

    mb, ma = _halo_specs(tm, m, main_w, 0)
    sb, sa = _halo_specs(tm, m, small_w, small0 // small_w)
    const = lambda a: pl.BlockSpec(a.shape, lambda i: (0,) * a.ndim)
    out_spec = pl.BlockSpec((tm, width), lambda i: (i, 0))
    outs = pl.pallas_call(
        functools.partial(_rwkv_prep_kernel, tiles_per_sample=tps, ctx_tiles=n_ctx // tm, width=width,
                          n_decay=2 * dl),
        grid=(m // tm,),
        in_specs=[pl.BlockSpec((tm, main_w), lambda i: (i, 0)), mb, ma,
                  pl.BlockSpec((tm, small_w), lambda i: (i, small0 // small_w)), sb, sa,
                  const(mu_main), const(mu_small), pl.BlockSpec((1, width), lambda i: (0, 0)), const(bias),
                  pl.BlockSpec(wl.shape, lambda i: (0, 0)), const(g2)],
        out_specs=[out_spec] * 9,
        out_shape=[jax.ShapeDtypeStruct((m, width), ACT)] * 4 + [jax.ShapeDtypeStruct((m, width), F32)] * 5,
        compiler_params=pltpu.CompilerParams(dimension_semantics=("parallel",), vmem_limit_bytes=VMEM_LIMIT),
        name="rwkv_prep",
    )(p2, p2, p2, p2, p2, p2, mu_main, mu_small, prm["k_k"].reshape(1, width), bias, wl.astype(BF16), g2)
    return [o.reshape(bsz, t, width) for o in outs]


def _wkv_chunks(chains):
    c = chains[0][0].shape[0]
    n = 2 * c
    rev = [ch[8] for ch in chains]
    each = lambda f, *cols: [f(*a) for a in zip(*cols)]
    bf = lambda x: x.astype(BF16)
    cat0 = lambda *xs: jnp.concatenate([bf(x) for x in xs], axis=0)
    cat1 = lambda *xs: jnp.concatenate([bf(x) for x in xs], axis=1)

    rc = lax.broadcasted_iota(jnp.int32, (c, c), 0)
    cc = lax.broadcasted_iota(jnp.int32, (c, c), 1)
    seen = {False: (rc >= cc).astype(F32), True: (rc <= cc).astype(F32)}
    r2 = lax.broadcasted_iota(jnp.int32, (n, n), 0)
    c2 = lax.broadcasted_iota(jnp.int32, (n, n), 1)
    before = {False: r2 > c2, True: r2 < c2}
    upto = {False: r2 >= c2, True: r2 <= c2}
    eye = jnp.where(r2 == c2, 1.0, 0.0)
    first = lax.broadcasted_iota(jnp.int32, (c, LANES), 1) < RW_HEAD_DIM
    stack = lambda x: jnp.concatenate([jnp.where(first, x, 0.0), jnp.where(first, 0.0, x)], axis=0)
    rk = lax.broadcasted_iota(jnp.int32, (LANES, LANES), 0)
    ck = lax.broadcasted_iota(jnp.int32, (LANES, LANES), 1)

    seen3 = {v_: jnp.concatenate([bf(m)] * 3, axis=1) for v_, m in seen.items()}
    cum = [_dot(seen3[ch[8]], jnp.concatenate(_bf16_pieces(ch[4], 3), axis=0)) for ch in chains]
    e_neg = each(lambda q: jnp.exp(-q), cum)
    e_end = [jnp.exp(q[0:1] if ch[8] else q[c - 1:c]) for q, ch in zip(cum, chains)]
    e_end_col = each(lambda e: jnp.sum(jnp.where(rk == ck, e, 0.0), axis=1, keepdims=True), e_end)
    rs = [stack(ch[0] * jnp.exp(q)) for ch, q in zip(chains, cum)]
    as_ = [stack(-ch[3] * jnp.exp(q - ch[4])) for ch, q in zip(chains, cum)]
    bs = [stack(ch[3] * ch[5] * en) for ch, en in zip(chains, e_neg)]
    ks = [stack(ch[1] * (1.0 + (ch[5] - 1.0) * ch[6]) * en) for ch, en in zip(chains, e_neg)]
    vs = [stack(ch[2]) for ch in chains]
    h0 = [ch[7] for ch in chains]

    pair = each(lambda a, b, c_, d: _dot_nt(cat0(a, b), cat0(c_, d)), rs, as_, bs, ks)
    m_rb = [jnp.where(upto[v_], p[:n, :n], 0.0) for p, v_ in zip(pair, rev)]
    m_rk = [jnp.where(upto[v_], p[:n, n:], 0.0) for p, v_ in zip(pair, rev)]
    l_ab = [jnp.where(before[v_], p[n:, :n], 0.0) for p, v_ in zip(pair, rev)]
    m_ak = [jnp.where(before[v_], p[n:, n:], 0.0) for p, v_ in zip(pair, rev)]

    inv = each(lambda l: eye + l, l_ab)
    power = each(lambda l: _dot(bf(l), bf(l)), l_ab)
    steps = int(math.log2(c)) - 1
    for it in range(steps):
        if it < steps - 1:
            both = each(lambda i, p: _dot(cat0(i, p), bf(p)), inv, power)
            inv = each(lambda i, b: i + b[:n], inv, both)
            power = each(lambda b: b[n:], both)
        else:
            inv = each(lambda i, p: i + _dot(bf(i), bf(p)), inv, power)

    state_and_v = each(cat0, h0, vs)
    w = each(lambda a, m, sv: _dot(cat1(a, m), sv), as_, m_ak, state_and_v)
    y0 = each(lambda r_, m, sv: _dot(cat1(r_, m), sv), rs, m_rk, state_and_v)
    u = each(lambda i, w_: _dot(bf(i), bf(w_)), inv, w)
    y = each(lambda y_, m, u_: y_ + _dot(bf(m), bf(u_)), y0, m_rb, u)
    decayed = each(lambda b, k_, e: jnp.concatenate([b * e, k_ * e], axis=0).T, bs, ks, e_end)
    h_new = each(lambda h, ec, dc, u_, v_: h * ec + _dot(bf(dc), cat0(u_, v_)), h0, e_end_col, decayed, u, vs)
    return [q[:c] + q[c:] for q in y], h_new


def _wkv_kernel(rf_ref, kf_ref, vf_ref, kkf_ref, lwf_ref, arf_ref, rb_ref, kb_ref, vb_ref, kkb_ref, lwb_ref,
                arb_ref, ka_ref, yf_ref, yb_ref, h_ref, *, pairs):
    @pl.when(pl.program_id(2) == 0)
    def _():
        h_ref[...] = jnp.zeros_like(h_ref)

    dirs = ((rf_ref, kf_ref, vf_ref, kkf_ref, lwf_ref, arf_ref, yf_ref),
            (rb_ref, kb_ref, vb_ref, kkb_ref, lwb_ref, arb_ref, yb_ref))
    chains, outs = [], []
    for d, (r_ref, k_ref, v_ref, kk_ref, lw_ref, ar_ref, y_ref) in enumerate(dirs):
        for p in range(pairs):
            lanes = slice(p * LANES, (p + 1) * LANES)
            chains.append((r_ref[0, :, lanes].astype(F32), k_ref[0, :, lanes].astype(F32),
                           v_ref[0, :, lanes].astype(F32), kk_ref[0, :, lanes].astype(F32),
                           lw_ref[0, :, lanes], ar_ref[0, :, lanes], ka_ref[:, lanes], h_ref[d, p], d == 1))
            outs.append((y_ref, lanes, d, p))
    ys, hs = _wkv_chunks(chains)
    for (y_ref, lanes, d, p), y, h_new in zip(outs, ys, hs):
        y_ref[0, :, lanes] = y.astype(y_ref.dtype)
        h_ref[d, p] = h_new


def wkv7(r, k, v, kk, lw, ar, k_a, n_ctx):
    bsz, t, width = r.shape
    c = WKV_CHUNK
    n_cc, n_chunks = n_ctx // c, t // c
    pairs = WKV_PAIRS
    wb = pairs * LANES
    fwd = pl.BlockSpec((1, c, wb), lambda i, p, s: (i, s, p))
    bwd = pl.BlockSpec((1, c, wb), lambda i, p, s: (i, _mirrored_chunk(s, n_cc, n_chunks), p))
    return pl.pallas_call(
        functools.partial(_wkv_kernel, pairs=pairs),
        grid=(bsz, width // wb, n_chunks),
        in_specs=[fwd] * 6 + [bwd] * 6 + [pl.BlockSpec((1, wb), lambda i, p, s: (0, p))],
        out_specs=[fwd, bwd],
        out_shape=[jax.ShapeDtypeStruct((bsz, t, width), ACT)] * 2,
        scratch_shapes=[pltpu.VMEM((2, pairs, LANES, LANES), F32)],
        compiler_params=pltpu.CompilerParams(dimension_semantics=("parallel", "parallel", "arbitrary"),
                                             vmem_limit_bytes=VMEM_LIMIT),
        name="wkv7",
    )(r, k, v, kk, lw[0], ar[0], r, k, v, kk, lw[1], ar[1], k_a.reshape(1, width))


def _conv_kernel(x_ref, before_ref, after_ref, w_ref, b_ref, o_ref, *, tiles_per_sample, ctx_tiles):
    half = SSM_CONV // 2
    others = [o for o in range(SSM_CONV) if o != half]
    x, taps = _segment_taps(x_ref, before_ref, after_ref, [o - half for o in others], tiles_per_sample, ctx_tiles)
    acc = x * w_ref[half:half + 1] + b_ref[...]
    for o, tap in zip(others, taps):
        acc = acc + tap * w_ref[o:o + 1]
    o_ref[...] = jax.nn.silu(acc).astype(o_ref.dtype)


def conv_silu(p, col0, conv_w, conv_b, n_ctx):
    bsz, t, _ = p.shape
    taps, ch = conv_w.shape
    tm = _seg_tile(n_ctx, t - n_ctx, 256)
    tps = t // tm
    m = bsz * t
    p2 = p.reshape(m, -1)
    before, after = _halo_specs(tm, m, ch, col0 // ch)
    w_pad = jnp.concatenate([conv_w, jnp.zeros((-taps % 8, ch), F32)], axis=0)
    const = lambda a: pl.BlockSpec(a.shape, lambda i: (0,) * a.ndim)
    out = pl.pallas_call(
        functools.partial(_conv_kernel, tiles_per_sample=tps, ctx_tiles=n_ctx // tm),
        grid=(m // tm,),
        in_specs=[pl.BlockSpec((tm, ch), lambda i: (i, col0 // ch)), before, after, const(w_pad),
                  pl.BlockSpec((1, ch), lambda i: (0, 0))],
        out_specs=pl.BlockSpec((tm, ch), lambda i: (i, 0)),
        out_shape=jax.ShapeDtypeStruct((m, ch), ACT),
        compiler_params=pltpu.CompilerParams(dimension_semantics=("parallel",), vmem_limit_bytes=VMEM_LIMIT),
        name="conv_silu",
    )(p2, p2, p2, w_pad, conv_b.reshape(1, ch))
    return out.reshape(bsz, t, ch)


def _ssd_kernel(xf_ref, dtf_ref, xb_ref, dtb_ref, bias_ref, arow_ref, tabw_ref, tabx_ref, yf_ref, yb_ref,
                s_ref, ct_ref, cp_ref, *, inner, heads_per_group):
    L, hp = SSD_CHUNK, SSM_HEAD_DIM
    e_heads = heads_per_group
    width = e_heads * hp
    gn = SSM_GROUPS * SSM_STATE
    n_heads = SSM_GROUPS * e_heads

    @pl.when(pl.program_id(1) == 0)
    def _():
        s_ref[...] = jnp.zeros_like(s_ref)

    rl = lax.broadcasted_iota(jnp.int32, (L, L), 0)
    cl = lax.broadcasted_iota(jnp.int32, (L, L), 1)
    upto = (rl >= cl, rl <= cl)
    dirs = ((xf_ref, dtf_ref, yf_ref), (xb_ref, dtb_ref, yb_ref))
    for d, (_, dt_ref, _) in enumerate(dirs):
        dt_all = jax.nn.softplus(dt_ref[0].astype(F32) + bias_ref[...])
        cum = _dot(upto[d].astype(F32), dt_all * arow_ref[...], HIGHEST)
        ct_ref[d] = cum.T
        cp_ref[d] = jnp.concatenate(_bf16_pieces(cum, 3) + _bf16_pieces(dt_all, 2), axis=1)

    lane = lax.broadcasted_iota(jnp.int32, (L, LANES), 1)

    def group(g, carry):
        two = range(2)
        xcol = pl.ds(pl.multiple_of(g * width, width), width)
        bcol = pl.ds(pl.multiple_of(inner + g * SSM_STATE, SSM_STATE), SSM_STATE)
        ccol = pl.ds(pl.multiple_of(inner + gn + g * SSM_STATE, SSM_STATE), SSM_STATE)
        x = [dirs[d][0][0, :, xcol].astype(F32) for d in two]
        bm = [dirs[d][0][0, :, bcol].astype(F32) for d in two]
        cmb = [dirs[d][0][0, :, ccol].astype(BF16) for d in two]
        tabw = [tabw_ref[d, g] for d in two]
        tabx = [tabx_ref[d, g] for d in two]
        cum_w = [_dot(cp_ref[d, :, :3 * LANES], tabw[d]) for d in two]
        dt_x = [_dot(cp_ref[d, :, 3 * LANES:], tabx[d]) for d in two]
        s0 = [s_ref[d, g] for d in two]
        cb = [_dot_nt(cmb[d], bm[d].astype(BF16)) for d in two]
        y_off = [_dot(cmb[d], s0[d].astype(BF16)) for d in two]
        bt = [bm[d].T.astype(BF16) for d in two]
        cum_x = [jnp.concatenate(
            [jnp.where(lane < hp, cum_w[d][:, (2 * q) * LANES:(2 * q + 1) * LANES],
                       cum_w[d][:, (2 * q + 1) * LANES:(2 * q + 2) * LANES]) for q in range(e_heads // 2)],
            axis=1) for d in two]
        tot_x = [cum_x[d][0:1] if d == 1 else cum_x[d][L - 1:L] for d in two]
        xdt = [x[d] * dt_x[d] for d in two]
        cbm = [jnp.where(upto[d], cb[d], 0.0) for d in two]
        lhs, rhs = [], []
        for d in two:
            for q in range(e_heads // 2):
                mats = []
                for e in (2 * q, 2 * q + 1):
                    row = d * n_heads + g * e_heads + e
                    seg = cum_w[d][:, e * LANES:(e + 1) * LANES] - ct_ref[d, pl.ds(row, 1), :]
                    mats.append((cbm[d] * jnp.exp(jnp.minimum(seg, 0.0))).astype(BF16))
                slab = xdt[d][:, q * LANES:(q + 1) * LANES]
                lhs.append(jnp.concatenate(mats, axis=1))
                rhs.append(jnp.concatenate([jnp.where(lane < hp, slab, 0.0), jnp.where(lane < hp, 0.0, slab)],
                                           axis=0).astype(BF16))
        y_diag = [_dot(a, b) for a, b in zip(lhs, rhs)]
        s_add = [_dot(bt[d], (xdt[d] * jnp.exp(tot_x[d] - cum_x[d])).astype(BF16)) for d in two]
        for d in two:
            slabs = y_diag[d * (e_heads // 2):(d + 1) * (e_heads // 2)]
            y = y_off[d] * jnp.exp(cum_x[d]) + jnp.concatenate(slabs, axis=1)
            dirs[d][2][0, :, xcol] = y.astype(dirs[d][2].dtype)
            s_ref[d, g] = s0[d] * jnp.exp(tot_x[d]) + s_add[d]
        return carry

    lax.fori_loop(0, SSM_GROUPS, group, 0, unroll=True)


def ssd(xbc, p_c, dt_col, dt_bias, a_log, n_ctx, inner):
    bsz, t, xw = xbc.shape
    L = SSD_CHUNK
    n_heads = dt_bias.shape[1]
    e_heads = n_heads // SSM_GROUPS
    width = e_heads * SSM_HEAD_DIM
    n_cc, n_chunks = n_ctx // L, t // L
    pad = LANES - 2 * n_heads
    bias = jnp.concatenate([dt_bias[0], dt_bias[1], jnp.zeros((pad,), F32)]).reshape(1, LANES)
    arow = jnp.concatenate([-jnp.exp(a_log[0]), -jnp.exp(a_log[1]), jnp.zeros((pad,), F32)]).reshape(1, LANES)
    head_row = (jnp.arange(2)[:, None, None, None] * n_heads + jnp.arange(SSM_GROUPS)[None, :, None, None] * e_heads)
    src = jnp.arange(LANES)[None, None, :, None]
    tabw = (src == head_row + jnp.arange(e_heads * LANES)[None, None, None, :] // LANES).astype(BF16)
    tabx = (src == head_row + jnp.arange(width)[None, None, None, :] // SSM_HEAD_DIM).astype(BF16)
    tabw = jnp.concatenate([tabw] * 3, axis=2)
    tabx = jnp.concatenate([tabx] * 2, axis=2)

    mirrored = lambda s: _mirrored_chunk(s, n_cc, n_chunks)
    const = lambda a: pl.BlockSpec(a.shape, lambda i, s: (0,) * a.ndim)
    return pl.pallas_call(
        functools.partial(_ssd_kernel, inner=inner, heads_per_group=e_heads),
        grid=(bsz, n_chunks),
        in_specs=[
            pl.BlockSpec((1, L, xw), lambda i, s: (i, s, 0)),
            pl.BlockSpec((1, L, LANES), lambda i, s: (i, s, dt_col // LANES)),
            pl.BlockSpec((1, L, xw), lambda i, s: (i, mirrored(s), 0)),
            pl.BlockSpec((1, L, LANES), lambda i, s: (i, mirrored(s), dt_col // LANES)),
            const(bias), const(arow), const(tabw), const(tabx),
        ],
        out_specs=[pl.BlockSpec((1, L, inner), lambda i, s: (i, s, 0)),
                   pl.BlockSpec((1, L, inner), lambda i, s: (i, mirrored(s), 0))],
        out_shape=[jax.ShapeDtypeStruct((bsz, t, inner), ACT)] * 2,
        scratch_shapes=[pltpu.VMEM((2, SSM_GROUPS, SSM_STATE, width), F32),
                        pltpu.VMEM((2, LANES, L), F32),
                        pltpu.VMEM((2, L, 5 * LANES), BF16)],
        compiler_params=pltpu.CompilerParams(dimension_semantics=("parallel", "arbitrary"),
                                             vmem_limit_bytes=VMEM_LIMIT),
        name="ssd",
    )(xbc, p_c, xbc, p_c, bias, arow, tabw, tabx)


def _merge_kernel(x_ref, yf_ref, yb_ref, r_ref, k_ref, v_ref, og_ref, z_ref, grw_ref, gssm_ref, gl_ref, gc_ref,
                  lnw_ref, lnb_ref, rk_ref, wrw_ref, wout_ref, o_ref, *, tm, tiles_per_sample, n_ctx):
    f32 = lambda ref: ref[...].astype(F32)
    y = f32(yf_ref) + f32(yb_ref)
    inv_n = 1.0 / RW_HEAD_DIM
    centred = y - _group_sums(y, RW_HEAD_DIM) * inv_n
    var = _group_sums(centred * centred, RW_HEAD_DIM) * inv_n
    y = centred * lax.rsqrt(var + RW_GN_EPS) * lnw_ref[...] + lnb_ref[...]
    y = y + _group_sums(f32(r_ref) * f32(k_ref) * rk_ref[...], RW_HEAD_DIM) * f32(v_ref)
    y_rw = y * og_ref[...]
    t1 = _dot(y_rw.astype(wrw_ref.dtype), wrw_ref[...])
    merged = jax.nn.sigmoid(f32(grw_ref)) * t1 + jax.nn.sigmoid(f32(gssm_ref)) * f32(z_ref)
    mix = _dot(merged.astype(wout_ref.dtype), wout_ref[...])
    gate = jnp.where(_ctx_rows(tm, tiles_per_sample, n_ctx), gc_ref[...], gl_ref[0])
    o_ref[...] = x_ref[...] + gate * mix


def merge(x, y_f, y_b, r, k, v, out_gate, z_ssm, proj, col_grw, col_gssm, gate_l, gate_c, ln_w, ln_b, r_k,
          w_rw, w_out, n_ctx):
    bsz, t, d = x.shape
    tm = _row_tile(t, 272)
    tps = t // tm
    m = bsz * t
    rows = lambda blk: pl.BlockSpec((tm, d), lambda i: (i, blk))
    const = lambda shape: pl.BlockSpec(shape, lambda i: (0,) * len(shape))
    flat = lambda a: a.reshape(m, -1)
    out = pl.pallas_call(
        functools.partial(_merge_kernel, tm=tm, tiles_per_sample=tps, n_ctx=n_ctx),
        grid=(m // tm,),
        in_specs=[rows(0)] * 8 + [rows(col_grw // d), rows(col_gssm // d),
                                  pl.BlockSpec((1, 1, d), lambda i: (i // tps, 0, 0)), const((1, d)),
                                  const((1, d)), const((1, d)), const((1, d)),
                                  const(w_rw.shape), const(w_out.shape)],
        out_specs=rows(0),
        out_shape=jax.ShapeDtypeStruct((m, d), F32),
        compiler_params=pltpu.CompilerParams(dimension_semantics=("parallel",), vmem_limit_bytes=VMEM_LIMIT),
        name="merge",
    )(flat(x), flat(y_f), flat(y_b), flat(r), flat(k), flat(v), flat(out_gate), flat(z_ssm), flat(proj), flat(proj),
      gate_l.reshape(bsz, 1, d), gate_c.reshape(1, d), ln_w.reshape(1, d), ln_b.reshape(1, d), r_k.reshape(1, d),
      w_rw, w_out)
    return out.reshape(bsz, t, d)


def _ssm_out_kernel(yf_ref, yb_ref, xs_ref, z_ref, dskip_ref, nw_ref, w_ref, o_ref, *, group):
    f32 = lambda ref: ref[...].astype(F32)
    y = (f32(yf_ref) + f32(yb_ref) + dskip_ref[...] * f32(xs_ref)) * jax.nn.silu(f32(z_ref))
    ms = _group_sums(y * y, group) * (1.0 / group)
    y = y * lax.rsqrt(ms + SSM_NORM_EPS) * nw_ref[...]
    o_ref[...] = _dot(y.astype(w_ref.dtype), w_ref[...]).astype(o_ref.dtype)


def ssm_out(y_f, y_b, xbc, p_c, z_col, d_skip, norm_w, w):
    bsz, t, inner = y_f.shape
    m = bsz * t
    d = w.shape[1]
    tm = _row_tile(t, 272)
    rows = lambda blk: pl.BlockSpec((tm, inner), lambda i: (i, blk))
    const = lambda shape: pl.BlockSpec(shape, lambda i: (0,) * len(shape))
    flat = lambda a: a.reshape(m, -1)
    out = pl.pallas_call(
        functools.partial(_ssm_out_kernel, group=inner // SSM_GROUPS),
        grid=(m // tm,),
        in_specs=[rows(0), rows(0), rows(0), rows(z_col // inner), const((1, inner)), const((1, inner)),
                  const(w.shape)],
        out_specs=pl.BlockSpec((tm, d), lambda i: (i, 0)),
        out_shape=jax.ShapeDtypeStruct((m, d), ACT),
        compiler_params=pltpu.CompilerParams(dimension_semantics=("parallel",), vmem_limit_bytes=VMEM_LIMIT),
        name="ssm_out",
    )(flat(y_f), flat(y_b), flat(xbc), flat(p_c), d_skip.reshape(1, inner), norm_w.reshape(1, inner), w)
    return out.reshape(bsz, t, d)


def _router_kernel(x_ref, g_ref, scl_ref, shl_ref, scc_ref, shc_ref, wr_ref, h_ref, logit_ref, *,
                   tm, tiles_per_sample, n_ctx):
    is_ctx = _ctx_rows(tm, tiles_per_sample, n_ctx)
    h = _modulated(x_ref[...], g_ref[...], scl_ref[0], shl_ref[0], scc_ref[...], shc_ref[...], is_ctx)
    bits = pltpu.bitcast(h.astype(BF16).astype(F32), jnp.uint32)
    half = h.shape[1] // 2
    h_ref[...] = (bits[:, :half] >> 16) | (bits[:, half:] & jnp.uint32(0xFFFF0000))
    logit_ref[...] = _dot(h, wr_ref[...], HIGHEST)


def router(x, gain, scale_l, shift_l, scale_c, shift_c, w_router_padded, n_ctx):
    bsz, t, d = x.shape
    tm = _row_tile(t, 544)
    tps = t // tm
    m = bsz * t
    row = lambda v: v.reshape(1, d)
    const = lambda shape: pl.BlockSpec(shape, lambda i: (0,) * len(shape))
    per_sample = pl.BlockSpec((1, 1, d), lambda i: (i // tps, 0, 0))
    return pl.pallas_call(
        functools.partial(_router_kernel, tm=tm, tiles_per_sample=tps, n_ctx=n_ctx),
        grid=(m // tm,),
        in_specs=[pl.BlockSpec((tm, d), lambda i: (i, 0)), const((1, d)), per_sample, per_sample,
                  const((1, d)), const((1, d)), const(w_router_padded.shape)],
        out_specs=[pl.BlockSpec((tm, d // 2), lambda i: (i, 0)), pl.BlockSpec((tm, LANES), lambda i: (i, 0))],
        out_shape=[jax.ShapeDtypeStruct((m, d // 2), jnp.uint32), jax.ShapeDtypeStruct((m, LANES), F32)],
        compiler_params=pltpu.CompilerParams(dimension_semantics=("parallel",), vmem_limit_bytes=VMEM_LIMIT),
        name="router",
    )(x.reshape(m, d), row(gain), scale_l.reshape(bsz, 1, d), shift_l.reshape(bsz, 1, d),
      row(scale_c), row(shift_c), w_router_padded)


def _experts_kernel(be_ref, nb_ref, x_ref, w1_ref, w3_ref, w2_ref, o_ref, w1b_ref, w3b_ref, w2b_ref):
    i = pl.program_id(0)

    @pl.when((i == 0) | (be_ref[i] != be_ref[jnp.maximum(i - 1, 0)]))
    def _():
        w1b_ref[...] = w1_ref[0, 0].astype(BF16)
        w3b_ref[...] = w3_ref[0, 0].astype(BF16)
        w2b_ref[...] = w2_ref[0, 0].astype(BF16)

    @pl.when(i < nb_ref[0])
    def _():
        packed = x_ref[...]
        x = jnp.concatenate([pltpu.bitcast(packed << 16, F32),
                             pltpu.bitcast(packed & jnp.uint32(0xFFFF0000), F32)], axis=1).astype(BF16)
        hidden = jax.nn.silu(_dot(x, w1b_ref[...])) * _dot(x, w3b_ref[...])
        o_ref[...] = _dot(hidden.astype(BF16), w2b_ref[...])

    @pl.when(i >= nb_ref[0])
    def _():
        o_ref[...] = jnp.zeros_like(o_ref)


def experts(xb, block_expert, n_used, w1, w3, w2, layer):
    n_rows = xb.shape[0]
    d, de = w1.shape[2:]
    rows = MOE_ROWS
    n_blocks = n_rows // rows
    return pl.pallas_call(
        _experts_kernel,
        grid_spec=pltpu.PrefetchScalarGridSpec(
            num_scalar_prefetch=2,
            grid=(n_blocks,),
            in_specs=[
                pl.BlockSpec((rows, d // 2), lambda i, be, nb: (i, 0)),
                pl.BlockSpec((1, 1, d, de), lambda i, be, nb: (layer, be[i], 0, 0)),
                pl.BlockSpec((1, 1, d, de), lambda i, be, nb: (layer, be[i], 0, 0)),
                pl.BlockSpec((1, 1, de, d), lambda i, be, nb: (layer, be[i], 0, 0)),
            ],
            out_specs=pl.BlockSpec((rows, d), lambda i, be, nb: (i, 0)),
            scratch_shapes=[pltpu.VMEM((d, de), BF16), pltpu.VMEM((d, de), BF16), pltpu.VMEM((de, d), BF16)],
        ),
        out_shape=jax.ShapeDtypeStruct((n_rows, d), F32),
        compiler_params=pltpu.CompilerParams(dimension_semantics=("arbitrary",), vmem_limit_bytes=VMEM_LIMIT),
        name="experts",
    )(block_expert, n_used, xb, w1, w3, w2)


def _combine_kernel(x_ref, y_ref, gate_ref, gl_ref, gc_ref, fin_ref, o_ref, *, tm, tiles_per_sample, n_ctx, final):
    d = x_ref.shape[1]
    g = gate_ref[...]
    f = y_ref[:, :d] * g[:, 0:1] + y_ref[:, d:] * g[:, 1:2]
    mod = jnp.where(_ctx_rows(tm, tiles_per_sample, n_ctx), gc_ref[...], gl_ref[0])
    out = x_ref[...] + mod * f
    if final:
        out = out * lax.rsqrt(jnp.mean(out * out, axis=-1, keepdims=True) + NORM_EPS) * fin_ref[...]
    o_ref[...] = out


def combine(x, y_pairs, gates, gate_l, gate_c, final_gain, n_ctx, final):
    bsz, t, d = x.shape
    m = bsz * t
    tm = _row_tile(t, 544)
    tps = t // tm
    const = lambda shape: pl.BlockSpec(shape, lambda i: (0,) * len(shape))
    out = pl.pallas_call(
        functools.partial(_combine_kernel, tm=tm, tiles_per_sample=tps, n_ctx=n_ctx, final=final),
        grid=(m // tm,),
        in_specs=[pl.BlockSpec((tm, d), lambda i: (i, 0)), pl.BlockSpec((tm, TOP_K * d), lambda i: (i, 0)),
                  pl.BlockSpec((tm, LANES), lambda i: (i, 0)), pl.BlockSpec((1, 1, d), lambda i: (i // tps, 0, 0)),
                  const((1, d)), const((1, d))],
        out_specs=pl.BlockSpec((tm, d), lambda i: (i, 0)),
        out_shape=jax.ShapeDtypeStruct((m, d), F32),
        compiler_params=pltpu.CompilerParams(dimension_semantics=("parallel",), vmem_limit_bytes=VMEM_LIMIT),
        name="combine",
    )(x.reshape(m, d), y_pairs.reshape(m, TOP_K * d), gates, gate_l.reshape(bsz, 1, d), gate_c.reshape(1, d),
      final_gain.reshape(1, d))
    return out.reshape(bsz, t, d)


def _top2(vals):
    idx = jnp.arange(vals.shape[-1], dtype=jnp.int32)
    i1 = jnp.argmax(vals, axis=-1).astype(jnp.int32)
    v1 = jnp.max(vals, axis=-1)
    rest = jnp.where(idx == i1[..., None], -jnp.inf, vals)
    i2 = jnp.argmax(rest, axis=-1).astype(jnp.int32)
    v2 = jnp.max(rest, axis=-1)
    return v1, i1, v2, i2


def moe(h, logits, b_router, w1, w3, w2, layer):
    n_tok, d = h.shape
    n_exp = b_router.shape[0]
    epg = n_exp // N_EXPERT_GROUPS
    scores = jax.nn.sigmoid(logits)
    biased = (scores + b_router.astype(F32)).reshape(n_tok, N_EXPERT_GROUPS, epg)
    g1, _, g2, _ = _top2(biased)
    top_group = jnp.argmax(g1 + g2, axis=-1).astype(jnp.int32)
    in_top = jnp.arange(N_EXPERT_GROUPS, dtype=jnp.int32)[None, :, None] == top_group[:, None, None]
    in_group = jnp.sum(jnp.where(in_top, biased, 0.0), axis=1)
    _, l1, _, l2 = _top2(in_group)
    expert = top_group[:, None] * epg + jnp.stack([l1, l2], axis=-1)
    picked = expert[:, :, None] == jnp.arange(n_exp, dtype=jnp.int32)[None, None, :]
    gate = jnp.sum(jnp.where(picked, scores[:, None, :], 0.0), axis=-1)
    gate = gate / jnp.sum(gate, axis=-1, keepdims=True)

    n_assign = n_tok * TOP_K
    flat_e = expert.reshape(-1).astype(jnp.int32)
    onehot = picked.reshape(n_assign, n_exp).astype(jnp.int32)
    rank = jnp.sum((jnp.cumsum(onehot, axis=0) - onehot) * onehot, axis=-1)
    counts = jnp.sum(onehot, axis=0)
    padded = (counts + MOE_ROWS - 1) // MOE_ROWS * MOE_ROWS
    pad_end = jnp.cumsum(padded)
    dest = ((pad_end - padded)[flat_e] + rank).astype(jnp.int32)
    n_blocks = -(-n_assign // MOE_ROWS) + n_exp
    slot_token = jnp.zeros((n_blocks * MOE_ROWS,), jnp.int32).at[dest].set(
        jnp.arange(n_assign, dtype=jnp.int32) // TOP_K)
    block_expert = jnp.clip(
        jnp.sum(jnp.arange(n_blocks, dtype=jnp.int32)[:, None] >= (pad_end // MOE_ROWS)[None, :], axis=-1),
        0, n_exp - 1).astype(jnp.int32)
    n_used = (pad_end[-1:] // MOE_ROWS).astype(jnp.int32)
    yb = experts(h[slot_token], block_expert, n_used, w1, w3, w2, layer)
    return yb[dest], jnp.pad(gate, ((0, 0), (0, LANES - TOP_K)))


def _ssm_branch(p, prm, w_branch, n_ctx):
    inner = prm["norm_w"].shape[0]
    xbc_w = prm["conv_w"].shape[1]
    xbc = conv_silu(p, 0, prm["conv_w"], prm["conv_b"], n_ctx)
    y_f, y_b = ssd(xbc, p, xbc_w + inner, prm["dt_bias"], prm["a_log"], n_ctx, inner)
    d_skip = jnp.repeat(prm["d"][0] + prm["d"][1], SSM_HEAD_DIM)
    return ssm_out(y_f, y_b, xbc, p, xbc_w, d_skip, prm["norm_w"], w_branch)


def _pad_cols(w, n):
    return jnp.pad(w, ((0, 0), (0, n - w.shape[1])))


def kernel(x, c, ctx, c_ctx, w_mod, b_mod, norm_mix_g, w_in, rw_shift_mu, rw_w0, rw_w2, rw_a0, rw_a2, rw_g2,
           rw_k_k, rw_k_a, rw_r_k, rw_ln_w, rw_ln_b, ssm_conv_w, ssm_conv_b, ssm_dt_bias, ssm_a_log, ssm_d,
           ssm_norm_w, w_branch_rw, w_branch_ssm, w_out, norm_ffn_g, w_router, b_router, exp_w1, exp_w3, exp_w2,
           norm_final_g):
    bsz, n_lat, d = x.shape
    depth = w_in.shape[0]
    n_ctx = ctx.shape[1]
    rows = n_lat // GRID_W
    t = n_ctx + n_lat
    width = RW_HEADS * RW_HEAD_DIM
    dl, al, gl = rw_w2.shape[2], rw_a2.shape[2], rw_g2.shape[1]
    inner = ssm_norm_w.shape[1]
    xbc_w = ssm_conv_w.shape[2]
    n_heads = ssm_dt_bias.shape[2]
    n_exp = w_router.shape[1]

    def to_c(a):
        ch = a.shape[-1]
        lat = a[:, n_ctx:].reshape(bsz, rows, GRID_W, ch).transpose(0, 2, 1, 3).reshape(bsz, n_lat, ch)
        return jnp.concatenate([a[:, :n_ctx], lat], axis=1)

    def to_r(a):
        ch = a.shape[-1]
        lat = a[:, n_ctx:].reshape(bsz, GRID_W, rows, ch).transpose(0, 2, 1, 3).reshape(bsz, n_lat, ch)
        return jnp.concatenate([a[:, :n_ctx], lat], axis=1)

    o = 0
    src = {}
    for name, size in (("r", width), ("w_f", dl), ("w_b", dl), ("k", width), ("v", width), ("a_f", al),
                       ("a_b", al), ("g", gl), ("z", inner), ("xbc", xbc_w), ("dt", 2 * n_heads),
                       ("g_rw", d), ("g_ssm", d)):
        src[name] = (o, size)
        o += size
    take = lambda a, name: a[..., src[name][0]:src[name][0] + src[name][1]]
    lora_w = 2 * dl + 2 * al
    lora_pad = -(-lora_w // LANES) * LANES
    g_pad = -(-gl // LANES) * LANES
    dt_pad = -(-2 * n_heads // LANES) * LANES
    lay = {"r": (0, width), "k": (width, width), "v": (2 * width, width), "g_rw": (3 * width, d),
           "g_ssm": (3 * width + d, d), "lora": (3 * width + 2 * d, lora_w),
           "g": (3 * width + 2 * d + lora_pad, gl)}

    silu_c = jax.nn.silu(c)
    silu_cc = jax.nn.silu(c_ctx)[None, :]
    act = jnp.concatenate([silu_c, silu_cc, jnp.zeros((-(bsz + 1) % 8, d), F32)], axis=0)
    w_router_p = _pad_cols(w_router.astype(F32), LANES)

    xr = jnp.concatenate([ctx, x], axis=1)
    for l in range(depth):
        mod = matmul(act, w_mod[l], precision=HIGHEST, name="modulation")[:bsz + 1] + b_mod[l]
        shift_m, scale_m, gate_m, shift_f, scale_f, gate_f = jnp.split(mod[:bsz], 6, axis=-1)
        cshift_m, cscale_m, cgate_m, cshift_f, cscale_f, cgate_f = jnp.split(mod[bsz], 6, axis=-1)

        wl = w_in[l].astype(BF16)
        w_r = jnp.concatenate(
            [take(wl, "r"), take(wl, "k"), take(wl, "v"), take(wl, "g_rw"), take(wl, "g_ssm"),
             _pad_cols(jnp.concatenate([take(wl, n) for n in ("w_f", "w_b", "a_f", "a_b")], axis=1), lora_pad),
             _pad_cols(take(wl, "g"), g_pad)], axis=1)
        w_c = jnp.concatenate([take(wl, "xbc"), take(wl, "z"), _pad_cols(take(wl, "dt"), dt_pad)], axis=1)
        mods = (norm_mix_g[l], scale_m, shift_m, cscale_m, cshift_m)
        h_mix = mod_norm(xr, *mods, n_ctx)
        p_r = project(h_mix, w_r, ACT)
        p_c = project(to_c(h_mix), w_c, ACT)

        mu = rw_shift_mu[l]
        rw_prm = dict(
            shift_mu={"r": take(mu, "r"), "k": take(mu, "k"), "v": take(mu, "v"), "g": take(mu, "g"),
                      "lora": jnp.concatenate([take(mu, n) for n in ("w_f", "w_b", "a_f", "a_b")])},
            w0=rw_w0[l], w2=rw_w2[l], a0=rw_a0[l], a2=rw_a2[l], g2=rw_g2[l],
            k_k=rw_k_k[l], k_a=rw_k_a[l], r_k=rw_r_k[l], ln_w=rw_ln_w[l], ln_b=rw_ln_b[l])
        r, k, v, kk, lw_f, lw_b, ar_f, ar_b, out_gate = rwkv_prep(p_r, lay, rw_prm, n_ctx)
        y_f, y_b = wkv7(r, k, v, kk, (lw_f, lw_b), (ar_f, ar_b), rw_k_a[l], n_ctx)
        ssm_prm = dict(conv_w=ssm_conv_w[l], conv_b=ssm_conv_b[l], dt_bias=ssm_dt_bias[l], a_log=ssm_a_log[l],
                       d=ssm_d[l], norm_w=ssm_norm_w[l])
        z_ssm = to_r(_ssm_branch(p_c, ssm_prm, w_branch_ssm[l].astype(BF16), n_ctx))
        xr = merge(xr, y_f, y_b, r, k, v, out_gate, z_ssm, p_r, lay["g_rw"][0], lay["g_ssm"][0], gate_m, cgate_m,
                   rw_ln_w[l], rw_ln_b[l], rw_r_k[l], w_branch_rw[l].astype(BF16), w_out[l].astype(BF16), n_ctx)

        h, logits = router(xr, norm_ffn_g[l], scale_f, shift_f, cscale_f, cshift_f, w_router_p, n_ctx)
        y_pairs, gates = moe(h, logits[:, :n_exp], b_router, exp_w1, exp_w3, exp_w2, l)
        xr = combine(xr, y_pairs, gates, gate_f, cgate_f, norm_final_g, n_ctx, final=(l == depth - 1))
    return xr[:, n_ctx:]
```

```python
import functools
import math

import jax
import jax.numpy as jnp
from jax import lax
from jax.experimental import pallas as pl
from jax.experimental.pallas import tpu as pltpu

F32 = jnp.float32
BF16 = jnp.bfloat16
HIGHEST = lax.Precision.HIGHEST

GRID_W = 64
RW_HEADS = 16
RW_HEAD_DIM = 64
RW_GN_EPS = 64e-5
SSM_HEAD_DIM = 64
SSM_GROUPS = 8
SSM_STATE = 128
SSM_CONV = 5
SSM_NORM_EPS = 1e-5
N_EXPERT_GROUPS = 4
TOP_K = 2
NORM_EPS = 1e-6

LANES = 128
WKV_CHUNK = 64
WKV_BASE = 8
WKV_PAIRS = 8
SSD_CHUNK = 128
MOE_ROWS = 256
VMEM_LIMIT = 56 * 1024 * 1024


def _dot(a, b, precision=None):
    return jnp.dot(a, b, preferred_element_type=F32, precision=precision)


def _dot_nt(a, b, precision=None):
    return lax.dot_general(a, b, (((1,), (1,)), ((), ())), preferred_element_type=F32, precision=precision)


def _dot_tn(a, b, precision=None):
    return lax.dot_general(a, b, (((0,), (0,)), ((), ())), preferred_element_type=F32, precision=precision)


def _row_tile(n_rows_per_sample, limit=1088):
    for tm in (1088, 544, 512, 272, 256, 128, 64, 32, 16):
        if tm <= limit and n_rows_per_sample % tm == 0:
            return tm
    raise ValueError(n_rows_per_sample)


def _col_tile(n_cols, limit=1536):
    best = LANES
    for k in range(1, n_cols // LANES + 1):
        tn = k * LANES
        if n_cols % tn == 0 and tn <= limit:
            best = tn
    return best


def _bf16_pieces(x, n):
    pieces = []
    for _ in range(n):
        p = x.astype(BF16)
        pieces.append(p)
        x = x - p.astype(F32)
    return pieces


def _group_sums(x, group):
    span = max(group, LANES)
    rr = lax.broadcasted_iota(jnp.int32, (span, span), 0) // group
    cc = lax.broadcasted_iota(jnp.int32, (span, span), 1) // group
    ones = (rr == cc).astype(BF16)
    pieces = _bf16_pieces(x, 3)
    cols = []
    for j in range(x.shape[1] // span):
        sl = slice(j * span, (j + 1) * span)
        cols.append(_dot(pieces[0][:, sl], ones) + _dot(pieces[1][:, sl], ones) + _dot(pieces[2][:, sl], ones))
    return cols[0] if len(cols) == 1 else jnp.concatenate(cols, axis=1)


HALO = 16
ACT = BF16


def _seg_tile(n_ctx, n_lat, limit):
    for tm in (1024, 512, 256, 128, 64, 32, 16):
        if tm <= limit and n_ctx % tm == 0 and n_lat % tm == 0:
            return tm
    raise ValueError((n_ctx, n_lat))


def _segment_taps(x_ref, before_ref, after_ref, offsets, tiles_per_sample, ctx_tiles):
    tm = x_ref.shape[0]
    ti = pl.program_id(0) % tiles_per_sample
    first = (ti == 0) | (ti == ctx_tiles)
    last = (ti == ctx_tiles - 1) | (ti == tiles_per_sample - 1)
    x = x_ref[...].astype(F32)
    before = before_ref[...].astype(F32) * jnp.where(first, 0.0, 1.0)
    after = after_ref[...].astype(F32) * jnp.where(last, 0.0, 1.0)
    ext = jnp.concatenate([before, x, after], axis=0)
    n = tm + 2 * HALO
    return x, [pltpu.roll(ext, (-o) % n, 0)[HALO:HALO + tm] for o in offsets]


def _halo_specs(tm, n_rows, width, col_block):
    last = n_rows // HALO - 1
    return (pl.BlockSpec((HALO, width), lambda i: (jnp.maximum(i * (tm // HALO) - 1, 0), col_block)),
            pl.BlockSpec((HALO, width), lambda i: (jnp.minimum((i + 1) * (tm // HALO), last), col_block)))


def _mirrored_chunk(s, n_ctx_chunks, n_chunks):
    return jnp.where(s < n_ctx_chunks, n_ctx_chunks - 1 - s, n_chunks - 1 + n_ctx_chunks - s)


def _modulated(x, g, sc_l, sh_l, sc_c, sh_c, is_ctx):
    y = x * lax.rsqrt(jnp.mean(x * x, axis=-1, keepdims=True) + NORM_EPS) * g
    return y * (1.0 + jnp.where(is_ctx, sc_c, sc_l)) + jnp.where(is_ctx, sh_c, sh_l)


def _ctx_rows(tm, tiles_per_sample, n_ctx):
    row0 = (pl.program_id(0) % tiles_per_sample) * tm
    return row0 + lax.broadcasted_iota(jnp.int32, (tm, 1), 0) < n_ctx


def _mod_norm_kernel(x_ref, g_ref, scl_ref, shl_ref, scc_ref, shc_ref, h_ref, *, tm, tiles_per_sample, n_ctx):
    is_ctx = _ctx_rows(tm, tiles_per_sample, n_ctx)
    h = _modulated(x_ref[...], g_ref[...], scl_ref[0], shl_ref[0], scc_ref[...], shc_ref[...], is_ctx)
    h_ref[...] = h.astype(h_ref.dtype)


def mod_norm(x, gain, scale_l, shift_l, scale_c, shift_c, n_ctx):
    bsz, t, d = x.shape
    tm = _row_tile(t)
    tps = t // tm
    row = lambda v: v.reshape(1, d)
    const = pl.BlockSpec((1, d), lambda i: (0, 0))
    per_sample = pl.BlockSpec((1, 1, d), lambda i: (i // tps, 0, 0))
    out = pl.pallas_call(
        functools.partial(_mod_norm_kernel, tm=tm, tiles_per_sample=tps, n_ctx=n_ctx),
        grid=(bsz * tps,),
        in_specs=[pl.BlockSpec((tm, d), lambda i: (i, 0)), const, per_sample, per_sample, const, const],
        out_specs=pl.BlockSpec((tm, d), lambda i: (i, 0)),
        out_shape=jax.ShapeDtypeStruct((bsz * t, d), BF16),
        compiler_params=pltpu.CompilerParams(dimension_semantics=("parallel",), vmem_limit_bytes=VMEM_LIMIT),
        name="mod_norm",
    )(x.reshape(bsz * t, d), row(gain), scale_l.reshape(bsz, 1, d), shift_l.reshape(bsz, 1, d),
      row(scale_c), row(shift_c))
    return out.reshape(bsz, t, d)


def _project_kernel(h_ref, w_ref, o_ref):
    o_ref[...] = _dot(h_ref[...], w_ref[...]).astype(o_ref.dtype)


def project(h, w, out_dtype):
    bsz, t, d = h.shape
    n = w.shape[1]
    m = bsz * t
    tm = _row_tile(t, 544)
    tn = _col_tile(n, limit=n)
    out = pl.pallas_call(
        _project_kernel,
        grid=(n // tn, m // tm),
        in_specs=[pl.BlockSpec((tm, d), lambda j, i: (i, 0)), pl.BlockSpec((d, tn), lambda j, i: (0, j))],
        out_specs=pl.BlockSpec((tm, tn), lambda j, i: (i, j)),
        out_shape=jax.ShapeDtypeStruct((m, n), out_dtype),
        compiler_params=pltpu.CompilerParams(dimension_semantics=("parallel", "parallel"),
                                             vmem_limit_bytes=VMEM_LIMIT),
        name="project",
    )(h.reshape(m, d), w)
    return out.reshape(bsz, t, n)


def _matmul_kernel(x_ref, w_ref, o_ref, *, act, precision):
    x = x_ref[...]
    if act == "tanh":
        x = jnp.tanh(x)
    elif act == "sigmoid":
        x = jax.nn.sigmoid(x)
    o_ref[...] = _dot(x.astype(w_ref.dtype), w_ref[...], precision).astype(o_ref.dtype)


def matmul(x, w, act=None, precision=None, out_dtype=F32, name="matmul"):
    m, k = x.shape
    n = w.shape[1]
    tm = _row_tile(m) if m >= 16 else m
    tn = _col_tile(n)
    return pl.pallas_call(
        functools.partial(_matmul_kernel, act=act, precision=precision),
        grid=(m // tm, n // tn),
        in_specs=[pl.BlockSpec((tm, k), lambda i, j: (i, 0)), pl.BlockSpec((k, tn), lambda i, j: (0, j))],
        out_specs=pl.BlockSpec((tm, tn), lambda i, j: (i, j)),
        out_shape=jax.ShapeDtypeStruct((m, n), out_dtype),
        compiler_params=pltpu.CompilerParams(dimension_semantics=("parallel", "parallel"),
                                             vmem_limit_bytes=VMEM_LIMIT),
        name=name,
    )(x, w)


def _rwkv_prep_kernel(m_ref, mb_ref, ma_ref, s_ref, sb_ref, sa_ref, mu_m_ref, mu_s_ref, kk_ref, bias_ref,
                      wl_ref, g2_ref, r_ref, k_ref, v_ref, kko_ref, lwf_ref, lwb_ref, arf_ref, arb_ref, og_ref, *,
                      tiles_per_sample, ctx_tiles, width, n_decay):
    def shifted(x_ref, before_ref, after_ref, mu_ref):
        x, (prev, nxt) = _segment_taps(x_ref, before_ref, after_ref, (-1, 1), tiles_per_sample, ctx_tiles)
        return x + mu_ref[...] * (0.5 * (prev + nxt) - x)

    main = shifted(m_ref, mb_ref, ma_ref, mu_m_ref)
    small = shifted(s_ref, sb_ref, sa_ref, mu_s_ref)
    k = main[:, width:2 * width]
    r_ref[...] = main[:, :width].astype(r_ref.dtype)
    k_ref[...] = k.astype(k_ref.dtype)
    v_ref[...] = main[:, 2 * width:].astype(v_ref.dtype)
    kk = k * kk_ref[...]
    kko_ref[...] = (kk * lax.rsqrt(jnp.maximum(_group_sums(kk * kk, RW_HEAD_DIM), 1e-24))).astype(kko_ref.dtype)

    half = small.shape[1] // 2
    lora = small[:, :half]
    lane = lax.broadcasted_iota(jnp.int32, lora.shape, 1)
    heads = _dot(jnp.where(lane < n_decay, jnp.tanh(lora), lora).astype(BF16), wl_ref[...]) + bias_ref[...]
    for d, (lw_ref, ar_ref) in enumerate(((lwf_ref, arf_ref), (lwb_ref, arb_ref))):
        w_pre = heads[:, d * width:(d + 1) * width]
        lw_ref[...] = (-math.exp(-0.5)) * jax.nn.sigmoid(w_pre)
        ar_ref[...] = jax.nn.sigmoid(heads[:, (2 + d) * width:(3 + d) * width])
    og_ref[...] = _dot(jax.nn.sigmoid(small[:, half:]).astype(BF16), g2_ref[...])


def rwkv_prep(p, lay, prm, n_ctx):
    bsz, t, _ = p.shape
    width = RW_HEADS * RW_HEAD_DIM
    m = bsz * t
    tm = _seg_tile(n_ctx, t - n_ctx, 256)
    tps = t // tm
    p2 = p.reshape(m, -1)
    main_w = 3 * width
    small0 = lay["lora"][0]
    small_w = p2.shape[1] - small0
    half = small_w // 2
    dl, al, gl = prm["w2"].shape[1], prm["a2"].shape[1], prm["g2"].shape[0]
    assert lay["r"][0] == 0 and lay["g"][0] == small0 + half and small0 % small_w == 0
    mu = prm["shift_mu"]
    pad1 = lambda a, n: jnp.pad(a, (0, n - a.shape[0]))
    mu_main = jnp.concatenate([mu["r"], mu["k"], mu["v"]]).reshape(1, main_w)
    mu_small = jnp.concatenate([pad1(mu["lora"], half), pad1(mu["g"], half)]).reshape(1, small_w)
    wl = jnp.zeros((half, 4 * width), F32)
    for j, (blk, rows0, nrows) in enumerate(((prm["w2"][0], 0, dl), (prm["w2"][1], dl, dl),
                                             (prm["a2"][0], 2 * dl, al), (prm["a2"][1], 2 * dl + al, al))):
        wl = wl.at[rows0:rows0 + nrows, j * width:(j + 1) * width].set(blk)
    bias = jnp.concatenate([prm["w0"][0], prm["w0"][1], prm["a0"][0], prm["a0"][1]]).reshape(1, 4 * width)
    g2 = jnp.pad(prm["g2"], ((0, half - gl), (0, 0))).astype(BF16)

    mb, ma = _halo_specs(tm, m, main_w, 0)
    sb, sa = _halo_specs(tm, m, small_w, small0 // small_w)
    const = lambda a: pl.BlockSpec(a.shape, lambda i: (0,) * a.ndim)
    out_spec = pl.BlockSpec((tm, width), lambda i: (i, 0))
    outs = pl.pallas_call(
        functools.partial(_rwkv_prep_kernel, tiles_per_sample=tps, ctx_tiles=n_ctx // tm, width=width,
                          n_decay=2 * dl),
        grid=(m // tm,),
        in_specs=[pl.BlockSpec((tm, main_w), lambda i: (i, 0)), mb, ma,
                  pl.BlockSpec((tm, small_w), lambda i: (i, small0 // small_w)), sb, sa,
                  const(mu_main), const(mu_small), pl.BlockSpec((1, width), lambda i: (0, 0)), const(bias),
                  pl.BlockSpec(wl.shape, lambda i: (0, 0)), const(g2)],
        out_specs=[out_spec] * 9,
        out_shape=[jax.ShapeDtypeStruct((m, width), ACT)] * 4 + [jax.ShapeDtypeStruct((m, width), F32)] * 5,
        compiler_params=pltpu.CompilerParams(dimension_semantics=("parallel",), vmem_limit_bytes=VMEM_LIMIT),
        name="rwkv_prep",
    )(p2, p2, p2, p2, p2, p2, mu_main, mu_small, prm["k_k"].reshape(1, width), bias, wl.astype(BF16), g2)
    return [o.reshape(bsz, t, width) for o in outs]


def _wkv_chunks(chains):
    c = chains[0][0].shape[0]
    n = 2 * c
    rev = [ch[8] for ch in chains]
    each = lambda f, *cols: [f(*a) for a in zip(*cols)]
    bf = lambda x: x.astype(BF16)
    cat0 = lambda *xs: jnp.concatenate([bf(x) for x in xs], axis=0)
    cat1 = lambda *xs: jnp.concatenate([bf(x) for x in xs], axis=1)

    rc = lax.broadcasted_iota(jnp.int32, (c, c), 0)
    cc = lax.broadcasted_iota(jnp.int32, (c, c), 1)
    seen = {False: (rc >= cc).astype(F32), True: (rc <= cc).astype(F32)}
    r2 = lax.broadcasted_iota(jnp.int32, (n, n), 0)
    c2 = lax.broadcasted_iota(jnp.int32, (n, n), 1)
    before = {False: r2 > c2, True: r2 < c2}
    upto = {False: r2 >= c2, True: r2 <= c2}
    eye = jnp.where(r2 == c2, 1.0, 0.0)
    first = lax.broadcasted_iota(jnp.int32, (c, LANES), 1) < RW_HEAD_DIM
    stack = lambda x: jnp.concatenate([jnp.where(first, x, 0.0), jnp.where(first, 0.0, x)], axis=0)
    rk = lax.broadcasted_iota(jnp.int32, (LANES, LANES), 0)
    ck = lax.broadcasted_iota(jnp.int32, (LANES, LANES), 1)

    seen3 = {v_: jnp.concatenate([bf(m)] * 3, axis=1) for v_, m in seen.items()}
    cum = [_dot(seen3[ch[8]], jnp.concatenate(_bf16_pieces(ch[4], 3), axis=0)) for ch in chains]
    e_neg = each(lambda q: jnp.exp(-q), cum)
    e_end = [jnp.exp(q[0:1] if ch[8] else q[c - 1:c]) for q, ch in zip(cum, chains)]
    e_end_col = each(lambda e: jnp.sum(jnp.where(rk == ck, e, 0.0), axis=1, keepdims=True), e_end)
    rs = [stack(ch[0] * jnp.exp(q)) for ch, q in zip(chains, cum)]
    as_ = [stack(-ch[3] * jnp.exp(q - ch[4])) for ch, q in zip(chains, cum)]
    bs = [stack(ch[3] * ch[5] * en) for ch, en in zip(chains, e_neg)]
    ks = [stack(ch[1] * (1.0 + (ch[5] - 1.0) * ch[6]) * en) for ch, en in zip(chains, e_neg)]
    vs = [stack(ch[2]) for ch in chains]
    h0 = [ch[7] for ch in chains]

    pair = each(lambda a, b, c_, d: _dot_nt(cat0(a, b), cat0(c_, d)), rs, as_, bs, ks)
    m_rb = [jnp.where(upto[v_], p[:n, :n], 0.0) for p, v_ in zip(pair, rev)]
    m_rk = [jnp.where(upto[v_], p[:n, n:], 0.0) for p, v_ in zip(pair, rev)]
    l_ab = [jnp.where(before[v_], p[n:, :n], 0.0) for p, v_ in zip(pair, rev)]
    m_ak = [jnp.where(before[v_], p[n:, n:], 0.0) for p, v_ in zip(pair, rev)]

    same = lambda s: (r2 // s) == (c2 // s)
    diag = each(lambda l: jnp.where(same(WKV_BASE), l, 0.0), l_ab)
    inv = each(lambda l: eye + l, diag)
    power = each(lambda l: _dot(bf(l), bf(l)), diag)
    steps = int(math.log2(WKV_BASE)) - 1
    for it in range(steps):
        if it < steps - 1:
            both = each(lambda i, p: _dot(cat0(i, p), bf(p)), inv, power)
            inv = each(lambda i, b: i + b[:n], inv, both)
            power = each(lambda b: b[n:], both)
        else:
            inv = each(lambda i, p: i + _dot(bf(i), bf(p)), inv, power)
    size = WKV_BASE
    while size < c:
        joins = same(2 * size) & jnp.logical_not(same(size))
        ot = each(lambda l, i: _dot(bf(jnp.where(joins, l, 0.0)), bf(i)), l_ab, inv)
        inv = each(lambda i, x: i + _dot(bf(i), bf(x)), inv, ot)
        size *= 2

    state_and_v = each(cat0, h0, vs)
    w = each(lambda a, m, sv: _dot(cat1(a, m), sv), as_, m_ak, state_and_v)
    y0 = each(lambda r_, m, sv: _dot(cat1(r_, m), sv), rs, m_rk, state_and_v)
    u = each(lambda i, w_: _dot(bf(i), bf(w_)), inv, w)
    y = each(lambda y_, m, u_: y_ + _dot(bf(m), bf(u_)), y0, m_rb, u)
    decayed = each(lambda b, k_, e: jnp.concatenate([b * e, k_ * e], axis=0).T, bs, ks, e_end)
    h_new = each(lambda h, ec, dc, u_, v_: h * ec + _dot(bf(dc), cat0(u_, v_)), h0, e_end_col, decayed, u, vs)
    return [q[:c] + q[c:] for q in y], h_new


def _wkv_kernel(rf_ref, kf_ref, vf_ref, kkf_ref, lwf_ref, arf_ref, rb_ref, kb_ref, vb_ref, kkb_ref, lwb_ref,
                arb_ref, ka_ref, yf_ref, yb_ref, h_ref, *, pairs):
    @pl.when(pl.program_id(2) == 0)
    def _():
        h_ref[...] = jnp.zeros_like(h_ref)

    dirs = ((rf_ref, kf_ref, vf_ref, kkf_ref, lwf_ref, arf_ref, yf_ref),
            (rb_ref, kb_ref, vb_ref, kkb_ref, lwb_ref, arb_ref, yb_ref))
    chains, outs = [], []
    for d, (r_ref, k_ref, v_ref, kk_ref, lw_ref, ar_ref, y_ref) in enumerate(dirs):
        for p in range(pairs):
            lanes = slice(p * LANES, (p + 1) * LANES)
            chains.append((r_ref[0, :, lanes].astype(F32), k_ref[0, :, lanes].astype(F32),
                           v_ref[0, :, lanes].astype(F32), kk_ref[0, :, lanes].astype(F32),
                           lw_ref[0, :, lanes], ar_ref[0, :, lanes], ka_ref[:, lanes], h_ref[d, p], d == 1))
            outs.append((y_ref, lanes, d, p))
    ys, hs = _wkv_chunks(chains)
    for (y_ref, lanes, d, p), y, h_new in zip(outs, ys, hs):
        y_ref[0, :, lanes] = y.astype(y_ref.dtype)
        h_ref[d, p] = h_new


def wkv7(r, k, v, kk, lw, ar, k_a, n_ctx):
    bsz, t, width = r.shape
    c = WKV_CHUNK
    n_cc, n_chunks = n_ctx // c, t // c
    pairs = WKV_PAIRS
    wb = pairs * LANES
    fwd = pl.BlockSpec((1, c, wb), lambda i, p, s: (i, s, p))
    bwd = pl.BlockSpec((1, c, wb), lambda i, p, s: (i, _mirrored_chunk(s, n_cc, n_chunks), p))
    return pl.pallas_call(
        functools.partial(_wkv_kernel, pairs=pairs),
        grid=(bsz, width // wb, n_chunks),
        in_specs=[fwd] * 6 + [bwd] * 6 + [pl.BlockSpec((1, wb), lambda i, p, s: (0, p))],
        out_specs=[fwd, bwd],
        out_shape=[jax.ShapeDtypeStruct((bsz, t, width), ACT)] * 2,
        scratch_shapes=[pltpu.VMEM((2, pairs, LANES, LANES), F32)],
        compiler_params=pltpu.CompilerParams(dimension_semantics=("parallel", "parallel", "arbitrary"),
                                             vmem_limit_bytes=VMEM_LIMIT),
        name="wkv7",
    )(r, k, v, kk, lw[0], ar[0], r, k, v, kk, lw[1], ar[1], k_a.reshape(1, width))


def _conv_kernel(x_ref, before_ref, after_ref, w_ref, b_ref, o_ref, *, tiles_per_sample, ctx_tiles):
    half = SSM_CONV // 2
    others = [o for o in range(SSM_CONV) if o != half]
    x, taps = _segment_taps(x_ref, before_ref, after_ref, [o - half for o in others], tiles_per_sample, ctx_tiles)
    acc = x * w_ref[half:half + 1] + b_ref[...]
    for o, tap in zip(others, taps):
        acc = acc + tap * w_ref[o:o + 1]
    o_ref[...] = jax.nn.silu(acc).astype(o_ref.dtype)


def conv_silu(p, col0, conv_w, conv_b, n_ctx):
    bsz, t, _ = p.shape
    taps, ch = conv_w.shape
    tm = _seg_tile(n_ctx, t - n_ctx, 256)
    tps = t // tm
    m = bsz * t
    p2 = p.reshape(m, -1)
    before, after = _halo_specs(tm, m, ch, col0 // ch)
    w_pad = jnp.concatenate([conv_w, jnp.zeros((-taps % 8, ch), F32)], axis=0)
    const = lambda a: pl.BlockSpec(a.shape, lambda i: (0,) * a.ndim)
    out = pl.pallas_call(
        functools.partial(_conv_kernel, tiles_per_sample=tps, ctx_tiles=n_ctx // tm),
        grid=(m // tm,),
        in_specs=[pl.BlockSpec((tm, ch), lambda i: (i, col0 // ch)), before, after, const(w_pad),
                  pl.BlockSpec((1, ch), lambda i: (0, 0))],
        out_specs=pl.BlockSpec((tm, ch), lambda i: (i, 0)),
        out_shape=jax.ShapeDtypeStruct((m, ch), ACT),
        compiler_params=pltpu.CompilerParams(dimension_semantics=("parallel",), vmem_limit_bytes=VMEM_LIMIT),
        name="conv_silu",
    )(p2, p2, p2, w_pad, conv_b.reshape(1, ch))
    return out.reshape(bsz, t, ch)


def _ssd_kernel(xf_ref, dtf_ref, xb_ref, dtb_ref, bias_ref, arow_ref, tabw_ref, tabx_ref, yf_ref, yb_ref,
                s_ref, ct_ref, cp_ref, *, inner, heads_per_group):
    L, hp = SSD_CHUNK, SSM_HEAD_DIM
    e_heads = heads_per_group
    width = e_heads * hp
    gn = SSM_GROUPS * SSM_STATE
    n_heads = SSM_GROUPS * e_heads

    @pl.when(pl.program_id(1) == 0)
    def _():
        s_ref[...] = jnp.zeros_like(s_ref)

    rl = lax.broadcasted_iota(jnp.int32, (L, L), 0)
    cl = lax.broadcasted_iota(jnp.int32, (L, L), 1)
    upto = (rl >= cl, rl <= cl)
    dirs = ((xf_ref, dtf_ref, yf_ref), (xb_ref, dtb_ref, yb_ref))
    for d, (_, dt_ref, _) in enumerate(dirs):
        dt_all = jax.nn.softplus(dt_ref[0].astype(F32) + bias_ref[...])
        cum = _dot(upto[d].astype(F32), dt_all * arow_ref[...], HIGHEST)
        ct_ref[d] = cum.T
        cp_ref[d] = jnp.concatenate(_bf16_pieces(cum, 3) + _bf16_pieces(dt_all, 2), axis=1)

    lane = lax.broadcasted_iota(jnp.int32, (L, LANES), 1)

    def group(g, carry):
        two = range(2)
        xcol = pl.ds(pl.multiple_of(g * width, width), width)
        bcol = pl.ds(pl.multiple_of(inner + g * SSM_STATE, SSM_STATE), SSM_STATE)
        ccol = pl.ds(pl.multiple_of(inner + gn + g * SSM_STATE, SSM_STATE), SSM_STATE)
        x = [dirs[d][0][0, :, xcol].astype(F32) for d in two]
        bm = [dirs[d][0][0, :, bcol].astype(F32) for d in two]
        cmb = [dirs[d][0][0, :, ccol].astype(BF16) for d in two]
        tabw = [tabw_ref[d, g] for d in two]
        tabx = [tabx_ref[d, g] for d in two]
        cum_w = [_dot(cp_ref[d, :, :3 * LANES], tabw[d]) for d in two]
        dt_x = [_dot(cp_ref[d, :, 3 * LANES:], tabx[d]) for d in two]
        s0 = [s_ref[d, g] for d in two]
        cb = [_dot_nt(cmb[d], bm[d].astype(BF16)) for d in two]
        y_off = [_dot(cmb[d], s0[d].astype(BF16)) for d in two]
        bt = [bm[d].T.astype(BF16) for d in two]
        cum_x = [jnp.concatenate(
            [jnp.where(lane < hp, cum_w[d][:, (2 * q) * LANES:(2 * q + 1) * LANES],
                       cum_w[d][:, (2 * q + 1) * LANES:(2 * q + 2) * LANES]) for q in range(e_heads // 2)],
            axis=1) for d in two]
        tot_x = [cum_x[d][0:1] if d == 1 else cum_x[d][L - 1:L] for d in two]
        xdt = [x[d] * dt_x[d] for d in two]
        cbm = [jnp.where(upto[d], cb[d], 0.0) for d in two]
        lhs, rhs = [], []
        for d in two:
            for q in range(e_heads // 2):
                mats = []
                for e in (2 * q, 2 * q + 1):
                    row = d * n_heads + g * e_heads + e
                    seg = cum_w[d][:, e * LANES:(e + 1) * LANES] - ct_ref[d, pl.ds(row, 1), :]
                    mats.append((cbm[d] * jnp.exp(jnp.minimum(seg, 0.0))).astype(BF16))
                slab = xdt[d][:, q * LANES:(q + 1) * LANES]
                lhs.append(jnp.concatenate(mats, axis=1))
                rhs.append(jnp.concatenate([jnp.where(lane < hp, slab, 0.0), jnp.where(lane < hp, 0.0, slab)],
                                           axis=0).astype(BF16))
        y_diag = [_dot(a, b) for a, b in zip(lhs, rhs)]
        s_add = [_dot(bt[d], (xdt[d] * jnp.exp(tot_x[d] - cum_x[d])).astype(BF16)) for d in two]
        for d in two:
            slabs = y_diag[d * (e_heads // 2):(d + 1) * (e_heads // 2)]
            y = y_off[d] * jnp.exp(cum_x[d]) + jnp.concatenate(slabs, axis=1)
            dirs[d][2][0, :, xcol] = y.astype(dirs[d][2].dtype)
            s_ref[d, g] = s0[d] * jnp.exp(tot_x[d]) + s_add[d]
        return carry

    lax.fori_loop(0, SSM_GROUPS, group, 0, unroll=True)


def ssd(xbc, p_c, dt_col, dt_bias, a_log, n_ctx, inner):
    bsz, t, xw = xbc.shape
    L = SSD_CHUNK
    n_heads = dt_bias.shape[1]
    e_heads = n_heads // SSM_GROUPS
    width = e_heads * SSM_HEAD_DIM
    n_cc, n_chunks = n_ctx // L, t // L
    pad = LANES - 2 * n_heads
    bias = jnp.concatenate([dt_bias[0], dt_bias[1], jnp.zeros((pad,), F32)]).reshape(1, LANES)
    arow = jnp.concatenate([-jnp.exp(a_log[0]), -jnp.exp(a_log[1]), jnp.zeros((pad,), F32)]).reshape(1, LANES)
    head_row = (jnp.arange(2)[:, None, None, None] * n_heads + jnp.arange(SSM_GROUPS)[None, :, None, None] * e_heads)
    src = jnp.arange(LANES)[None, None, :, None]
    tabw = (src == head_row + jnp.arange(e_heads * LANES)[None, None, None, :] // LANES).astype(BF16)
    tabx = (src == head_row + jnp.arange(width)[None, None, None, :] // SSM_HEAD_DIM).astype(BF16)
    tabw = jnp.concatenate([tabw] * 3, axis=2)
    tabx = jnp.concatenate([tabx] * 2, axis=2)

    mirrored = lambda s: _mirrored_chunk(s, n_cc, n_chunks)
    const = lambda a: pl.BlockSpec(a.shape, lambda i, s: (0,) * a.ndim)
    return pl.pallas_call(
        functools.partial(_ssd_kernel, inner=inner, heads_per_group=e_heads),
        grid=(bsz, n_chunks),
        in_specs=[
            pl.BlockSpec((1, L, xw), lambda i, s: (i, s, 0)),
            pl.BlockSpec((1, L, LANES), lambda i, s: (i, s, dt_col // LANES)),
            pl.BlockSpec((1, L, xw), lambda i, s: (i, mirrored(s), 0)),
            pl.BlockSpec((1, L, LANES), lambda i, s: (i, mirrored(s), dt_col // LANES)),
            const(bias), const(arow), const(tabw), const(tabx),
        ],
        out_specs=[pl.BlockSpec((1, L, inner), lambda i, s: (i, s, 0)),
                   pl.BlockSpec((1, L, inner), lambda i, s: (i, mirrored(s), 0))],
        out_shape=[jax.ShapeDtypeStruct((bsz, t, inner), ACT)] * 2,
        scratch_shapes=[pltpu.VMEM((2, SSM_GROUPS, SSM_STATE, width), F32),
                        pltpu.VMEM((2, LANES, L), F32),
                        pltpu.VMEM((2, L, 5 * LANES), BF16)],
        compiler_params=pltpu.CompilerParams(dimension_semantics=("parallel", "arbitrary"),
                                             vmem_limit_bytes=VMEM_LIMIT),
        name="ssd",
    )(xbc, p_c, xbc, p_c, bias, arow, tabw, tabx)


def _merge_kernel(x_ref, yf_ref, yb_ref, r_ref, k_ref, v_ref, og_ref, z_ref, grw_ref, gssm_ref, gl_ref, gc_ref,
                  lnw_ref, lnb_ref, rk_ref, wrw_ref, wout_ref, o_ref, *, tm, tiles_per_sample, n_ctx):
    f32 = lambda ref: ref[...].astype(F32)
    y = f32(yf_ref) + f32(yb_ref)
    inv_n = 1.0 / RW_HEAD_DIM
    centred = y - _group_sums(y, RW_HEAD_DIM) * inv_n
    var = _group_sums(centred * centred, RW_HEAD_DIM) * inv_n
    y = centred * lax.rsqrt(var + RW_GN_EPS) * lnw_ref[...] + lnb_ref[...]
    y = y + _group_sums(f32(r_ref) * f32(k_ref) * rk_ref[...], RW_HEAD_DIM) * f32(v_ref)
    y_rw = y * og_ref[...]
    t1 = _dot(y_rw.astype(wrw_ref.dtype), wrw_ref[...])
    merged = jax.nn.sigmoid(f32(grw_ref)) * t1 + jax.nn.sigmoid(f32(gssm_ref)) * f32(z_ref)
    mix = _dot(merged.astype(wout_ref.dtype), wout_ref[...])
    gate = jnp.where(_ctx_rows(tm, tiles_per_sample, n_ctx), gc_ref[...], gl_ref[0])
    o_ref[...] = x_ref[...] + gate * mix


def merge(x, y_f, y_b, r, k, v, out_gate, z_ssm, proj, col_grw, col_gssm, gate_l, gate_c, ln_w, ln_b, r_k,
          w_rw, w_out, n_ctx):
    bsz, t, d = x.shape
    tm = _row_tile(t, 272)
    tps = t // tm
    m = bsz * t
    rows = lambda blk: pl.BlockSpec((tm, d), lambda i: (i, blk))
    const = lambda shape: pl.BlockSpec(shape, lambda i: (0,) * len(shape))
    flat = lambda a: a.reshape(m, -1)
    out = pl.pallas_call(
        functools.partial(_merge_kernel, tm=tm, tiles_per_sample=tps, n_ctx=n_ctx),
        grid=(m // tm,),
        in_specs=[rows(0)] * 8 + [rows(col_grw // d), rows(col_gssm // d),
                                  pl.BlockSpec((1, 1, d), lambda i: (i // tps, 0, 0)), const((1, d)),
                                  const((1, d)), const((1, d)), const((1, d)),
                                  const(w_rw.shape), const(w_out.shape)],
        out_specs=rows(0),
        out_shape=jax.ShapeDtypeStruct((m, d), F32),
        compiler_params=pltpu.CompilerParams(dimension_semantics=("parallel",), vmem_limit_bytes=VMEM_LIMIT),
        name="merge",
    )(flat(x), flat(y_f), flat(y_b), flat(r), flat(k), flat(v), flat(out_gate), flat(z_ssm), flat(proj), flat(proj),
      gate_l.reshape(bsz, 1, d), gate_c.reshape(1, d), ln_w.reshape(1, d), ln_b.reshape(1, d), r_k.reshape(1, d),
      w_rw, w_out)
    return out.reshape(bsz, t, d)


def _ssm_out_kernel(yf_ref, yb_ref, xs_ref, z_ref, dskip_ref, nw_ref, w_ref, o_ref, *, group):
    f32 = lambda ref: ref[...].astype(F32)
    y = (f32(yf_ref) + f32(yb_ref) + dskip_ref[...] * f32(xs_ref)) * jax.nn.silu(f32(z_ref))
    ms = _group_sums(y * y, group) * (1.0 / group)
    y = y * lax.rsqrt(ms + SSM_NORM_EPS) * nw_ref[...]
    o_ref[...] = _dot(y.astype(w_ref.dtype), w_ref[...]).astype(o_ref.dtype)


def ssm_out(y_f, y_b, xbc, p_c, z_col, d_skip, norm_w, w):
    bsz, t, inner = y_f.shape
    m = bsz * t
    d = w.shape[1]
    tm = _row_tile(t, 272)
    rows = lambda blk: pl.BlockSpec((tm, inner), lambda i: (i, blk))
    const = lambda shape: pl.BlockSpec(shape, lambda i: (0,) * len(shape))
    flat = lambda a: a.reshape(m, -1)
    out = pl.pallas_call(
        functools.partial(_ssm_out_kernel, group=inner // SSM_GROUPS),
        grid=(m // tm,),
        in_specs=[rows(0), rows(0), rows(0), rows(z_col // inner), const((1, inner)), const((1, inner)),
                  const(w.shape)],
        out_specs=pl.BlockSpec((tm, d), lambda i: (i, 0)),
        out_shape=jax.ShapeDtypeStruct((m, d), ACT),
        compiler_params=pltpu.CompilerParams(dimension_semantics=("parallel",), vmem_limit_bytes=VMEM_LIMIT),
        name="ssm_out",
    )(flat(y_f), flat(y_b), flat(xbc), flat(p_c), d_skip.reshape(1, inner), norm_w.reshape(1, inner), w)
    return out.reshape(bsz, t, d)


def _router_kernel(x_ref, g_ref, scl_ref, shl_ref, scc_ref, shc_ref, wr_ref, h_ref, logit_ref, *,
                   tm, tiles_per_sample, n_ctx):
    is_ctx = _ctx_rows(tm, tiles_per_sample, n_ctx)
    h = _modulated(x_ref[...], g_ref[...], scl_ref[0], shl_ref[0], scc_ref[...], shc_ref[...], is_ctx)
    bits = pltpu.bitcast(h.astype(BF16).astype(F32), jnp.uint32)
    half = h.shape[1] // 2
    h_ref[...] = (bits[:, :half] >> 16) | (bits[:, half:] & jnp.uint32(0xFFFF0000))
    logit_ref[...] = _dot(h, wr_ref[...], HIGHEST)


def router(x, gain, scale_l, shift_l, scale_c, shift_c, w_router_padded, n_ctx):
    bsz, t, d = x.shape
    tm = _row_tile(t, 544)
    tps = t // tm
    m = bsz * t
    row = lambda v: v.reshape(1, d)
    const = lambda shape: pl.BlockSpec(shape, lambda i: (0,) * len(shape))
    per_sample = pl.BlockSpec((1, 1, d), lambda i: (i // tps, 0, 0))
    return pl.pallas_call(
        functools.partial(_router_kernel, tm=tm, tiles_per_sample=tps, n_ctx=n_ctx),
        grid=(m // tm,),
        in_specs=[pl.BlockSpec((tm, d), lambda i: (i, 0)), const((1, d)), per_sample, per_sample,
                  const((1, d)), const((1, d)), const(w_router_padded.shape)],
        out_specs=[pl.BlockSpec((tm, d // 2), lambda i: (i, 0)), pl.BlockSpec((tm, LANES), lambda i: (i, 0))],
        out_shape=[jax.ShapeDtypeStruct((m, d // 2), jnp.uint32), jax.ShapeDtypeStruct((m, LANES), F32)],
        compiler_params=pltpu.CompilerParams(dimension_semantics=("parallel",), vmem_limit_bytes=VMEM_LIMIT),
        name="router",
    )(x.reshape(m, d), row(gain), scale_l.reshape(bsz, 1, d), shift_l.reshape(bsz, 1, d),
      row(scale_c), row(shift_c), w_router_padded)


def _experts_kernel(be_ref, nb_ref, x_ref, w1_ref, w3_ref, w2_ref, o_ref, w1b_ref, w3b_ref, w2b_ref):
    i = pl.program_id(0)

    @pl.when((i == 0) | (be_ref[i] != be_ref[jnp.maximum(i - 1, 0)]))
    def _():
        w1b_ref[...] = w1_ref[0, 0].astype(BF16)
        w3b_ref[...] = w3_ref[0, 0].astype(BF16)
        w2b_ref[...] = w2_ref[0, 0].astype(BF16)

    @pl.when(i < nb_ref[0])
    def _():
        packed = x_ref[...]
        x = jnp.concatenate([pltpu.bitcast(packed << 16, F32),
                             pltpu.bitcast(packed & jnp.uint32(0xFFFF0000), F32)], axis=1).astype(BF16)
        hidden = jax.nn.silu(_dot(x, w1b_ref[...])) * _dot(x, w3b_ref[...])
        o_ref[...] = _dot(hidden.astype(BF16), w2b_ref[...])

    @pl.when(i >= nb_ref[0])
    def _():
        o_ref[...] = jnp.zeros_like(o_ref)


def experts(xb, block_expert, n_used, w1, w3, w2, layer):
    n_rows = xb.shape[0]
    d, de = w1.shape[2:]
    rows = MOE_ROWS
    n_blocks = n_rows // rows
    return pl.pallas_call(
        _experts_kernel,
        grid_spec=pltpu.PrefetchScalarGridSpec(
            num_scalar_prefetch=2,
            grid=(n_blocks,),
            in_specs=[
                pl.BlockSpec((rows, d // 2), lambda i, be, nb: (i, 0)),
                pl.BlockSpec((1, 1, d, de), lambda i, be, nb: (layer, be[i], 0, 0)),
                pl.BlockSpec((1, 1, d, de), lambda i, be, nb: (layer, be[i], 0, 0)),
                pl.BlockSpec((1, 1, de, d), lambda i, be, nb: (layer, be[i], 0, 0)),
            ],
            out_specs=pl.BlockSpec((rows, d), lambda i, be, nb: (i, 0)),
            scratch_shapes=[pltpu.VMEM((d, de), BF16), pltpu.VMEM((d, de), BF16), pltpu.VMEM((de, d), BF16)],
        ),
        out_shape=jax.ShapeDtypeStruct((n_rows, d), F32),
        compiler_params=pltpu.CompilerParams(dimension_semantics=("arbitrary",), vmem_limit_bytes=VMEM_LIMIT),
        name="experts",
    )(block_expert, n_used, xb, w1, w3, w2)


def _combine_kernel(x_ref, y_ref, gate_ref, gl_ref, gc_ref, fin_ref, o_ref, *, tm, tiles_per_sample, n_ctx, final):
    d = x_ref.shape[1]
    g = gate_ref[...]
    f = y_ref[:, :d] * g[:, 0:1] + y_ref[:, d:] * g[:, 1:2]
    mod = jnp.where(_ctx_rows(tm, tiles_per_sample, n_ctx), gc_ref[...], gl_ref[0])
    out = x_ref[...] + mod * f
    if final:
        out = out * lax.rsqrt(jnp.mean(out * out, axis=-1, keepdims=True) + NORM_EPS) * fin_ref[...]
    o_ref[...] = out


def combine(x, y_pairs, gates, gate_l, gate_c, final_gain, n_ctx, final):
    bsz, t, d = x.shape
    m = bsz * t
    tm = _row_tile(t, 544)
    tps = t // tm
    const = lambda shape: pl.BlockSpec(shape, lambda i: (0,) * len(shape))
    out = pl.pallas_call(
        functools.partial(_combine_kernel, tm=tm, tiles_per_sample=tps, n_ctx=n_ctx, final=final),
        grid=(m // tm,),
        in_specs=[pl.BlockSpec((tm, d), lambda i: (i, 0)), pl.BlockSpec((tm, TOP_K * d), lambda i: (i, 0)),
                  pl.BlockSpec((tm, LANES), lambda i: (i, 0)), pl.BlockSpec((1, 1, d), lambda i: (i // tps, 0, 0)),
                  const((1, d)), const((1, d))],
        out_specs=pl.BlockSpec((tm, d), lambda i: (i, 0)),
        out_shape=jax.ShapeDtypeStruct((m, d), F32),
        compiler_params=pltpu.CompilerParams(dimension_semantics=("parallel",), vmem_limit_bytes=VMEM_LIMIT),
        name="combine",
    )(x.reshape(m, d), y_pairs.reshape(m, TOP_K * d), gates, gate_l.reshape(bsz, 1, d), gate_c.reshape(1, d),
      final_gain.reshape(1, d))
    return out.reshape(bsz, t, d)


def _top2(vals):
    idx = jnp.arange(vals.shape[-1], dtype=jnp.int32)
    i1 = jnp.argmax(vals, axis=-1).astype(jnp.int32)
    v1 = jnp.max(vals, axis=-1)
    rest = jnp.where(idx == i1[..., None], -jnp.inf, vals)
    i2 = jnp.argmax(rest, axis=-1).astype(jnp.int32)
    v2 = jnp.max(rest, axis=-1)
    return v1, i1, v2, i2


def moe(h, logits, b_router, w1, w3, w2, layer):
    n_tok, d = h.shape
    n_exp = b_router.shape[0]
    epg = n_exp // N_EXPERT_GROUPS
    scores = jax.nn.sigmoid(logits)
    biased = (scores + b_router.astype(F32)).reshape(n_tok, N_EXPERT_GROUPS, epg)
    g1, _, g2, _ = _top2(biased)
    top_group = jnp.argmax(g1 + g2, axis=-1).astype(jnp.int32)
    in_top = jnp.arange(N_EXPERT_GROUPS, dtype=jnp.int32)[None, :, None] == top_group[:, None, None]
    in_group = jnp.sum(jnp.where(in_top, biased, 0.0), axis=1)
    _, l1, _, l2 = _top2(in_group)
    expert = top_group[:, None] * epg + jnp.stack([l1, l2], axis=-1)
    picked = expert[:, :, None] == jnp.arange(n_exp, dtype=jnp.int32)[None, None, :]
    gate = jnp.sum(jnp.where(picked, scores[:, None, :], 0.0), axis=-1)
    gate = gate / jnp.sum(gate, axis=-1, keepdims=True)

    n_assign = n_tok * TOP_K
    flat_e = expert.reshape(-1).astype(jnp.int32)
    onehot = picked.reshape(n_assign, n_exp).astype(jnp.int32)
    rank = jnp.sum((jnp.cumsum(onehot, axis=0) - onehot) * onehot, axis=-1)
    counts = jnp.sum(onehot, axis=0)
    padded = (counts + MOE_ROWS - 1) // MOE_ROWS * MOE_ROWS
    pad_end = jnp.cumsum(padded)
    dest = ((pad_end - padded)[flat_e] + rank).astype(jnp.int32)
    n_blocks = -(-n_assign // MOE_ROWS) + n_exp
    slot_token = jnp.zeros((n_blocks * MOE_ROWS,), jnp.int32).at[dest].set(
        jnp.arange(n_assign, dtype=jnp.int32) // TOP_K)
    block_expert = jnp.clip(
        jnp.sum(jnp.arange(n_blocks, dtype=jnp.int32)[:, None] >= (pad_end // MOE_ROWS)[None, :], axis=-1),
        0, n_exp - 1).astype(jnp.int32)
    n_used = (pad_end[-1:] // MOE_ROWS).astype(jnp.int32)
    yb = experts(h[slot_token], block_expert, n_used, w1, w3, w2, layer)
    return yb[dest], jnp.pad(gate, ((0, 0), (0, LANES - TOP_K)))


def _ssm_branch(p, prm, w_branch, n_ctx):
    inner = prm["norm_w"].shape[0]
    xbc_w = prm["conv_w"].shape[1]
    xbc = conv_silu(p, 0, prm["conv_w"], prm["conv_b"], n_ctx)
    y_f, y_b = ssd(xbc, p, xbc_w + inner, prm["dt_bias"], prm["a_log"], n_ctx, inner)
    d_skip = jnp.repeat(prm["d"][0] + prm["d"][1], SSM_HEAD_DIM)
    return ssm_out(y_f, y_b, xbc, p, xbc_w, d_skip, prm["norm_w"], w_branch)


def _pad_cols(w, n):
    return jnp.pad(w, ((0, 0), (0, n - w.shape[1])))


def kernel(x, c, ctx, c_ctx, w_mod, b_mod, norm_mix_g, w_in, rw_shift_mu, rw_w0, rw_w2, rw_a0, rw_a2, rw_g2,
           rw_k_k, rw_k_a, rw_r_k, rw_ln_w, rw_ln_b, ssm_conv_w, ssm_conv_b, ssm_dt_bias, ssm_a_log, ssm_d,
           ssm_norm_w, w_branch_rw, w_branch_ssm, w_out, norm_ffn_g, w_router, b_router, exp_w1, exp_w3, exp_w2,
           norm_final_g):
    bsz, n_lat, d = x.shape
    depth = w_in.shape[0]
    n_ctx = ctx.shape[1]
    rows = n_lat // GRID_W
    t = n_ctx + n_lat
    width = RW_HEADS * RW_HEAD_DIM
    dl, al, gl = rw_w2.shape[2], rw_a2.shape[2], rw_g2.shape[1]
    inner = ssm_norm_w.shape[1]
    xbc_w = ssm_conv_w.shape[2]
    n_heads = ssm_dt_bias.shape[2]
    n_exp = w_router.shape[1]

    def to_c(a):
        ch = a.shape[-1]
        lat = a[:, n_ctx:].reshape(bsz, rows, GRID_W, ch).transpose(0, 2, 1, 3).reshape(bsz, n_lat, ch)
        return jnp.concatenate([a[:, :n_ctx], lat], axis=1)

    def to_r(a):
        ch = a.shape[-1]
        lat = a[:, n_ctx:].reshape(bsz, GRID_W, rows, ch).transpose(0, 2, 1, 3).reshape(bsz, n_lat, ch)
        return jnp.concatenate([a[:, :n_ctx], lat], axis=1)

    o = 0
    src = {}
    for name, size in (("r", width), ("w_f", dl), ("w_b", dl), ("k", width), ("v", width), ("a_f", al),
                       ("a_b", al), ("g", gl), ("z", inner), ("xbc", xbc_w), ("dt", 2 * n_heads),
                       ("g_rw", d), ("g_ssm", d)):
        src[name] = (o, size)
        o += size
    take = lambda a, name: a[..., src[name][0]:src[name][0] + src[name][1]]
    lora_w = 2 * dl + 2 * al
    lora_pad = -(-lora_w // LANES) * LANES
    g_pad = -(-gl // LANES) * LANES
    dt_pad = -(-2 * n_heads // LANES) * LANES
    lay = {"r": (0, width), "k": (width, width), "v": (2 * width, width), "g_rw": (3 * width, d),
           "g_ssm": (3 * width + d, d), "lora": (3 * width + 2 * d, lora_w),
           "g": (3 * width + 2 * d + lora_pad, gl)}

    silu_c = jax.nn.silu(c)
    silu_cc = jax.nn.silu(c_ctx)[None, :]
    act = jnp.concatenate([silu_c, silu_cc, jnp.zeros((-(bsz + 1) % 8, d), F32)], axis=0)
    w_router_p = _pad_cols(w_router.astype(F32), LANES)

    xr = jnp.concatenate([ctx, x], axis=1)
    for l in range(depth):
        mod = matmul(act, w_mod[l], precision=HIGHEST, name="modulation")[:bsz + 1] + b_mod[l]
        shift_m, scale_m, gate_m, shift_f, scale_f, gate_f = jnp.split(mod[:bsz], 6, axis=-1)
        cshift_m, cscale_m, cgate_m, cshift_f, cscale_f, cgate_f = jnp.split(mod[bsz], 6, axis=-1)

        wl = w_in[l].astype(BF16)
        w_r = jnp.concatenate(
            [take(wl, "r"), take(wl, "k"), take(wl, "v"), take(wl, "g_rw"), take(wl, "g_ssm"),
             _pad_cols(jnp.concatenate([take(wl, n) for n in ("w_f", "w_b", "a_f", "a_b")], axis=1), lora_pad),
             _pad_cols(take(wl, "g"), g_pad)], axis=1)
        w_c = jnp.concatenate([take(wl, "xbc"), take(wl, "z"), _pad_cols(take(wl, "dt"), dt_pad)], axis=1)
        mods = (norm_mix_g[l], scale_m, shift_m, cscale_m, cshift_m)
        h_mix = mod_norm(xr, *mods, n_ctx)
        p_r = project(h_mix, w_r, ACT)
        p_c = project(to_c(h_mix), w_c, ACT)

        mu = rw_shift_mu[l]
        rw_prm = dict(
            shift_mu={"r": take(mu, "r"), "k": take(mu, "k"), "v": take(mu, "v"), "g": take(mu, "g"),
                      "lora": jnp.concatenate([take(mu, n) for n in ("w_f", "w_b", "a_f", "a_b")])},
            w0=rw_w0[l], w2=rw_w2[l], a0=rw_a0[l], a2=rw_a2[l], g2=rw_g2[l],
            k_k=rw_k_k[l], k_a=rw_k_a[l], r_k=rw_r_k[l], ln_w=rw_ln_w[l], ln_b=rw_ln_b[l])
        r, k, v, kk, lw_f, lw_b, ar_f, ar_b, out_gate = rwkv_prep(p_r, lay, rw_prm, n_ctx)
        y_f, y_b = wkv7(r, k, v, kk, (lw_f, lw_b), (ar_f, ar_b), rw_k_a[l], n_ctx)
        ssm_prm = dict(conv_w=ssm_conv_w[l], conv_b=ssm_conv_b[l], dt_bias=ssm_dt_bias[l], a_log=ssm_a_log[l],
                       d=ssm_d[l], norm_w=ssm_norm_w[l])
        z_ssm = to_r(_ssm_branch(p_c, ssm_prm, w_branch_ssm[l].astype(BF16), n_ctx))
        xr = merge(xr, y_f, y_b, r, k, v, out_gate, z_ssm, p_r, lay["g_rw"][0], lay["g_ssm"][0], gate_m, cgate_m,
                   rw_ln_w[l], rw_ln_b[l], rw_r_k[l], w_branch_rw[l].astype(BF16), w_out[l].astype(BF16), n_ctx)

        h, logits = router(xr, norm_ffn_g[l], scale_f, shift_f, cscale_f, cshift_f, w_router_p, n_ctx)
        y_pairs, gates = moe(h, logits[:, :n_exp], b_router, exp_w1, exp_w3, exp_w2, l)
        xr = combine(xr, y_pairs, gates, gate_f, cgate_f, norm_final_g, n_ctx, final=(l == depth - 1))
    return xr[:, n_ctx:]
```
